```python
import jax, jax.numpy as jnp
from jax import lax
import numpy as np


D_MODEL = 1024
BATCH = 8
SEQ = 8192
DEPTH = 1

N_Q_HEADS = 16
N_KV_HEADS = 4
HEAD_DIM = 64
WINDOW = 128
ATTN_BLOCK = 128
ROPE_THETA = 500000.0
ROPE_DIM = HEAD_DIM // 4
HGRN_HEADS = 8
HGRN_DK = 128
HGRN_DV = 128
HGRN_CHUNK = 64
D_FF = 2816
ATTN_WIDTH = N_Q_HEADS * HEAD_DIM
KV_WIDTH = N_KV_HEADS * HEAD_DIM
HGRN_KWIDTH = HGRN_HEADS * HGRN_DK
HGRN_VWIDTH = HGRN_HEADS * HGRN_DV
IN_WIDTHS = (ATTN_WIDTH, KV_WIDTH, KV_WIDTH, HGRN_KWIDTH, HGRN_KWIDTH, HGRN_VWIDTH, HGRN_VWIDTH, D_MODEL, D_MODEL)
D_IN = sum(IN_WIDTHS)
DEEPNORM_ALPHA = (2 * DEPTH) ** 0.25
DEEPNORM_BETA = (8 * DEPTH) ** -0.25
LN_EPS = 1e-5
RMS_EPS = 1e-6
NEG_INF = -1e30

kernel_name = 'hybrid_swa_sink_hgrn2_macaron_deepnorm'


def layer_norm(x, g, b):
    xf = x.astype(jnp.float32)
    mu = jnp.mean(xf, axis=-1, keepdims=True)
    var = jnp.mean(jnp.square(xf - mu), axis=-1, keepdims=True)
    y = (xf - mu) * lax.rsqrt(var + LN_EPS) * g.astype(jnp.float32) + b.astype(jnp.float32)
    return y.astype(x.dtype)


def swiglu(x, w1, w3, w2):
    return (jax.nn.silu(x @ w1) * (x @ w3)) @ w2


def rope_tables(seq_len):
    pos = jnp.arange(seq_len, dtype=jnp.float32)
    inv_freq = ROPE_THETA ** (-jnp.arange(0, ROPE_DIM, 2, dtype=jnp.float32) / ROPE_DIM)
    ang = pos[:, None] * inv_freq[None, :]
    return jnp.cos(ang)[None, :, None, :], jnp.sin(ang)[None, :, None, :]


def partial_rope(t, cos, sin):
    tf = t.astype(jnp.float32)
    half = ROPE_DIM // 2
    t1, t2, rest = tf[..., :half], tf[..., half:ROPE_DIM], tf[..., ROPE_DIM:]
    rot = jnp.concatenate([t1 * cos - t2 * sin, t2 * cos + t1 * sin, rest], axis=-1)
    return rot.astype(t.dtype)


def sliding_window_attention(q, k, v, sinks):
    B, S = q.shape[0], q.shape[1]
    nb = S // ATTN_BLOCK
    grp = N_Q_HEADS // N_KV_HEADS
    qb = q.reshape(B, nb, ATTN_BLOCK, N_KV_HEADS, grp, HEAD_DIM)

    def band(t):
        tp = jnp.pad(t, ((0, 0), (ATTN_BLOCK, 0), (0, 0), (0, 0)))
        tp = tp.reshape(B, nb + 1, ATTN_BLOCK, N_KV_HEADS, HEAD_DIM)
        return jnp.concatenate([tp[:, :-1], tp[:, 1:]], axis=2)

    kb, vb = band(k), band(v)
    scores = jnp.einsum('bnqkgd,bnskd->bnkgqs', qb, kb).astype(jnp.float32) * (HEAD_DIM ** -0.5)
    qi = jnp.arange(ATTN_BLOCK)[:, None]
    kj = jnp.arange(2 * ATTN_BLOCK)[None, :]
    blk = jnp.arange(nb)[:, None, None]
    dist = qi + ATTN_BLOCK - kj
    mask = (dist >= 0) & (dist < WINDOW) & (blk * ATTN_BLOCK + kj - ATTN_BLOCK >= 0)
    scores = jnp.where(mask[None, :, None, None], scores, NEG_INF)
    sink = sinks.astype(jnp.float32).reshape(1, 1, N_KV_HEADS, grp, 1, 1)
    m = jnp.maximum(jnp.max(scores, axis=-1, keepdims=True), sink)
    p = jnp.exp(scores - m)
    denom = jnp.sum(p, axis=-1, keepdims=True) + jnp.exp(sink - m)
    probs = (p / denom).astype(v.dtype)
    out = jnp.einsum('bnkgqs,bnskd->bnqkgd', probs, vb)
    return out.reshape(B, S, ATTN_WIDTH)


def hgrn2_recurrence(q, f_logit, v, lb):
    B, S = q.shape[0], q.shape[1]
    nc = S // HGRN_CHUNK
    lb = lb.reshape(HGRN_HEADS, HGRN_DK)
    f = lb + (1.0 - lb) * jax.nn.sigmoid(f_logit.astype(jnp.float32))
    k = 1.0 - f

    def chunks(t):
        return t.reshape(B, nc, HGRN_CHUNK, HGRN_HEADS, t.shape[-1]).transpose(0, 3, 1, 2, 4)

    qc, kc, vc = chunks(q.astype(jnp.float32)), chunks(k), chunks(v.astype(jnp.float32))
    gc = jnp.cumsum(chunks(jnp.log(f)), axis=3)
    g_last = gc[:, :, :, -1:, :]
    q_dec = qc * jnp.exp(gc)
    k_inv = kc * jnp.exp(-gc)
    k_end = kc * jnp.exp(g_last - gc)
    causal = jnp.tril(jnp.ones((HGRN_CHUNK, HGRN_CHUNK), dtype=bool))
    scores = jnp.where(causal, jnp.einsum('bhntd,bhnsd->bhnts', q_dec, k_inv), 0.0)
    o_intra = jnp.einsum('bhnts,bhnse->bhnte', scores, vc)
    upd = jnp.einsum('bhnsd,bhnse->bhnde', k_end, vc)
    decay = jnp.exp(g_last[:, :, :, 0, :])

    def step(state, inp):
        a_n, u_n = inp
        return state * a_n[..., None] + u_n, state

    s0 = jnp.zeros((B, HGRN_HEADS, HGRN_DK, HGRN_DV), jnp.float32)
    _, s_start = lax.scan(step, s0, (jnp.moveaxis(decay, 2, 0), jnp.moveaxis(upd, 2, 0)))
    s_start = jnp.moveaxis(s_start, 0, 2)
    o = o_intra + jnp.einsum('bhntd,bhnde->bhnte', q_dec, s_start)
    return o.transpose(0, 2, 3, 1, 4).reshape(B, S, HGRN_HEADS, HGRN_DV)


def token_mixer(h, w_in, b_in, sinks, lb, norm_g, w_pa, w_ph, w_out, cos, sin):
    B, S = h.shape[0], h.shape[1]
    proj = h @ w_in + b_in
    splits = np.cumsum(IN_WIDTHS)[:-1].tolist()
    q_a, k_a, v_a, f_h, q_h, i_h, og_h, gate_a, gate_h = jnp.split(proj, splits, axis=-1)
    q_a = partial_rope(q_a.reshape(B, S, N_Q_HEADS, HEAD_DIM), cos, sin)
    k_a = partial_rope(k_a.reshape(B, S, N_KV_HEADS, HEAD_DIM), cos, sin)
    v_a = v_a.reshape(B, S, N_KV_HEADS, HEAD_DIM)
    y_attn = sliding_window_attention(q_a, k_a, v_a, sinks)
    q_h = jax.nn.silu(q_h).reshape(B, S, HGRN_HEADS, HGRN_DK)
    o_h = hgrn2_recurrence(q_h, f_h.reshape(B, S, HGRN_HEADS, HGRN_DK),
                           i_h.reshape(B, S, HGRN_HEADS, HGRN_DV), lb)
    o_h = o_h * lax.rsqrt(jnp.mean(jnp.square(o_h), axis=-1, keepdims=True) + RMS_EPS) * norm_g.astype(jnp.float32)
    y_hgrn = o_h.reshape(B, S, HGRN_VWIDTH).astype(h.dtype) * jax.nn.silu(og_h)
    merged = jax.nn.sigmoid(gate_a) * (y_attn @ w_pa) + jax.nn.sigmoid(gate_h) * (y_hgrn @ w_ph)
    return merged @ w_out


def setup_inputs(seed: int = 0) -> dict:
    key = jax.random.key(seed)
    ks = jax.random.split(key, 24)

    def nrm(k, shape, scale):
        return jax.random.normal(k, shape, jnp.float32) * scale

    D = D_MODEL
    return {
        'x': nrm(ks[0], (BATCH, SEQ, D), 1.0),
        'ln1_g': 1.0 + nrm(ks[1], (DEPTH, D), 0.02),
        'ln1_b': nrm(ks[2], (DEPTH, D), 0.02),
        'ffn1_w1': nrm(ks[3], (DEPTH, D, D_FF), D ** -0.5),
        'ffn1_w3': nrm(ks[4], (DEPTH, D, D_FF), D ** -0.5),
        'ffn1_w2': nrm(ks[5], (DEPTH, D_FF, D), D_FF ** -0.5 * DEEPNORM_BETA),
        'ln2_g': 1.0 + nrm(ks[6], (DEPTH, D), 0.02),
        'ln2_b': nrm(ks[7], (DEPTH, D), 0.02),
        'w_in': nrm(ks[8], (DEPTH, D, D_IN), D ** -0.5),
        'b_in': nrm(ks[9], (DEPTH, D_IN), 0.02),
        'attn_sinks': nrm(ks[10], (DEPTH, N_Q_HEADS), 0.5),
        'hgrn_lb_logits': nrm(ks[11], (DEPTH + 1, HGRN_KWIDTH), 0.1),
        'hgrn_norm_g': 1.0 + nrm(ks[12], (DEPTH, HGRN_DV), 0.02),
        'w_proj_attn': nrm(ks[13], (DEPTH, ATTN_WIDTH, D), ATTN_WIDTH ** -0.5 * DEEPNORM_BETA),
        'w_proj_hgrn': nrm(ks[14], (DEPTH, HGRN_VWIDTH, D), HGRN_VWIDTH ** -0.5 * DEEPNORM_BETA),
        'w_out': nrm(ks[15], (DEPTH, D, D), D ** -0.5 * DEEPNORM_BETA),
        'ln3_g': 1.0 + nrm(ks[16], (DEPTH, D), 0.02),
        'ln3_b': nrm(ks[17], (DEPTH, D), 0.02),
        'ffn2_w1': nrm(ks[18], (DEPTH, D, D_FF), D ** -0.5),
        'ffn2_w3': nrm(ks[19], (DEPTH, D, D_FF), D ** -0.5),
        'ffn2_w2': nrm(ks[20], (DEPTH, D_FF, D), D_FF ** -0.5 * DEEPNORM_BETA),
    }


def reference(x, ln1_g, ln1_b, ffn1_w1, ffn1_w3, ffn1_w2, ln2_g, ln2_b, w_in, b_in,
              attn_sinks, hgrn_lb_logits, hgrn_norm_g, w_proj_attn, w_proj_hgrn, w_out,
              ln3_g, ln3_b, ffn2_w1, ffn2_w3, ffn2_w2):
    cos, sin = rope_tables(x.shape[1])
    lb_all = jnp.cumsum(jax.nn.softmax(hgrn_lb_logits.astype(jnp.float32), axis=0), axis=0)
    for l in range(DEPTH):
        x = layer_norm(DEEPNORM_ALPHA * x + 0.5 * swiglu(x, ffn1_w1[l], ffn1_w3[l], ffn1_w2[l]),
                       ln1_g[l], ln1_b[l])
        mix = token_mixer(x, w_in[l], b_in[l], attn_sinks[l], lb_all[l], hgrn_norm_g[l],
                          w_proj_attn[l], w_proj_hgrn[l], w_out[l], cos, sin)
        x = layer_norm(DEEPNORM_ALPHA * x + mix, ln2_g[l], ln2_b[l])
        x = layer_norm(DEEPNORM_ALPHA * x + 0.5 * swiglu(x, ffn2_w1[l], ffn2_w3[l], ffn2_w2[l]),
                       ln3_g[l], ln3_b[l])
    return x
```

```python
import functools

import jax
import jax.numpy as jnp
import numpy as np
from jax import lax
from jax.experimental import pallas as pl
from jax.experimental.pallas import tpu as pltpu

D_MODEL = 1024
DEPTH = 1
N_Q_HEADS = 16
N_KV_HEADS = 4
HEAD_DIM = 64
GQA_GROUP = N_Q_HEADS // N_KV_HEADS
WINDOW = 128
ATTN_BLOCK = 128
ROPE_THETA = 500000.0
ROPE_DIM = HEAD_DIM // 4
HGRN_HEADS = 8
HGRN_DK = 128
HGRN_DV = 128
HGRN_CHUNK = 64
D_FF = 2816
ATTN_WIDTH = N_Q_HEADS * HEAD_DIM
KV_WIDTH = N_KV_HEADS * HEAD_DIM
HGRN_WIDTH = HGRN_HEADS * HGRN_DK
D_IN = ATTN_WIDTH + 2 * KV_WIDTH + 4 * HGRN_WIDTH + 2 * D_MODEL
DEEPNORM_ALPHA = (2 * DEPTH) ** 0.25
LN_EPS = 1e-5
RMS_EPS = 1e-6
NEG_INF = -1e30

LANES = 128
V7X_VMEM_BYTES = 64 * 1024 * 1024
VMEM_LIMIT_BYTES = 56 * 1024 * 1024

F32 = jnp.float32
BF16 = jnp.bfloat16


def _sigmoid(x):
    return 0.5 * jnp.tanh(0.5 * x) + 0.5


def _layer_norm(y, g, b):
    mu = jnp.mean(y, axis=-1, keepdims=True)
    yc = y - mu
    var = jnp.mean(yc * yc, axis=-1, keepdims=True)
    return yc * lax.rsqrt(var + LN_EPS) * g + b


def _resident(shape):
    return pl.BlockSpec(shape, lambda *_: (0,) * len(shape), pipeline_mode=pl.Buffered(1))


def _params(*semantics):
    return pltpu.CompilerParams(dimension_semantics=semantics, vmem_limit_bytes=VMEM_LIMIT_BYTES)


def _ffn_ln_body(x_ref, w1_ref, w3_ref, w2_ref, g_ref, b_ref, o_ref, acc_ref, *, ff_chunk):
    x = x_ref[...]
    xb = x.astype(BF16)
    for c in range(D_FF // ff_chunk):
        cols = slice(c * ff_chunk, (c + 1) * ff_chunk)
        h1 = jnp.dot(xb, w1_ref[:, cols], preferred_element_type=F32)
        h3 = jnp.dot(xb, w3_ref[:, cols], preferred_element_type=F32)
        a = (h1 * _sigmoid(h1) * h3).astype(BF16)
        part = jnp.dot(a, w2_ref[cols, :], preferred_element_type=F32)
        if c == 0:
            acc_ref[...] = part
        else:
            acc_ref[...] += part
    y = DEEPNORM_ALPHA * x + 0.5 * acc_ref[...]
    o_ref[...] = _layer_norm(y, g_ref[...], b_ref[...])


def _ffn_ln(x2d, w1, w3, w2, g, b, *, tm, ff_chunk=256):
    n = x2d.shape[0]
    row = pl.BlockSpec((tm, D_MODEL), lambda i: (i, 0))
    return pl.pallas_call(
        functools.partial(_ffn_ln_body, ff_chunk=ff_chunk),
        grid=(n // tm,),
        in_specs=[row, _resident((D_MODEL, D_FF)), _resident((D_MODEL, D_FF)), _resident((D_FF, D_MODEL)),
                  _resident((1, D_MODEL)), _resident((1, D_MODEL))],
        out_specs=row,
        out_shape=jax.ShapeDtypeStruct((n, D_MODEL), F32),
        scratch_shapes=[pltpu.VMEM((tm, D_MODEL), F32)],
        compiler_params=_params("arbitrary"),
        name="ffn_ln",
    )(x2d, w1, w3, w2, g, b)


PROJ_CHUNK = 512
QKV_WIDTH = ATTN_WIDTH + 2 * KV_WIDTH
REST_WIDTH = 3 * HGRN_WIDTH + 2 * D_MODEL


def _rope_block(t, c, s_lo, s_hi):
    half = ROPE_DIM // 2
    return t * c + pltpu.roll(t, half, axis=1) * s_hi + pltpu.roll(t, LANES - half, axis=1) * s_lo


def _in_proj_body(x_ref, w_ref, b_ref, c_ref, slo_ref, shi_ref, qk_ref, v_ref, fl_ref, rest_ref):
    xb = x_ref[...].astype(BF16)
    c, s_lo, s_hi = c_ref[...], slo_ref[...], shi_ref[...]
    q_scale = HEAD_DIM ** -0.5
    for j in range(D_IN // PROJ_CHUNK):
        lo = j * PROJ_CHUNK
        cols = slice(lo, lo + PROJ_CHUNK)
        p = jnp.dot(xb, w_ref[:, cols], preferred_element_type=F32) + b_ref[:, cols]
        for u in range(PROJ_CHUNK // LANES):
            col = lo + u * LANES
            t = p[:, u * LANES:(u + 1) * LANES]
            if col < ATTN_WIDTH:
                qk_ref[:, col:col + LANES] = (_rope_block(t, c, s_lo, s_hi) * q_scale).astype(BF16)
            elif col < ATTN_WIDTH + KV_WIDTH:
                qk_ref[:, col:col + LANES] = _rope_block(t, c, s_lo, s_hi).astype(BF16)
            elif col < QKV_WIDTH:
                off = col - ATTN_WIDTH - KV_WIDTH
                v_ref[:, off:off + LANES] = t.astype(BF16)
            elif col < QKV_WIDTH + HGRN_WIDTH:
                off = col - QKV_WIDTH
                fl_ref[:, off:off + LANES] = t
            else:
                off = col - QKV_WIDTH - HGRN_WIDTH
                rest_ref[:, off:off + LANES] = t.astype(BF16)


def _in_proj(x2d, w_in, b_in, rope_c, rope_slo, rope_shi, *, tm, seq):
    n = x2d.shape[0]
    tiles_per_seq = seq // tm
    row = lambda w: pl.BlockSpec((tm, w), lambda i: (i, 0))
    tab = pl.BlockSpec((tm, LANES), lambda i: (i % tiles_per_seq, 0))
    return pl.pallas_call(
        _in_proj_body,
        grid=(n // tm,),
        in_specs=[row(D_MODEL), _resident((D_MODEL, D_IN)), _resident((1, D_IN)), tab, tab, tab],
        out_specs=[row(ATTN_WIDTH + KV_WIDTH), row(KV_WIDTH), row(HGRN_WIDTH), row(REST_WIDTH)],
        out_shape=[jax.ShapeDtypeStruct((n, ATTN_WIDTH + KV_WIDTH), BF16),
                   jax.ShapeDtypeStruct((n, KV_WIDTH), BF16),
                   jax.ShapeDtypeStruct((n, HGRN_WIDTH), F32),
                   jax.ShapeDtypeStruct((n, REST_WIDTH), BF16)],
        compiler_params=_params("arbitrary"),
        name="in_proj",
    )(x2d, w_in, b_in, rope_c, rope_slo, rope_shi)


def _rope_tables(seq):
    half = ROPE_DIM // 2
    pos = jnp.arange(seq, dtype=F32)
    inv_freq = ROPE_THETA ** (-jnp.arange(0, ROPE_DIM, 2, dtype=F32) / ROPE_DIM)
    ang = pos[:, None] * inv_freq[None, :]
    cos, sin = jnp.cos(ang), jnp.sin(ang)
    ones = jnp.ones((seq, HEAD_DIM - ROPE_DIM), F32)
    zeros = jnp.zeros((seq, HEAD_DIM - half), F32)
    c = jnp.concatenate([cos, cos, ones], axis=1)
    s_lo = jnp.concatenate([-sin, zeros], axis=1)
    s_hi = jnp.concatenate([jnp.zeros((seq, half), F32), sin, ones * 0.0], axis=1)
    rep = LANES // HEAD_DIM
    return tuple(jnp.tile(t, (1, rep)) for t in (c, s_lo, s_hi))


def _swa_body(sink_ref, q_ref, kc_ref, kp_ref, vc_ref, vp_ref, o_ref):
    blk = pl.program_id(1)
    q = q_ref[0]
    kc, kp, vc, vp = kc_ref[0], kp_ref[0], vc_ref[0], vp_ref[0]
    rows = GQA_GROUP * ATTN_BLOCK
    qi = lax.broadcasted_iota(jnp.int32, (rows, 2 * ATTN_BLOCK), 0) % ATTN_BLOCK
    kj = lax.broadcasted_iota(jnp.int32, (rows, 2 * ATTN_BLOCK), 1)
    dist = qi + ATTN_BLOCK - kj
    valid = (dist >= 0) & (dist < WINDOW) & ((kj >= ATTN_BLOCK) | (blk > 0))
    outs = []
    for kh in range(N_KV_HEADS):
        hs = slice(kh * HEAD_DIM, (kh + 1) * HEAD_DIM)
        kb = jnp.concatenate([kp[:, hs], kc[:, hs]], axis=0)
        vb = jnp.concatenate([vp[:, hs], vc[:, hs]], axis=0)
        heads = [kh * GQA_GROUP + g for g in range(GQA_GROUP)]
        qg = jnp.concatenate([q[:, h * HEAD_DIM:(h + 1) * HEAD_DIM] for h in heads], axis=0)
        s = lax.dot_general(qg, kb, (((1,), (1,)), ((), ())), preferred_element_type=F32)
        s = jnp.where(valid, s, NEG_INF)
        sink = jnp.concatenate([jnp.full((ATTN_BLOCK, 1), sink_ref[h], F32) for h in heads], axis=0)
        m = jnp.maximum(jnp.max(s, axis=-1, keepdims=True), sink)
        p = jnp.exp(s - m)
        denom = jnp.sum(p, axis=-1, keepdims=True) + jnp.exp(sink - m)
        probs = (p * (1.0 / denom)).astype(BF16)
        o = jnp.dot(probs, vb, preferred_element_type=F32)
        outs += [o[g * ATTN_BLOCK:(g + 1) * ATTN_BLOCK] for g in range(GQA_GROUP)]
    o_ref[0] = jnp.concatenate(outs, axis=1).astype(BF16)


def _swa_attn(sinks, qk, v, *, batch, seq):
    nb = seq // ATTN_BLOCK
    qk3 = qk.reshape(batch, seq, ATTN_WIDTH + KV_WIDTH)
    v3 = v.reshape(batch, seq, KV_WIDTH)
    k_col = ATTN_WIDTH // KV_WIDTH
    cur = lambda b, n: (b, n, 0)
    prev = lambda b, n: (b, jnp.maximum(n - 1, 0), 0)
    out = pl.pallas_call(
        _swa_body,
        grid=(batch, nb),
        in_specs=[pl.BlockSpec(memory_space=pltpu.SMEM),
                  pl.BlockSpec((1, ATTN_BLOCK, ATTN_WIDTH), cur),
                  pl.BlockSpec((1, ATTN_BLOCK, KV_WIDTH), lambda b, n: (b, n, k_col)),
                  pl.BlockSpec((1, ATTN_BLOCK, KV_WIDTH), lambda b, n: (b, jnp.maximum(n - 1, 0), k_col)),
                  pl.BlockSpec((1, ATTN_BLOCK, KV_WIDTH), cur),
                  pl.BlockSpec((1, ATTN_BLOCK, KV_WIDTH), prev)],
        out_specs=pl.BlockSpec((1, ATTN_BLOCK, ATTN_WIDTH), cur),
        out_shape=jax.ShapeDtypeStruct((batch, seq, ATTN_WIDTH), BF16),
        compiler_params=_params("arbitrary", "arbitrary"),
        name="swa_attn",
    )(sinks, qk3, qk3, qk3, v3, v3)
    return out.reshape(batch * seq, ATTN_WIDTH)


def _split3(x):
    hi = x.astype(BF16)
    r = x - hi.astype(F32)
    mid = r.astype(BF16)
    lo = (r - mid.astype(F32)).astype(BF16)
    return hi, mid, lo


def _hgrn_body(fl_ref, q_ref, v_ref, og_ref, lb_ref, ng_ref, o_ref, state_ref, *, n_chunks):
    @pl.when(pl.program_id(1) == 0)
    def _():
        state_ref[...] = jnp.zeros_like(state_ref)

    c = HGRN_CHUNK
    ti = lax.broadcasted_iota(jnp.int32, (c, c), 0)
    si = lax.broadcasted_iota(jnp.int32, (c, c), 1)
    causal = ti >= si
    tril = causal.astype(BF16)
    ng = ng_ref[...]

    def chunk(ci, carry):
        rows = pl.ds(pl.multiple_of(ci * c, c), c)
        for h in range(HGRN_HEADS):
            lanes = slice(h * HGRN_DK, (h + 1) * HGRN_DK)
            lb = lb_ref[:, lanes]
            f = lb + (1.0 - lb) * _sigmoid(fl_ref[0, rows, lanes])
            k = 1.0 - f
            hi, mid, lo = _split3(jnp.log(f))
            gc = (jnp.dot(tril, hi, preferred_element_type=F32)
                  + jnp.dot(tril, mid, preferred_element_type=F32)
                  + jnp.dot(tril, lo, preferred_element_type=F32))
            g_last = gc[c - 1:c, :]
            qr = q_ref[0, rows, lanes].astype(F32)
            q_dec = (qr * _sigmoid(qr) * jnp.exp(gc)).astype(BF16)
            k_inv = (k * jnp.exp(-gc)).astype(BF16)
            k_end = (k * jnp.exp(g_last - gc)).astype(BF16)
            v = v_ref[0, rows, lanes]
            st = state_ref[h]
            sc = lax.dot_general(q_dec, k_inv, (((1,), (1,)), ((), ())), preferred_element_type=F32)
            sc = jnp.where(causal, sc, 0.0).astype(BF16)
            o = jnp.dot(sc, v, preferred_element_type=F32)
            o += lax.dot_general(q_dec, st.astype(BF16), (((1,), (1,)), ((), ())), preferred_element_type=F32)
            upd = lax.dot_general(v, k_end, (((0,), (0,)), ((), ())), preferred_element_type=F32)
            state_ref[h] = st * jnp.exp(g_last) + upd
            o = o * lax.rsqrt(jnp.mean(o * o, axis=-1, keepdims=True) + RMS_EPS) * ng
            og = og_ref[0, rows, lanes].astype(F32)
            o_ref[0, rows, lanes] = (o * (og * _sigmoid(og))).astype(BF16)
        return carry

    lax.fori_loop(0, n_chunks, chunk, 0)


def _hgrn2(fl, rest, lb, norm_g, *, batch, seq, tm):
    fl3 = fl.reshape(batch, seq, HGRN_WIDTH)
    rest3 = rest.reshape(batch, seq, REST_WIDTH)
    blk = lambda col: pl.BlockSpec((1, tm, HGRN_WIDTH), lambda b, i: (b, i, col))
    out = pl.pallas_call(
        functools.partial(_hgrn_body, n_chunks=tm // HGRN_CHUNK),
        grid=(batch, seq // tm),
        in_specs=[blk(0), blk(0), blk(1), blk(2), _resident((1, HGRN_WIDTH)), _resident((1, HGRN_DV))],
        out_specs=blk(0),
        out_shape=jax.ShapeDtypeStruct((batch, seq, HGRN_WIDTH), BF16),
        scratch_shapes=[pltpu.VMEM((HGRN_HEADS, HGRN_DV, HGRN_DK), F32)],
        compiler_params=_params("arbitrary", "arbitrary"),
        name="hgrn2",
    )(fl3, rest3, rest3, rest3, lb, norm_g)
    return out.reshape(batch * seq, HGRN_WIDTH)


def _mix_out_body(x_ref, ya_ref, yh_ref, ga_ref, gh_ref, wpa_ref, wph_ref, wo_ref, g_ref, b_ref, o_ref):
    pa = jnp.dot(ya_ref[...], wpa_ref[...], preferred_element_type=F32)
    ph = jnp.dot(yh_ref[...], wph_ref[...], preferred_element_type=F32)
    merged = _sigmoid(ga_ref[...].astype(F32)) * pa + _sigmoid(gh_ref[...].astype(F32)) * ph
    mix = jnp.dot(merged.astype(BF16), wo_ref[...], preferred_element_type=F32)
    y = DEEPNORM_ALPHA * x_ref[...] + mix
    o_ref[...] = _layer_norm(y, g_ref[...], b_ref[...])


def _mix_out(x2d, y_attn, y_hgrn, rest, w_pa, w_ph, w_out, g, b, *, tm):
    n = x2d.shape[0]
    row = pl.BlockSpec((tm, D_MODEL), lambda i: (i, 0))
    gate = lambda col: pl.BlockSpec((tm, D_MODEL), lambda i: (i, col))
    sq = _resident((D_MODEL, D_MODEL))
    return pl.pallas_call(
        _mix_out_body,
        grid=(n // tm,),
        in_specs=[row, row, row, gate(3), gate(4), sq, sq, sq, _resident((1, D_MODEL)), _resident((1, D_MODEL))],
        out_specs=row,
        out_shape=jax.ShapeDtypeStruct((n, D_MODEL), F32),
        compiler_params=_params("arbitrary"),
        name="mix_out",
    )(x2d, y_attn, y_hgrn, rest, rest, w_pa, w_ph, w_out, g, b)


def _tile(n, want):
    t = min(n, want)
    assert n % t == 0, (n, t)
    return t


def kernel(x, ln1_g, ln1_b, ffn1_w1, ffn1_w3, ffn1_w2, ln2_g, ln2_b, w_in, b_in, attn_sinks, hgrn_lb_logits,
           hgrn_norm_g, w_proj_attn, w_proj_hgrn, w_out, ln3_g, ln3_b, ffn2_w1, ffn2_w3, ffn2_w2):
    batch, seq, d = x.shape
    assert d == D_MODEL and seq % ATTN_BLOCK == 0
    n = batch * seq
    tm = _tile(seq, 512)
    lb_all = jnp.cumsum(jax.nn.softmax(hgrn_lb_logits.astype(F32), axis=0), axis=0)
    rope = _rope_tables(seq)
    bf = lambda w: w.astype(BF16)
    h = x.reshape(n, d)
    for l in range(DEPTH):
        h = _ffn_ln(h, bf(ffn1_w1[l]), bf(ffn1_w3[l]), bf(ffn1_w2[l]), ln1_g[l:l + 1], ln1_b[l:l + 1], tm=tm)
        qk, v, fl, rest = _in_proj(h, bf(w_in[l]), b_in[l:l + 1], *rope, tm=tm, seq=seq)
        y_attn = _swa_attn(attn_sinks[l], qk, v, batch=batch, seq=seq)
        y_hgrn = _hgrn2(fl, rest, lb_all[l:l + 1], hgrn_norm_g[l:l + 1], batch=batch, seq=seq, tm=tm)
        h = _mix_out(h, y_attn, y_hgrn, rest, bf(w_proj_attn[l]), bf(w_proj_hgrn[l]), bf(w_out[l]),
                     ln2_g[l:l + 1], ln2_b[l:l + 1], tm=tm)
        h = _ffn_ln(h, bf(ffn2_w1[l]), bf(ffn2_w3[l]), bf(ffn2_w2[l]), ln3_g[l:l + 1], ln3_b[l:l + 1], tm=tm)
    return h.reshape(batch, seq, d)
```

```python
import functools

import jax
import jax.numpy as jnp
from jax import lax
from jax.experimental import pallas as pl
from jax.experimental.pallas import tpu as pltpu

D_MODEL = 1024
DEPTH = 1
N_Q_HEADS = 16
N_KV_HEADS = 4
HEAD_DIM = 64
GQA_GROUP = N_Q_HEADS // N_KV_HEADS
WINDOW = 128
ATTN_BLOCK = 128
ROPE_THETA = 500000.0
ROPE_DIM = HEAD_DIM // 4
ROPE_HALF = ROPE_DIM // 2
HGRN_HEADS = 8
HGRN_DK = 128
HGRN_DV = 128
HGRN_CHUNK = 64
D_FF = 2816
ATTN_WIDTH = N_Q_HEADS * HEAD_DIM
KV_WIDTH = N_KV_HEADS * HEAD_DIM
HGRN_WIDTH = HGRN_HEADS * HGRN_DK
D_IN = ATTN_WIDTH + 2 * KV_WIDTH + 4 * HGRN_WIDTH + 2 * D_MODEL
DEEPNORM_ALPHA = (2 * DEPTH) ** 0.25
LN_EPS = 1e-5
RMS_EPS = 1e-6
NEG_INF = -1e30

LANES = 128
F32_SUBLANES = 8
VMEM_LIMIT_BYTES = 56 * 1024 * 1024

F32 = jnp.float32
BF16 = jnp.bfloat16

assert WINDOW == ATTN_BLOCK and ROPE_HALF == F32_SUBLANES


def _sigmoid(x):
    return 0.5 * jnp.tanh(0.5 * x) + 0.5


def _layer_norm(y, g, b):
    mu = jnp.mean(y, axis=-1, keepdims=True)
    yc = y - mu
    var = jnp.mean(yc * yc, axis=-1, keepdims=True)
    return yc * lax.rsqrt(var + LN_EPS) * g + b


def _resident(shape):
    return pl.BlockSpec(shape, lambda *_: (0,) * len(shape), pipeline_mode=pl.Buffered(1))


def _params(*semantics):
    return pltpu.CompilerParams(dimension_semantics=semantics, vmem_limit_bytes=VMEM_LIMIT_BYTES)


def _ffn_ln_body(x_ref, w1_ref, w3_ref, w2_ref, g_ref, b_ref, o_ref, acc_ref, *, ff_chunk):
    x = x_ref[...]
    xb = x.astype(BF16)
    for c in range(D_FF // ff_chunk):
        cols = slice(c * ff_chunk, (c + 1) * ff_chunk)
        h1 = jnp.dot(xb, w1_ref[:, cols], preferred_element_type=F32)
        h3 = jnp.dot(xb, w3_ref[:, cols], preferred_element_type=F32)
        a = (h1 * _sigmoid(h1) * h3).astype(BF16)
        part = jnp.dot(a, w2_ref[cols, :], preferred_element_type=F32)
        if c == 0:
            acc_ref[...] = part
        else:
            acc_ref[...] += part
    y = DEEPNORM_ALPHA * x + 0.5 * acc_ref[...]
    o_ref[...] = _layer_norm(y, g_ref[...], b_ref[...])


def _ffn_ln(x2d, w1, w3, w2, g, b, *, tm, ff_chunk=256):
    n = x2d.shape[0]
    row = pl.BlockSpec((tm, D_MODEL), lambda i: (i, 0))
    return pl.pallas_call(
        functools.partial(_ffn_ln_body, ff_chunk=ff_chunk),
        grid=(n // tm,),
        in_specs=[row, _resident((D_MODEL, D_FF)), _resident((D_MODEL, D_FF)), _resident((D_FF, D_MODEL)),
                  _resident((1, D_MODEL)), _resident((1, D_MODEL))],
        out_specs=row,
        out_shape=jax.ShapeDtypeStruct((n, D_MODEL), F32),
        scratch_shapes=[pltpu.VMEM((tm, D_MODEL), F32)],
        compiler_params=_params("arbitrary"),
        name="ffn_ln",
    )(x2d, w1, w3, w2, g, b)


PROJ_CHUNK = 512
QV_ROWS = ATTN_WIDTH + KV_WIDTH
QV_CHUNK = 256
REST_WIDTH = 3 * HGRN_WIDTH + 2 * D_MODEL
TOK_WIDTH = HGRN_WIDTH + REST_WIDTH + KV_WIDTH


def _rope_lanes(t, c, s_lo, s_hi):
    return t * c + pltpu.roll(t, ROPE_HALF, axis=1) * s_hi + pltpu.roll(t, LANES - ROPE_HALF, axis=1) * s_lo


def _in_proj_body(x_ref, wt_ref, bt_ref, w_ref, b_ref, c_ref, slo_ref, shi_ref, cos_ref, sin_ref,
                  qt_ref, vt_ref, k_ref, fl_ref, rest_ref):
    xb = x_ref[...].astype(BF16)
    q_scale = HEAD_DIM ** -0.5
    cos_t, sin_t = cos_ref[...], sin_ref[...]
    for j in range(QV_ROWS // QV_CHUNK):
        rows = slice(j * QV_CHUNK, (j + 1) * QV_CHUNK)
        pt = lax.dot_general(wt_ref[rows, :], xb, (((1,), (1,)), ((), ())), preferred_element_type=F32)
        pt = pt + bt_ref[rows, :]
        if j * QV_CHUNK < ATTN_WIDTH:
            for h in range(QV_CHUNK // HEAD_DIM):
                r0 = h * HEAD_DIM
                t1 = pt[r0:r0 + ROPE_HALF]
                t2 = pt[r0 + ROPE_HALF:r0 + ROPE_DIM]
                rot = jnp.concatenate([t1 * cos_t - t2 * sin_t, t2 * cos_t + t1 * sin_t,
                                       pt[r0 + ROPE_DIM:r0 + HEAD_DIM]], axis=0)
                qt_ref[j * QV_CHUNK + r0:j * QV_CHUNK + r0 + HEAD_DIM, :] = (rot * q_scale).astype(BF16)
        else:
            vt_ref[...] = pt.astype(BF16)
    c, s_lo, s_hi = c_ref[...], slo_ref[...], shi_ref[...]
    lo = 0
    while lo < TOK_WIDTH:
        width = min(PROJ_CHUNK, TOK_WIDTH - lo)
        cols = slice(lo, lo + width)
        p = jnp.dot(xb, w_ref[:, cols], preferred_element_type=F32) + b_ref[:, cols]
        if lo < HGRN_WIDTH:
            fl_ref[:, cols] = p
        elif lo < HGRN_WIDTH + REST_WIDTH:
            rest_ref[:, lo - HGRN_WIDTH:lo - HGRN_WIDTH + width] = p.astype(BF16)
        else:
            for u in range(width // LANES):
                t = p[:, u * LANES:(u + 1) * LANES]
                k_ref[:, u * LANES:(u + 1) * LANES] = _rope_lanes(t, c, s_lo, s_hi).astype(BF16)
        lo += width


def _in_proj(x2d, w_in, b_in, rope, *, tm, seq):
    n = x2d.shape[0]
    tiles_per_seq = seq // tm
    q_end, k_end, v_end = ATTN_WIDTH, ATTN_WIDTH + KV_WIDTH, ATTN_WIDTH + 2 * KV_WIDTH
    w_t = jnp.concatenate([w_in[:, :q_end], w_in[:, k_end:v_end]], axis=1).T.astype(BF16)
    b_t = jnp.concatenate([b_in[:q_end], b_in[k_end:v_end]])[:, None]
    w_tok = jnp.concatenate([w_in[:, v_end:], w_in[:, q_end:k_end]], axis=1).astype(BF16)
    b_tok = jnp.concatenate([b_in[v_end:], b_in[q_end:k_end]])[None, :]
    c, s_lo, s_hi, cos_t, sin_t = rope
    row = lambda w: pl.BlockSpec((tm, w), lambda i: (i, 0))
    col = lambda r: pl.BlockSpec((r, tm), lambda i: (0, i))
    tab = pl.BlockSpec((tm, LANES), lambda i: (i % tiles_per_seq, 0))
    tab_t = pl.BlockSpec((ROPE_HALF, tm), lambda i: (0, i % tiles_per_seq))
    return pl.pallas_call(
        _in_proj_body,
        grid=(n // tm,),
        in_specs=[row(D_MODEL), _resident((QV_ROWS, D_MODEL)), _resident((QV_ROWS, 1)),
                  _resident((D_MODEL, TOK_WIDTH)), _resident((1, TOK_WIDTH)), tab, tab, tab, tab_t, tab_t],
        out_specs=[col(ATTN_WIDTH), col(KV_WIDTH), row(KV_WIDTH), row(HGRN_WIDTH), row(REST_WIDTH)],
        out_shape=[jax.ShapeDtypeStruct((ATTN_WIDTH, n), BF16),
                   jax.ShapeDtypeStruct((KV_WIDTH, n), BF16),
                   jax.ShapeDtypeStruct((n, KV_WIDTH), BF16),
                   jax.ShapeDtypeStruct((n, HGRN_WIDTH), F32),
                   jax.ShapeDtypeStruct((n, REST_WIDTH), BF16)],
        compiler_params=_params("arbitrary"),
        name="in_proj",
    )(x2d, w_t, b_t, w_tok, b_tok, c, s_lo, s_hi, cos_t, sin_t)


def _rope_tables(seq):
    pos = jnp.arange(seq, dtype=F32)
    inv_freq = ROPE_THETA ** (-jnp.arange(0, ROPE_DIM, 2, dtype=F32) / ROPE_DIM)
    ang = pos[:, None] * inv_freq[None, :]
    cos, sin = jnp.cos(ang), jnp.sin(ang)
    ones = jnp.ones((seq, HEAD_DIM - ROPE_DIM), F32)
    c = jnp.concatenate([cos, cos, ones], axis=1)
    s_lo = jnp.concatenate([-sin, jnp.zeros((seq, HEAD_DIM - ROPE_HALF), F32)], axis=1)
    s_hi = jnp.concatenate([jnp.zeros((seq, ROPE_HALF), F32), sin, 0.0 * ones], axis=1)
    rep = LANES // HEAD_DIM
    return tuple(jnp.tile(t, (1, rep)) for t in (c, s_lo, s_hi)) + (cos.T, sin.T)


ATTN_TILE_BLOCKS = 4


def _swa_body(sink_ref, qt_ref, k_ref, kh_ref, vt_ref, vth_ref, o_ref):
    first_tile = pl.program_id(1) == 0
    halo_bias = jnp.where(first_tile, NEG_INF, 0.0).astype(F32)
    wide = GQA_GROUP * ATTN_BLOCK
    key_i = lax.broadcasted_iota(jnp.int32, (ATTN_BLOCK, wide), 0)
    qry_i = lax.broadcasted_iota(jnp.int32, (ATTN_BLOCK, wide), 1) % ATTN_BLOCK
    from_prev = key_i > qry_i
    zero_half = jnp.zeros((HEAD_DIM, ATTN_BLOCK), BF16)
    for qb in range(ATTN_TILE_BLOCKS):
        toks = slice(qb * ATTN_BLOCK, (qb + 1) * ATTN_BLOCK)
        k_cur = k_ref[toks, :]
        k_prev = kh_ref[...] if qb == 0 else k_ref[(qb - 1) * ATTN_BLOCK:qb * ATTN_BLOCK, :]
        vt_cur = vt_ref[:, toks]
        vt_prev = vth_ref[...] if qb == 0 else vt_ref[:, (qb - 1) * ATTN_BLOCK:qb * ATTN_BLOCK]
        keys = jnp.concatenate([k_prev, k_cur], axis=0)
        for kh in range(N_KV_HEADS):
            heads = [kh * GQA_GROUP + g for g in range(GQA_GROUP)]
            pair = slice((kh // 2) * LANES, (kh // 2 + 1) * LANES)
            cols = []
            for h in heads:
                qh = qt_ref[h * HEAD_DIM:(h + 1) * HEAD_DIM, toks]
                cols.append(jnp.concatenate([qh, zero_half] if kh % 2 == 0 else [zero_half, qh], axis=0))
            q_rhs = jnp.concatenate(cols, axis=1)
            s = jnp.dot(keys[:, pair], q_rhs, preferred_element_type=F32)
            s_prev = s[:ATTN_BLOCK]
            if qb == 0:
                s_prev = s_prev + halo_bias
            c = jnp.where(from_prev, s_prev, s[ATTN_BLOCK:])
            sink = jnp.concatenate([jnp.full((1, ATTN_BLOCK), sink_ref[h], F32) for h in heads], axis=1)
            m = jnp.maximum(jnp.max(c, axis=0, keepdims=True), sink)
            p = jnp.exp(c - m)
            denom = jnp.sum(p, axis=0, keepdims=True) + jnp.exp(sink - m)
            pb = p.astype(BF16)
            zero = jnp.zeros_like(pb)
            p_both = jnp.concatenate([jnp.where(from_prev, pb, zero), jnp.where(from_prev, zero, pb)], axis=0)
            vt_both = jnp.concatenate([vt_prev[kh * HEAD_DIM:(kh + 1) * HEAD_DIM, :],
                                       vt_cur[kh * HEAD_DIM:(kh + 1) * HEAD_DIM, :]], axis=1)
            ot = jnp.dot(vt_both, p_both, preferred_element_type=F32) * (1.0 / denom)
            for g, h in enumerate(heads):
                o_ref[h * HEAD_DIM:(h + 1) * HEAD_DIM, toks] = ot[:, g * ATTN_BLOCK:(g + 1) * ATTN_BLOCK].astype(BF16)


def _swa_attn(sinks, qt, k, vt, *, batch, seq):
    tq = ATTN_TILE_BLOCKS * ATTN_BLOCK
    tiles = seq // tq
    main = lambda b, i: b * tiles + i
    halo = lambda b, i: jnp.maximum((b * tiles + i) * ATTN_TILE_BLOCKS - 1, 0)
    return pl.pallas_call(
        _swa_body,
        grid=(batch, tiles),
        in_specs=[pl.BlockSpec(memory_space=pltpu.SMEM),
                  pl.BlockSpec((ATTN_WIDTH, tq), lambda b, i: (0, main(b, i))),
                  pl.BlockSpec((tq, KV_WIDTH), lambda b, i: (main(b, i), 0)),
                  pl.BlockSpec((ATTN_BLOCK, KV_WIDTH), lambda b, i: (halo(b, i), 0)),
                  pl.BlockSpec((KV_WIDTH, tq), lambda b, i: (0, main(b, i))),
                  pl.BlockSpec((KV_WIDTH, ATTN_BLOCK), lambda b, i: (0, halo(b, i)))],
        out_specs=pl.BlockSpec((ATTN_WIDTH, tq), lambda b, i: (0, main(b, i))),
        out_shape=jax.ShapeDtypeStruct((ATTN_WIDTH, batch * seq), BF16),
        compiler_params=_params("arbitrary", "arbitrary"),
        name="swa_attn",
    )(sinks, qt, k, k, vt, vt)


def _split3(x):
    hi = x.astype(BF16)
    r = x - hi.astype(F32)
    mid = r.astype(BF16)
    lo = (r - mid.astype(F32)).astype(BF16)
    return hi, mid, lo


def _hgrn_body(fl_ref, q_ref, v_ref, og_ref, lb_ref, ng_ref, o_ref, state_ref, *, n_chunks):
    @pl.when(pl.program_id(1) == 0)
    def _():
        state_ref[...] = jnp.zeros_like(state_ref)

    c = HGRN_CHUNK
    ti = lax.broadcasted_iota(jnp.int32, (c, c), 0)
    si = lax.broadcasted_iota(jnp.int32, (c, c), 1)
    causal = ti >= si
    tril = causal.astype(BF16)
    ng = ng_ref[...]

    def chunk(ci, carry):
        rows = pl.ds(pl.multiple_of(ci * c, c), c)
        for h in range(HGRN_HEADS):
            lanes = slice(h * HGRN_DK, (h + 1) * HGRN_DK)
            lb = lb_ref[:, lanes]
            f = lb + (1.0 - lb) * _sigmoid(fl_ref[0, rows, lanes])
            k = 1.0 - f
            hi, mid, lo = _split3(jnp.log(f))
            gc = (jnp.dot(tril, hi, preferred_element_type=F32)
                  + jnp.dot(tril, mid, preferred_element_type=F32)
                  + jnp.dot(tril, lo, preferred_element_type=F32))
            g_last = gc[c - 1:c, :]
            qr = q_ref[0, rows, lanes].astype(F32)
            q_dec = (qr * _sigmoid(qr) * jnp.exp(gc)).astype(BF16)
            k_inv = (k * jnp.exp(-gc)).astype(BF16)
            k_end = (k * jnp.exp(g_last - gc)).astype(BF16)
            v = v_ref[0, rows, lanes]
            st = state_ref[h]
            sc = lax.dot_general(q_dec, k_inv, (((1,), (1,)), ((), ())), preferred_element_type=F32)
            sc = jnp.where(causal, sc, 0.0).astype(BF16)
            o = jnp.dot(sc, v, preferred_element_type=F32)
            o += lax.dot_general(q_dec, st.astype(BF16), (((1,), (1,)), ((), ())), preferred_element_type=F32)
            upd = lax.dot_general(v, k_end, (((0,), (0,)), ((), ())), preferred_element_type=F32)
            state_ref[h] = st * jnp.exp(g_last) + upd
            o = o * lax.rsqrt(jnp.mean(o * o, axis=-1, keepdims=True) + RMS_EPS) * ng
            og = og_ref[0, rows, lanes].astype(F32)
            o_ref[0, rows, lanes] = (o * (og * _sigmoid(og))).astype(BF16)
        return carry

    lax.fori_loop(0, n_chunks, chunk, 0)


def _hgrn2(fl, rest, lb, norm_g, *, batch, seq, tm):
    fl3 = fl.reshape(batch, seq, HGRN_WIDTH)
    rest3 = rest.reshape(batch, seq, REST_WIDTH)
    blk = lambda col: pl.BlockSpec((1, tm, HGRN_WIDTH), lambda b, i: (b, i, col))
    out = pl.pallas_call(
        functools.partial(_hgrn_body, n_chunks=tm // HGRN_CHUNK),
        grid=(batch, seq // tm),
        in_specs=[blk(0), blk(0), blk(1), blk(2), _resident((1, HGRN_WIDTH)), _resident((1, HGRN_DV))],
        out_specs=blk(0),
        out_shape=jax.ShapeDtypeStruct((batch, seq, HGRN_WIDTH), BF16),
        scratch_shapes=[pltpu.VMEM((HGRN_HEADS, HGRN_DV, HGRN_DK), F32)],
        compiler_params=_params("arbitrary", "arbitrary"),
        name="hgrn2",
    )(fl3, rest3, rest3, rest3, lb, norm_g)
    return out.reshape(batch * seq, HGRN_WIDTH)


def _mix_out_body(x_ref, yat_ref, yh_ref, ga_ref, gh_ref, wpa_ref, wph_ref, wo_ref, g_ref, b_ref, o_ref):
    pa = lax.dot_general(yat_ref[...], wpa_ref[...], (((0,), (0,)), ((), ())), preferred_element_type=F32)
    ph = jnp.dot(yh_ref[...], wph_ref[...], preferred_element_type=F32)
    merged = _sigmoid(ga_ref[...].astype(F32)) * pa + _sigmoid(gh_ref[...].astype(F32)) * ph
    mix = jnp.dot(merged.astype(BF16), wo_ref[...], preferred_element_type=F32)
    y = DEEPNORM_ALPHA * x_ref[...] + mix
    o_ref[...] = _layer_norm(y, g_ref[...], b_ref[...])


def _mix_out(x2d, y_attn_t, y_hgrn, rest, w_pa, w_ph, w_out, g, b, *, tm):
    n = x2d.shape[0]
    row = pl.BlockSpec((tm, D_MODEL), lambda i: (i, 0))
    gate = lambda col: pl.BlockSpec((tm, D_MODEL), lambda i: (i, col))
    sq = _resident((D_MODEL, D_MODEL))
    return pl.pallas_call(
        _mix_out_body,
        grid=(n // tm,),
        in_specs=[row, pl.BlockSpec((ATTN_WIDTH, tm), lambda i: (0, i)), row, gate(3), gate(4), sq, sq, sq,
                  _resident((1, D_MODEL)), _resident((1, D_MODEL))],
        out_specs=row,
        out_shape=jax.ShapeDtypeStruct((n, D_MODEL), F32),
        compiler_params=_params("arbitrary"),
        name="mix_out",
    )(x2d, y_attn_t, y_hgrn, rest, rest, w_pa, w_ph, w_out, g, b)


def _tile(n, want):
    t = min(n, want)
    assert n % t == 0, (n, t)
    return t


def kernel(x, ln1_g, ln1_b, ffn1_w1, ffn1_w3, ffn1_w2, ln2_g, ln2_b, w_in, b_in, attn_sinks, hgrn_lb_logits,
           hgrn_norm_g, w_proj_attn, w_proj_hgrn, w_out, ln3_g, ln3_b, ffn2_w1, ffn2_w3, ffn2_w2):
    batch, seq, d = x.shape
    assert d == D_MODEL and seq % (ATTN_TILE_BLOCKS * ATTN_BLOCK) == 0
    n = batch * seq
    tm = _tile(seq, 512)
    lb_all = jnp.cumsum(jax.nn.softmax(hgrn_lb_logits.astype(F32), axis=0), axis=0)
    rope = _rope_tables(seq)
    bf = lambda w: w.astype(BF16)
    h = x.reshape(n, d)
    for l in range(DEPTH):
        h = _ffn_ln(h, bf(ffn1_w1[l]), bf(ffn1_w3[l]), bf(ffn1_w2[l]), ln1_g[l:l + 1], ln1_b[l:l + 1], tm=tm)
        qt, vt, k, fl, rest = _in_proj(h, w_in[l], b_in[l], rope, tm=tm, seq=seq)
        y_attn_t = _swa_attn(attn_sinks[l], qt, k, vt, batch=batch, seq=seq)
        y_hgrn = _hgrn2(fl, rest, lb_all[l:l + 1], hgrn_norm_g[l:l + 1], batch=batch, seq=seq, tm=tm)
        h = _mix_out(h, y_attn_t, y_hgrn, rest, bf(w_proj_attn[l]), bf(w_proj_hgrn[l]), bf(w_out[l]),
                     ln2_g[l:l + 1], ln2_b[l:l + 1], tm=tm)
        h = _ffn_ln(h, bf(ffn2_w1[l]), bf(ffn2_w3[l]), bf(ffn2_w2[l]), ln3_g[l:l + 1], ln3_b[l:l + 1], tm=tm)
    return h.reshape(batch, seq, d)
```

```python
import functools

import jax
import jax.numpy as jnp
from jax import lax
from jax.experimental import pallas as pl
from jax.experimental.pallas import tpu as pltpu

D_MODEL = 1024
DEPTH = 1
N_Q_HEADS = 16
N_KV_HEADS = 4
HEAD_DIM = 64
GQA_GROUP = N_Q_HEADS // N_KV_HEADS
WINDOW = 128
ATTN_BLOCK = 128
ROPE_THETA = 500000.0
ROPE_DIM = HEAD_DIM // 4
ROPE_HALF = ROPE_DIM // 2
HGRN_HEADS = 8
HGRN_DK = 128
HGRN_DV = 128
HGRN_CHUNK = 64
D_FF = 2816
ATTN_WIDTH = N_Q_HEADS * HEAD_DIM
KV_WIDTH = N_KV_HEADS * HEAD_DIM
HGRN_WIDTH = HGRN_HEADS * HGRN_DK
D_IN = ATTN_WIDTH + 2 * KV_WIDTH + 4 * HGRN_WIDTH + 2 * D_MODEL
DEEPNORM_ALPHA = (2 * DEPTH) ** 0.25
LN_EPS = 1e-5
RMS_EPS = 1e-6
NEG_INF = -1e30

LANES = 128
F32_SUBLANES = 8
VMEM_LIMIT_BYTES = 56 * 1024 * 1024

F32 = jnp.float32
BF16 = jnp.bfloat16

assert WINDOW == ATTN_BLOCK and ROPE_HALF == F32_SUBLANES


def _sigmoid(x):
    return 0.5 * jnp.tanh(0.5 * x) + 0.5


def _layer_norm(y, g, b):
    mu = jnp.mean(y, axis=-1, keepdims=True)
    yc = y - mu
    var = jnp.mean(yc * yc, axis=-1, keepdims=True)
    return yc * lax.rsqrt(var + LN_EPS) * g + b


def _resident(shape):
    return pl.BlockSpec(shape, lambda *_: (0,) * len(shape), pipeline_mode=pl.Buffered(1))


def _params(*semantics):
    return pltpu.CompilerParams(dimension_semantics=semantics, vmem_limit_bytes=VMEM_LIMIT_BYTES)


def _ffn_ln_body(x_ref, w1_ref, w3_ref, w2_ref, g_ref, b_ref, o_ref, acc_ref, *, ff_chunk):
    x = x_ref[...]
    xb = x.astype(BF16)
    for c in range(D_FF // ff_chunk):
        cols = slice(c * ff_chunk, (c + 1) * ff_chunk)
        h1 = jnp.dot(xb, w1_ref[:, cols], preferred_element_type=F32)
        h3 = jnp.dot(xb, w3_ref[:, cols], preferred_element_type=F32)
        a = (h1 * _sigmoid(h1) * h3).astype(BF16)
        part = jnp.dot(a, w2_ref[cols, :], preferred_element_type=F32)
        if c == 0:
            acc_ref[...] = part
        else:
            acc_ref[...] += part
    y = DEEPNORM_ALPHA * x + 0.5 * acc_ref[...]
    o_ref[...] = _layer_norm(y, g_ref[...], b_ref[...])


def _ffn_ln(x2d, w1, w3, w2, g, b, *, tm, ff_chunk=256):
    n = x2d.shape[0]
    row = pl.BlockSpec((tm, D_MODEL), lambda i: (i, 0))
    return pl.pallas_call(
        functools.partial(_ffn_ln_body, ff_chunk=ff_chunk),
        grid=(n // tm,),
        in_specs=[row, _resident((D_MODEL, D_FF)), _resident((D_MODEL, D_FF)), _resident((D_FF, D_MODEL)),
                  _resident((1, D_MODEL)), _resident((1, D_MODEL))],
        out_specs=row,
        out_shape=jax.ShapeDtypeStruct((n, D_MODEL), F32),
        scratch_shapes=[pltpu.VMEM((tm, D_MODEL), F32)],
        compiler_params=_params("arbitrary"),
        name="ffn_ln",
    )(x2d, w1, w3, w2, g, b)


PROJ_CHUNK = 512
QV_ROWS = ATTN_WIDTH + KV_WIDTH
QV_CHUNK = 256
REST_WIDTH = 3 * HGRN_WIDTH + 2 * D_MODEL
TOK_WIDTH = HGRN_WIDTH + REST_WIDTH + KV_WIDTH


def _rope_lanes(t, c, s_lo, s_hi):
    return t * c + pltpu.roll(t, ROPE_HALF, axis=1) * s_hi + pltpu.roll(t, LANES - ROPE_HALF, axis=1) * s_lo


def _in_proj_body(x_ref, wt_ref, bt_ref, w_ref, b_ref, fa_ref, fb_ref, c_ref, slo_ref, shi_ref, cos_ref, sin_ref,
                  qt_ref, vt_ref, k_ref, f_ref, rest_ref):
    xb = x_ref[...].astype(BF16)
    q_scale = HEAD_DIM ** -0.5
    cos_t, sin_t = cos_ref[...], sin_ref[...]
    for j in range(QV_ROWS // QV_CHUNK):
        rows = slice(j * QV_CHUNK, (j + 1) * QV_CHUNK)
        pt = lax.dot_general(wt_ref[rows, :], xb, (((1,), (1,)), ((), ())), preferred_element_type=F32)
        pt = pt + bt_ref[rows, :]
        if j * QV_CHUNK < ATTN_WIDTH:
            for h in range(QV_CHUNK // HEAD_DIM):
                r0 = h * HEAD_DIM
                t1 = pt[r0:r0 + ROPE_HALF]
                t2 = pt[r0 + ROPE_HALF:r0 + ROPE_DIM]
                rot = jnp.concatenate([t1 * cos_t - t2 * sin_t, t2 * cos_t + t1 * sin_t,
                                       pt[r0 + ROPE_DIM:r0 + HEAD_DIM]], axis=0)
                qt_ref[j * QV_CHUNK + r0:j * QV_CHUNK + r0 + HEAD_DIM, :] = (rot * q_scale).astype(BF16)
        else:
            vt_ref[...] = pt.astype(BF16)
    c, s_lo, s_hi = c_ref[...], slo_ref[...], shi_ref[...]
    lo = 0
    while lo < TOK_WIDTH:
        width = min(PROJ_CHUNK, TOK_WIDTH - lo)
        cols = slice(lo, lo + width)
        p = jnp.dot(xb, w_ref[:, cols], preferred_element_type=F32) + b_ref[:, cols]
        kind = _tok_kind(lo)
        if kind == "forget":
            f_ref[:, cols] = fa_ref[:, cols] + fb_ref[:, cols] * jnp.tanh(p)
        elif kind == "silu":
            rest_ref[:, lo - HGRN_WIDTH:lo - HGRN_WIDTH + width] = (p * jnp.tanh(p) + p).astype(BF16)
        elif kind == "sigmoid":
            rest_ref[:, lo - HGRN_WIDTH:lo - HGRN_WIDTH + width] = (0.5 * jnp.tanh(p) + 0.5).astype(BF16)
        elif kind == "plain":
            rest_ref[:, lo - HGRN_WIDTH:lo - HGRN_WIDTH + width] = p.astype(BF16)
        else:
            for u in range(width // LANES):
                t = p[:, u * LANES:(u + 1) * LANES]
                k_ref[:, u * LANES:(u + 1) * LANES] = _rope_lanes(t, c, s_lo, s_hi).astype(BF16)
        lo += width


_TOK_GROUPS = (("forget", HGRN_WIDTH), ("silu", HGRN_WIDTH), ("plain", HGRN_WIDTH), ("silu", HGRN_WIDTH),
               ("sigmoid", D_MODEL), ("sigmoid", D_MODEL), ("rope_k", KV_WIDTH))


def _tok_kind(col):
    for kind, width in _TOK_GROUPS:
        if col < width:
            return kind
        col -= width
    raise ValueError(col)


def _tok_half_scale():
    return jnp.concatenate([jnp.full((w,), 0.5 if kind in ("forget", "silu", "sigmoid") else 1.0, F32)
                            for kind, w in _TOK_GROUPS])[None, :]


def _in_proj(x2d, w_in, b_in, lb, rope, *, tm, seq):
    n = x2d.shape[0]
    tiles_per_seq = seq // tm
    q_end, k_end, v_end = ATTN_WIDTH, ATTN_WIDTH + KV_WIDTH, ATTN_WIDTH + 2 * KV_WIDTH
    w_t = jnp.concatenate([w_in[:, :q_end], w_in[:, k_end:v_end]], axis=1).T.astype(BF16)
    b_t = jnp.concatenate([b_in[:q_end], b_in[k_end:v_end]])[:, None]
    half = _tok_half_scale()
    w_tok = (jnp.concatenate([w_in[:, v_end:], w_in[:, q_end:k_end]], axis=1) * half).astype(BF16)
    b_tok = jnp.concatenate([b_in[v_end:], b_in[q_end:k_end]])[None, :] * half
    f_a = lb + 0.5 * (1.0 - lb)
    f_b = 0.5 * (1.0 - lb)
    c, s_lo, s_hi, cos_t, sin_t = rope
    row = lambda w: pl.BlockSpec((tm, w), lambda i: (i, 0))
    col = lambda r: pl.BlockSpec((r, tm), lambda i: (0, i))
    tab = pl.BlockSpec((tm, LANES), lambda i: (i % tiles_per_seq, 0))
    tab_t = pl.BlockSpec((ROPE_HALF, tm), lambda i: (0, i % tiles_per_seq))
    return pl.pallas_call(
        _in_proj_body,
        grid=(n // tm,),
        in_specs=[row(D_MODEL), _resident((QV_ROWS, D_MODEL)), _resident((QV_ROWS, 1)),
                  _resident((D_MODEL, TOK_WIDTH)), _resident((1, TOK_WIDTH)),
                  _resident((1, HGRN_WIDTH)), _resident((1, HGRN_WIDTH)), tab, tab, tab, tab_t, tab_t],
        out_specs=[col(ATTN_WIDTH), col(KV_WIDTH), row(KV_WIDTH), row(HGRN_WIDTH), row(REST_WIDTH)],
        out_shape=[jax.ShapeDtypeStruct((ATTN_WIDTH, n), BF16),
                   jax.ShapeDtypeStruct((KV_WIDTH, n), BF16),
                   jax.ShapeDtypeStruct((n, KV_WIDTH), BF16),
                   jax.ShapeDtypeStruct((n, HGRN_WIDTH), F32),
                   jax.ShapeDtypeStruct((n, REST_WIDTH), BF16)],
        compiler_params=_params("arbitrary"),
        name="in_proj",
    )(x2d, w_t, b_t, w_tok, b_tok, f_a, f_b, c, s_lo, s_hi, cos_t, sin_t)


def _rope_tables(seq):
    pos = jnp.arange(seq, dtype=F32)
    inv_freq = ROPE_THETA ** (-jnp.arange(0, ROPE_DIM, 2, dtype=F32) / ROPE_DIM)
    ang = pos[:, None] * inv_freq[None, :]
    cos, sin = jnp.cos(ang), jnp.sin(ang)
    ones = jnp.ones((seq, HEAD_DIM - ROPE_DIM), F32)
    c = jnp.concatenate([cos, cos, ones], axis=1)
    s_lo = jnp.concatenate([-sin, jnp.zeros((seq, HEAD_DIM - ROPE_HALF), F32)], axis=1)
    s_hi = jnp.concatenate([jnp.zeros((seq, ROPE_HALF), F32), sin, 0.0 * ones], axis=1)
    rep = LANES // HEAD_DIM
    return tuple(jnp.tile(t, (1, rep)) for t in (c, s_lo, s_hi)) + (cos.T, sin.T)


ATTN_TILE_BLOCKS = 4


def _swa_body(sink_ref, qt_ref, k_ref, kh_ref, vt_ref, vth_ref, o_ref):
    first_tile = pl.program_id(1) == 0
    halo_bias = jnp.where(first_tile, NEG_INF, 0.0).astype(F32)
    wide = GQA_GROUP * ATTN_BLOCK
    key_i = lax.broadcasted_iota(jnp.int32, (ATTN_BLOCK, wide), 0)
    qry_i = lax.broadcasted_iota(jnp.int32, (ATTN_BLOCK, wide), 1) % ATTN_BLOCK
    from_prev = key_i > qry_i
    zero_half = jnp.zeros((HEAD_DIM, ATTN_BLOCK), BF16)
    for qb in range(ATTN_TILE_BLOCKS):
        toks = slice(qb * ATTN_BLOCK, (qb + 1) * ATTN_BLOCK)
        k_cur = k_ref[toks, :]
        k_prev = kh_ref[...] if qb == 0 else k_ref[(qb - 1) * ATTN_BLOCK:qb * ATTN_BLOCK, :]
        vt_cur = vt_ref[:, toks]
        vt_prev = vth_ref[...] if qb == 0 else vt_ref[:, (qb - 1) * ATTN_BLOCK:qb * ATTN_BLOCK]
        keys = jnp.concatenate([k_prev, k_cur], axis=0)
        for kh in range(N_KV_HEADS):
            heads = [kh * GQA_GROUP + g for g in range(GQA_GROUP)]
            pair = slice((kh // 2) * LANES, (kh // 2 + 1) * LANES)
            cols = []
            for h in heads:
                qh = qt_ref[h * HEAD_DIM:(h + 1) * HEAD_DIM, toks]
                cols.append(jnp.concatenate([qh, zero_half] if kh % 2 == 0 else [zero_half, qh], axis=0))
            q_rhs = jnp.concatenate(cols, axis=1)
            s = jnp.dot(keys[:, pair], q_rhs, preferred_element_type=F32)
            s_prev = s[:ATTN_BLOCK]
            if qb == 0:
                s_prev = s_prev + halo_bias
            c = jnp.where(from_prev, s_prev, s[ATTN_BLOCK:])
            sink = jnp.concatenate([jnp.full((1, ATTN_BLOCK), sink_ref[h], F32) for h in heads], axis=1)
            m = jnp.maximum(jnp.max(c, axis=0, keepdims=True), sink)
            p = jnp.exp(c - m)
            denom = jnp.sum(p, axis=0, keepdims=True) + jnp.exp(sink - m)
            pb = p.astype(BF16)
            zero = jnp.zeros_like(pb)
            p_both = jnp.concatenate([jnp.where(from_prev, pb, zero), jnp.where(from_prev, zero, pb)], axis=0)
            vt_both = jnp.concatenate([vt_prev[kh * HEAD_DIM:(kh + 1) * HEAD_DIM, :],
                                       vt_cur[kh * HEAD_DIM:(kh + 1) * HEAD_DIM, :]], axis=1)
            ot = jnp.dot(vt_both, p_both, preferred_element_type=F32) * (1.0 / denom)
            for g, h in enumerate(heads):
                o_ref[h * HEAD_DIM:(h + 1) * HEAD_DIM, toks] = ot[:, g * ATTN_BLOCK:(g + 1) * ATTN_BLOCK].astype(BF16)


def _swa_attn(sinks, qt, k, vt, *, batch, seq):
    tq = ATTN_TILE_BLOCKS * ATTN_BLOCK
    tiles = seq // tq
    main = lambda b, i: b * tiles + i
    halo = lambda b, i: jnp.maximum((b * tiles + i) * ATTN_TILE_BLOCKS - 1, 0)
    return pl.pallas_call(
        _swa_body,
        grid=(batch, tiles),
        in_specs=[pl.BlockSpec(memory_space=pltpu.SMEM),
                  pl.BlockSpec((ATTN_WIDTH, tq), lambda b, i: (0, main(b, i))),
                  pl.BlockSpec((tq, KV_WIDTH), lambda b, i: (main(b, i), 0)),
                  pl.BlockSpec((ATTN_BLOCK, KV_WIDTH), lambda b, i: (halo(b, i), 0)),
                  pl.BlockSpec((KV_WIDTH, tq), lambda b, i: (0, main(b, i))),
                  pl.BlockSpec((KV_WIDTH, ATTN_BLOCK), lambda b, i: (0, halo(b, i)))],
        out_specs=pl.BlockSpec((ATTN_WIDTH, tq), lambda b, i: (0, main(b, i))),
        out_shape=jax.ShapeDtypeStruct((ATTN_WIDTH, batch * seq), BF16),
        compiler_params=_params("arbitrary", "arbitrary"),
        name="swa_attn",
    )(sinks, qt, k, k, vt, vt)


def _cumprod_rows(f):
    row = lax.broadcasted_iota(jnp.int32, (F32_SUBLANES, LANES), 0)
    blocks, carry = [], None
    for j in range(f.shape[0] // F32_SUBLANES):
        x = f[j * F32_SUBLANES:(j + 1) * F32_SUBLANES]
        shift = 1
        while shift < F32_SUBLANES:
            x = x * jnp.where(row >= shift, pltpu.roll(x, shift, axis=0), 1.0)
            shift *= 2
        if carry is not None:
            x = x * carry
        carry = x[F32_SUBLANES - 1:F32_SUBLANES, :]
        blocks.append(x)
    return jnp.concatenate(blocks, axis=0)


def _hgrn_body(f_ref, q_ref, v_ref, og_ref, ng_ref, o_ref,
               state_ref, qd_ref, ke_ref, sc_ref, dec_ref, *, n_chunks):
    @pl.when(pl.program_id(1) == 0)
    def _():
        state_ref[...] = jnp.zeros_like(state_ref)

    c = HGRN_CHUNK
    ti = lax.broadcasted_iota(jnp.int32, (c, c), 0)
    si = lax.broadcasted_iota(jnp.int32, (c, c), 1)
    causal = ti >= si
    ng = ng_ref[...]
    nt_dims = (((1,), (1,)), ((), ()))

    def gates(ci):
        rows = pl.ds(pl.multiple_of(ci * c, c), c)
        for h in range(HGRN_HEADS):
            lanes = slice(h * HGRN_DK, (h + 1) * HGRN_DK)
            f = f_ref[0, rows, lanes]
            k = 1.0 - f
            decay = _cumprod_rows(f)
            inv = 1.0 / decay
            d_last = decay[c - 1:c, :]
            q_dec = (q_ref[0, rows, lanes].astype(F32) * decay).astype(BF16)
            k_inv = (k * inv).astype(BF16)
            sc = lax.dot_general(q_dec, k_inv, nt_dims, preferred_element_type=F32)
            sc_ref[h] = jnp.where(causal, sc, 0.0).astype(BF16)
            qd_ref[:, lanes] = q_dec
            ke_ref[:, lanes] = (k * (d_last * inv)).astype(BF16)
            dec_ref[:, lanes] = d_last

    def outputs(ci):
        rows = pl.ds(pl.multiple_of(ci * c, c), c)
        for h in range(HGRN_HEADS):
            lanes = slice(h * HGRN_DK, (h + 1) * HGRN_DK)
            v = v_ref[0, rows, lanes]
            st = state_ref[h]
            o = jnp.dot(sc_ref[h], v, preferred_element_type=F32)
            o += lax.dot_general(qd_ref[:, lanes], st.astype(BF16), nt_dims, preferred_element_type=F32)
            upd = lax.dot_general(v, ke_ref[:, lanes], (((0,), (0,)), ((), ())), preferred_element_type=F32)
            state_ref[h] = st * dec_ref[:, lanes] + upd
            o = o * lax.rsqrt(jnp.mean(o * o, axis=-1, keepdims=True) + RMS_EPS) * ng
            o_ref[0, rows, lanes] = (o * og_ref[0, rows, lanes].astype(F32)).astype(BF16)

    gates(0)

    def step(ci, carry):
        outputs(ci)
        gates(ci + 1)
        return carry

    lax.fori_loop(0, n_chunks - 1, step, 0)
    outputs(n_chunks - 1)


def _hgrn2(f, rest, norm_g, *, batch, seq, tm):
    f3 = f.reshape(batch, seq, HGRN_WIDTH)
    rest3 = rest.reshape(batch, seq, REST_WIDTH)
    blk = lambda col: pl.BlockSpec((1, tm, HGRN_WIDTH), lambda b, i: (b, i, col))
    out = pl.pallas_call(
        functools.partial(_hgrn_body, n_chunks=tm // HGRN_CHUNK),
        grid=(batch, seq // tm),
        in_specs=[blk(0), blk(0), blk(1), blk(2), _resident((1, HGRN_DV))],
        out_specs=blk(0),
        out_shape=jax.ShapeDtypeStruct((batch, seq, HGRN_WIDTH), BF16),
        scratch_shapes=[pltpu.VMEM((HGRN_HEADS, HGRN_DV, HGRN_DK), F32),
                        pltpu.VMEM((HGRN_CHUNK, HGRN_WIDTH), BF16),
                        pltpu.VMEM((HGRN_CHUNK, HGRN_WIDTH), BF16),
                        pltpu.VMEM((HGRN_HEADS, HGRN_CHUNK, HGRN_CHUNK), BF16),
                        pltpu.VMEM((1, HGRN_WIDTH), F32)],
        compiler_params=_params("arbitrary", "arbitrary"),
        name="hgrn2",
    )(f3, rest3, rest3, rest3, norm_g)
    return out.reshape(batch * seq, HGRN_WIDTH)


def _mix_out_body(x_ref, yat_ref, yh_ref, ga_ref, gh_ref, wpa_ref, wph_ref, wo_ref, g_ref, b_ref, o_ref):
    pa = lax.dot_general(yat_ref[...], wpa_ref[...], (((0,), (0,)), ((), ())), preferred_element_type=F32)
    ph = jnp.dot(yh_ref[...], wph_ref[...], preferred_element_type=F32)
    merged = ga_ref[...].astype(F32) * pa + gh_ref[...].astype(F32) * ph
    mix = jnp.dot(merged.astype(BF16), wo_ref[...], preferred_element_type=F32)
    y = DEEPNORM_ALPHA * x_ref[...] + mix
    o_ref[...] = _layer_norm(y, g_ref[...], b_ref[...])


def _mix_out(x2d, y_attn_t, y_hgrn, rest, w_pa, w_ph, w_out, g, b, *, tm):
    n = x2d.shape[0]
    row = pl.BlockSpec((tm, D_MODEL), lambda i: (i, 0))
    gate = lambda col: pl.BlockSpec((tm, D_MODEL), lambda i: (i, col))
    sq = _resident((D_MODEL, D_MODEL))
    return pl.pallas_call(
        _mix_out_body,
        grid=(n // tm,),
        in_specs=[row, pl.BlockSpec((ATTN_WIDTH, tm), lambda i: (0, i)), row, gate(3), gate(4), sq, sq, sq,
                  _resident((1, D_MODEL)), _resident((1, D_MODEL))],
        out_specs=row,
        out_shape=jax.ShapeDtypeStruct((n, D_MODEL), F32),
        compiler_params=_params("arbitrary"),
        name="mix_out",
    )(x2d, y_attn_t, y_hgrn, rest, rest, w_pa, w_ph, w_out, g, b)


def _tile(n, want):
    t = min(n, want)
    assert n % t == 0, (n, t)
    return t


def kernel(x, ln1_g, ln1_b, ffn1_w1, ffn1_w3, ffn1_w2, ln2_g, ln2_b, w_in, b_in, attn_sinks, hgrn_lb_logits,
           hgrn_norm_g, w_proj_attn, w_proj_hgrn, w_out, ln3_g, ln3_b, ffn2_w1, ffn2_w3, ffn2_w2):
    batch, seq, d = x.shape
    assert d == D_MODEL and seq % (ATTN_TILE_BLOCKS * ATTN_BLOCK) == 0
    n = batch * seq
    tm = _tile(seq, 512)
    lb_all = jnp.cumsum(jax.nn.softmax(hgrn_lb_logits.astype(F32), axis=0), axis=0)
    rope = _rope_tables(seq)
    bf = lambda w: w.astype(BF16)
    h = x.reshape(n, d)
    for l in range(DEPTH):
        h = _ffn_ln(h, bf(ffn1_w1[l]), bf(ffn1_w3[l]), bf(ffn1_w2[l]), ln1_g[l:l + 1], ln1_b[l:l + 1], tm=tm)
        qt, vt, k, f, rest = _in_proj(h, w_in[l], b_in[l], lb_all[l:l + 1], rope, tm=tm, seq=seq)
        y_attn_t = _swa_attn(attn_sinks[l], qt, k, vt, batch=batch, seq=seq)
        y_hgrn = _hgrn2(f, rest, hgrn_norm_g[l:l + 1], batch=batch, seq=seq, tm=tm)
        h = _mix_out(h, y_attn_t, y_hgrn, rest, bf(w_proj_attn[l]), bf(w_proj_hgrn[l]), bf(w_out[l]),
                     ln2_g[l:l + 1], ln2_b[l:l + 1], tm=tm)
        h = _ffn_ln(h, bf(ffn2_w1[l]), bf(ffn2_w3[l]), bf(ffn2_w2[l]), ln3_g[l:l + 1], ln3_b[l:l + 1], tm=tm)
    return h.reshape(batch, seq, d)
```

```python
import functools

import jax
import jax.numpy as jnp
from jax import lax
from jax.experimental import pallas as pl
from jax.experimental.pallas import tpu as pltpu

D_MODEL = 1024
DEPTH = 1
N_Q_HEADS = 16
N_KV_HEADS = 4
HEAD_DIM = 64
GQA_GROUP = N_Q_HEADS // N_KV_HEADS
WINDOW = 128
ATTN_BLOCK = 128
ROPE_THETA = 500000.0
ROPE_DIM = HEAD_DIM // 4
ROPE_HALF = ROPE_DIM // 2
HGRN_HEADS = 8
HGRN_DK = 128
HGRN_DV = 128
HGRN_CHUNK = 64
D_FF = 2816
ATTN_WIDTH = N_Q_HEADS * HEAD_DIM
KV_WIDTH = N_KV_HEADS * HEAD_DIM
HGRN_WIDTH = HGRN_HEADS * HGRN_DK
D_IN = ATTN_WIDTH + 2 * KV_WIDTH + 4 * HGRN_WIDTH + 2 * D_MODEL
DEEPNORM_ALPHA = (2 * DEPTH) ** 0.25
LN_EPS = 1e-5
RMS_EPS = 1e-6
NEG_INF = -1e30
LOG2_E = 1.4426950408889634

LANES = 128
F32_SUBLANES = 8
VMEM_LIMIT_BYTES = 56 * 1024 * 1024

F32 = jnp.float32
BF16 = jnp.bfloat16

assert WINDOW == ATTN_BLOCK and ROPE_HALF == F32_SUBLANES


def _sigmoid(x):
    return 0.5 * jnp.tanh(0.5 * x) + 0.5


def _layer_norm(y, g, b):
    mu = jnp.mean(y, axis=-1, keepdims=True)
    yc = y - mu
    var = jnp.mean(yc * yc, axis=-1, keepdims=True)
    return yc * lax.rsqrt(var + LN_EPS) * g + b


def _resident(shape):
    return pl.BlockSpec(shape, lambda *_: (0,) * len(shape), pipeline_mode=pl.Buffered(1))


def _params(*semantics):
    return pltpu.CompilerParams(dimension_semantics=semantics, vmem_limit_bytes=VMEM_LIMIT_BYTES)


def _ffn_ln_body(x_ref, w1_ref, w3_ref, w2_ref, g_ref, b_ref, o_ref, acc_ref, *, ff_chunk):
    x = x_ref[...]
    xb = x.astype(BF16)
    for c in range(D_FF // ff_chunk):
        cols = slice(c * ff_chunk, (c + 1) * ff_chunk)
        h1 = jnp.dot(xb, w1_ref[:, cols], preferred_element_type=F32)
        h3 = jnp.dot(xb, w3_ref[:, cols], preferred_element_type=F32)
        a = (h1 * _sigmoid(h1) * h3).astype(BF16)
        part = jnp.dot(a, w2_ref[cols, :], preferred_element_type=F32)
        if c == 0:
            acc_ref[...] = part
        else:
            acc_ref[...] += part
    y = DEEPNORM_ALPHA * x + 0.5 * acc_ref[...]
    o_ref[...] = _layer_norm(y, g_ref[...], b_ref[...])


def _ffn_ln(x2d, w1, w3, w2, g, b, *, tm, ff_chunk=256):
    n = x2d.shape[0]
    row = pl.BlockSpec((tm, D_MODEL), lambda i: (i, 0))
    return pl.pallas_call(
        functools.partial(_ffn_ln_body, ff_chunk=ff_chunk),
        grid=(n // tm,),
        in_specs=[row, _resident((D_MODEL, D_FF)), _resident((D_MODEL, D_FF)), _resident((D_FF, D_MODEL)),
                  _resident((1, D_MODEL)), _resident((1, D_MODEL))],
        out_specs=row,
        out_shape=jax.ShapeDtypeStruct((n, D_MODEL), F32),
        scratch_shapes=[pltpu.VMEM((tm, D_MODEL), F32)],
        compiler_params=_params("arbitrary"),
        name="ffn_ln",
    )(x2d, w1, w3, w2, g, b)


PROJ_CHUNK = 512
QV_ROWS = ATTN_WIDTH + KV_WIDTH
QV_CHUNK = 256
REST_WIDTH = 3 * HGRN_WIDTH + 2 * D_MODEL
TOK_WIDTH = HGRN_WIDTH + REST_WIDTH + KV_WIDTH


def _rope_lanes(t, c, s_lo, s_hi):
    return t * c + pltpu.roll(t, ROPE_HALF, axis=1) * s_hi + pltpu.roll(t, LANES - ROPE_HALF, axis=1) * s_lo


def _in_proj_body(x_ref, wt_ref, bt_ref, w_ref, b_ref, fa_ref, fb_ref, c_ref, slo_ref, shi_ref, cos_ref, sin_ref,
                  qt_ref, vt_ref, k_ref, f_ref, rest_ref):
    xb = x_ref[...].astype(BF16)
    q_scale = HEAD_DIM ** -0.5 * LOG2_E
    cos_t, sin_t = cos_ref[...], sin_ref[...]
    for j in range(QV_ROWS // QV_CHUNK):
        rows = slice(j * QV_CHUNK, (j + 1) * QV_CHUNK)
        pt = lax.dot_general(wt_ref[rows, :], xb, (((1,), (1,)), ((), ())), preferred_element_type=F32)
        pt = pt + bt_ref[rows, :]
        if j * QV_CHUNK < ATTN_WIDTH:
            for h in range(QV_CHUNK // HEAD_DIM):
                r0 = h * HEAD_DIM
                t1 = pt[r0:r0 + ROPE_HALF]
                t2 = pt[r0 + ROPE_HALF:r0 + ROPE_DIM]
                rot = jnp.concatenate([t1 * cos_t - t2 * sin_t, t2 * cos_t + t1 * sin_t,
                                       pt[r0 + ROPE_DIM:r0 + HEAD_DIM]], axis=0)
                qt_ref[j * QV_CHUNK + r0:j * QV_CHUNK + r0 + HEAD_DIM, :] = (rot * q_scale).astype(BF16)
        else:
            vt_ref[...] = pt.astype(BF16)
    c, s_lo, s_hi = c_ref[...], slo_ref[...], shi_ref[...]
    lo = 0
    while lo < TOK_WIDTH:
        width = min(PROJ_CHUNK, TOK_WIDTH - lo)
        cols = slice(lo, lo + width)
        p = jnp.dot(xb, w_ref[:, cols], preferred_element_type=F32) + b_ref[:, cols]
        kind = _tok_kind(lo)
        if kind == "forget":
            f_ref[:, cols] = fa_ref[:, cols] + fb_ref[:, cols] * jnp.tanh(p)
        elif kind == "silu":
            rest_ref[:, lo - HGRN_WIDTH:lo - HGRN_WIDTH + width] = (p * jnp.tanh(p) + p).astype(BF16)
        elif kind == "sigmoid":
            rest_ref[:, lo - HGRN_WIDTH:lo - HGRN_WIDTH + width] = (0.5 * jnp.tanh(p) + 0.5).astype(BF16)
        elif kind == "plain":
            rest_ref[:, lo - HGRN_WIDTH:lo - HGRN_WIDTH + width] = p.astype(BF16)
        else:
            for u in range(width // LANES):
                t = p[:, u * LANES:(u + 1) * LANES]
                k_ref[:, u * LANES:(u + 1) * LANES] = _rope_lanes(t, c, s_lo, s_hi).astype(BF16)
        lo += width


_TOK_GROUPS = (("forget", HGRN_WIDTH), ("silu", HGRN_WIDTH), ("plain", HGRN_WIDTH), ("silu", HGRN_WIDTH),
               ("sigmoid", D_MODEL), ("sigmoid", D_MODEL), ("rope_k", KV_WIDTH))


def _tok_kind(col):
    for kind, width in _TOK_GROUPS:
        if col < width:
            return kind
        col -= width
    raise ValueError(col)


def _tok_half_scale():
    return jnp.concatenate([jnp.full((w,), 0.5 if kind in ("forget", "silu", "sigmoid") else 1.0, F32)
                            for kind, w in _TOK_GROUPS])[None, :]


def _in_proj(x2d, w_in, b_in, lb, rope, *, tm, seq):
    n = x2d.shape[0]
    tiles_per_seq = seq // tm
    q_end, k_end, v_end = ATTN_WIDTH, ATTN_WIDTH + KV_WIDTH, ATTN_WIDTH + 2 * KV_WIDTH
    w_t = jnp.concatenate([w_in[:, :q_end], w_in[:, k_end:v_end]], axis=1).T.astype(BF16)
    b_t = jnp.concatenate([b_in[:q_end], b_in[k_end:v_end]])[:, None]
    half = _tok_half_scale()
    w_tok = (jnp.concatenate([w_in[:, v_end:], w_in[:, q_end:k_end]], axis=1) * half).astype(BF16)
    b_tok = jnp.concatenate([b_in[v_end:], b_in[q_end:k_end]])[None, :] * half
    f_a = lb + 0.5 * (1.0 - lb)
    f_b = 0.5 * (1.0 - lb)
    c, s_lo, s_hi, cos_t, sin_t = rope
    row = lambda w: pl.BlockSpec((tm, w), lambda i: (i, 0))
    col = lambda r: pl.BlockSpec((r, tm), lambda i: (0, i))
    tab = pl.BlockSpec((tm, LANES), lambda i: (i % tiles_per_seq, 0))
    tab_t = pl.BlockSpec((ROPE_HALF, tm), lambda i: (0, i % tiles_per_seq))
    return pl.pallas_call(
        _in_proj_body,
        grid=(n // tm,),
        in_specs=[row(D_MODEL), _resident((QV_ROWS, D_MODEL)), _resident((QV_ROWS, 1)),
                  _resident((D_MODEL, TOK_WIDTH)), _resident((1, TOK_WIDTH)),
                  _resident((1, HGRN_WIDTH)), _resident((1, HGRN_WIDTH)), tab, tab, tab, tab_t, tab_t],
        out_specs=[col(ATTN_WIDTH), col(KV_WIDTH), row(KV_WIDTH), row(HGRN_WIDTH), row(REST_WIDTH)],
        out_shape=[jax.ShapeDtypeStruct((ATTN_WIDTH, n), BF16),
                   jax.ShapeDtypeStruct((KV_WIDTH, n), BF16),
                   jax.ShapeDtypeStruct((n, KV_WIDTH), BF16),
                   jax.ShapeDtypeStruct((n, HGRN_WIDTH), F32),
                   jax.ShapeDtypeStruct((n, REST_WIDTH), BF16)],
        compiler_params=_params("arbitrary"),
        name="in_proj",
    )(x2d, w_t, b_t, w_tok, b_tok, f_a, f_b, c, s_lo, s_hi, cos_t, sin_t)


def _rope_tables(seq):
    pos = jnp.arange(seq, dtype=F32)
    inv_freq = ROPE_THETA ** (-jnp.arange(0, ROPE_DIM, 2, dtype=F32) / ROPE_DIM)
    ang = pos[:, None] * inv_freq[None, :]
    cos, sin = jnp.cos(ang), jnp.sin(ang)
    ones = jnp.ones((seq, HEAD_DIM - ROPE_DIM), F32)
    c = jnp.concatenate([cos, cos, ones], axis=1)
    s_lo = jnp.concatenate([-sin, jnp.zeros((seq, HEAD_DIM - ROPE_HALF), F32)], axis=1)
    s_hi = jnp.concatenate([jnp.zeros((seq, ROPE_HALF), F32), sin, 0.0 * ones], axis=1)
    rep = LANES // HEAD_DIM
    return tuple(jnp.tile(t, (1, rep)) for t in (c, s_lo, s_hi)) + (cos.T, sin.T)


ATTN_TILE_BLOCKS = 4


def _swa_body(sink_ref, qt_ref, k_ref, kh_ref, vt_ref, vth_ref, o_ref):
    first_tile = pl.program_id(1) == 0
    halo_bias = jnp.where(first_tile, NEG_INF, 0.0).astype(F32)
    wide = GQA_GROUP * ATTN_BLOCK
    key_i = lax.broadcasted_iota(jnp.int32, (ATTN_BLOCK, wide), 0)
    qry_i = lax.broadcasted_iota(jnp.int32, (ATTN_BLOCK, wide), 1) % ATTN_BLOCK
    from_prev = key_i > qry_i
    keep_prev = from_prev.astype(BF16)
    keep_cur = 1.0 - keep_prev
    zero_half = jnp.zeros((HEAD_DIM, ATTN_BLOCK), BF16)
    units = [(qb, kh) for qb in range(ATTN_TILE_BLOCKS) for kh in range(N_KV_HEADS)]

    def heads_of(kh):
        return [kh * GQA_GROUP + g for g in range(GQA_GROUP)]

    def toks_of(qb):
        return slice(qb * ATTN_BLOCK, (qb + 1) * ATTN_BLOCK)

    def scores(qb, kh):
        toks = toks_of(qb)
        k_prev = kh_ref[...] if qb == 0 else k_ref[(qb - 1) * ATTN_BLOCK:qb * ATTN_BLOCK, :]
        keys = jnp.concatenate([k_prev, k_ref[toks, :]], axis=0)
        pair = slice((kh // 2) * LANES, (kh // 2 + 1) * LANES)
        cols = []
        for h in heads_of(kh):
            qh = qt_ref[h * HEAD_DIM:(h + 1) * HEAD_DIM, toks]
            cols.append(jnp.concatenate([qh, zero_half] if kh % 2 == 0 else [zero_half, qh], axis=0))
        q_rhs = jnp.concatenate(cols, axis=1)
        return jnp.dot(keys[:, pair], q_rhs, preferred_element_type=F32)

    def finish(qb, kh, s):
        toks = toks_of(qb)
        heads = heads_of(kh)
        s_prev = s[:ATTN_BLOCK]
        if qb == 0:
            s_prev = s_prev + halo_bias
        c = jnp.where(from_prev, s_prev, s[ATTN_BLOCK:])
        sink = jnp.concatenate([jnp.full((1, ATTN_BLOCK), sink_ref[h] * LOG2_E, F32) for h in heads], axis=1)
        m = jnp.maximum(jnp.max(c, axis=0, keepdims=True), sink)
        p = jnp.exp2(c - m)
        denom = jnp.sum(p, axis=0, keepdims=True) + jnp.exp2(sink - m)
        pb = p.astype(BF16)
        p_both = jnp.concatenate([pb * keep_prev, pb * keep_cur], axis=0)
        vt_prev = vth_ref[...] if qb == 0 else vt_ref[:, (qb - 1) * ATTN_BLOCK:qb * ATTN_BLOCK]
        vt_both = jnp.concatenate([vt_prev[kh * HEAD_DIM:(kh + 1) * HEAD_DIM, :],
                                   vt_ref[kh * HEAD_DIM:(kh + 1) * HEAD_DIM, toks]], axis=1)
        ot = jnp.dot(vt_both, p_both, preferred_element_type=F32) * (1.0 / denom)
        for g, h in enumerate(heads):
            o_ref[h * HEAD_DIM:(h + 1) * HEAD_DIM, toks] = ot[:, g * ATTN_BLOCK:(g + 1) * ATTN_BLOCK].astype(BF16)

    s_next = scores(*units[0])
    for i, unit in enumerate(units):
        s_cur = s_next
        if i + 1 < len(units):
            s_next = scores(*units[i + 1])
        finish(*unit, s_cur)


def _swa_attn(sinks, qt, k, vt, *, batch, seq):
    tq = ATTN_TILE_BLOCKS * ATTN_BLOCK
    tiles = seq // tq
    main = lambda b, i: b * tiles + i
    halo = lambda b, i: jnp.maximum((b * tiles + i) * ATTN_TILE_BLOCKS - 1, 0)
    return pl.pallas_call(
        _swa_body,
        grid=(batch, tiles),
        in_specs=[pl.BlockSpec(memory_space=pltpu.SMEM),
                  pl.BlockSpec((ATTN_WIDTH, tq), lambda b, i: (0, main(b, i))),
                  pl.BlockSpec((tq, KV_WIDTH), lambda b, i: (main(b, i), 0)),
                  pl.BlockSpec((ATTN_BLOCK, KV_WIDTH), lambda b, i: (halo(b, i), 0)),
                  pl.BlockSpec((KV_WIDTH, tq), lambda b, i: (0, main(b, i))),
                  pl.BlockSpec((KV_WIDTH, ATTN_BLOCK), lambda b, i: (0, halo(b, i)))],
        out_specs=pl.BlockSpec((ATTN_WIDTH, tq), lambda b, i: (0, main(b, i))),
        out_shape=jax.ShapeDtypeStruct((ATTN_WIDTH, batch * seq), BF16),
        compiler_params=_params("arbitrary", "arbitrary"),
        name="swa_attn",
    )(sinks, qt, k, k, vt, vt)


def _cumprod_rows(f):
    row = lax.broadcasted_iota(jnp.int32, (F32_SUBLANES, LANES), 0)
    blocks, carry = [], None
    for j in range(f.shape[0] // F32_SUBLANES):
        x = f[j * F32_SUBLANES:(j + 1) * F32_SUBLANES]
        shift = 1
        while shift < F32_SUBLANES:
            x = x * jnp.where(row >= shift, pltpu.roll(x, shift, axis=0), 1.0)
            shift *= 2
        if carry is not None:
            x = x * carry
        carry = x[F32_SUBLANES - 1:F32_SUBLANES, :]
        blocks.append(x)
    return jnp.concatenate(blocks, axis=0)


def _hgrn_body(f_ref, q_ref, v_ref, og_ref, ng_ref, o_ref,
               state_ref, qd_ref, ke_ref, sc_ref, dec_ref, *, n_chunks):
    @pl.when(pl.program_id(1) == 0)
    def _():
        state_ref[...] = jnp.zeros_like(state_ref)

    c = HGRN_CHUNK
    ti = lax.broadcasted_iota(jnp.int32, (c, c), 0)
    si = lax.broadcasted_iota(jnp.int32, (c, c), 1)
    causal = ti >= si
    ng = ng_ref[...]
    nt_dims = (((1,), (1,)), ((), ()))

    def gates(ci):
        rows = pl.ds(pl.multiple_of(ci * c, c), c)
        for h in range(HGRN_HEADS):
            lanes = slice(h * HGRN_DK, (h + 1) * HGRN_DK)
            f = f_ref[0, rows, lanes]
            k = 1.0 - f
            decay = _cumprod_rows(f)
            inv = 1.0 / decay
            d_last = decay[c - 1:c, :]
            q_dec = (q_ref[0, rows, lanes].astype(F32) * decay).astype(BF16)
            k_inv = (k * inv).astype(BF16)
            sc = lax.dot_general(q_dec, k_inv, nt_dims, preferred_element_type=F32)
            sc_ref[h] = jnp.where(causal, sc, 0.0).astype(BF16)
            qd_ref[:, lanes] = q_dec
            ke_ref[:, lanes] = (k * (d_last * inv)).astype(BF16)
            dec_ref[:, lanes] = d_last

    def outputs(ci):
        rows = pl.ds(pl.multiple_of(ci * c, c), c)
        for h in range(HGRN_HEADS):
            lanes = slice(h * HGRN_DK, (h + 1) * HGRN_DK)
            v = v_ref[0, rows, lanes]
            st = state_ref[h]
            o = jnp.dot(sc_ref[h], v, preferred_element_type=F32)
            o += lax.dot_general(qd_ref[:, lanes], st.astype(BF16), nt_dims, preferred_element_type=F32)
            upd = lax.dot_general(v, ke_ref[:, lanes], (((0,), (0,)), ((), ())), preferred_element_type=F32)
            state_ref[h] = st * dec_ref[:, lanes] + upd
            o = o * lax.rsqrt(jnp.mean(o * o, axis=-1, keepdims=True) + RMS_EPS) * ng
            o_ref[0, rows, lanes] = (o * og_ref[0, rows, lanes].astype(F32)).astype(BF16)

    gates(0)

    def step(ci, carry):
        outputs(ci)
        gates(ci + 1)
        return carry

    lax.fori_loop(0, n_chunks - 1, step, 0)
    outputs(n_chunks - 1)


def _hgrn2(f, rest, norm_g, *, batch, seq, tm):
    f3 = f.reshape(batch, seq, HGRN_WIDTH)
    rest3 = rest.reshape(batch, seq, REST_WIDTH)
    blk = lambda col: pl.BlockSpec((1, tm, HGRN_WIDTH), lambda b, i: (b, i, col))
    out = pl.pallas_call(
        functools.partial(_hgrn_body, n_chunks=tm // HGRN_CHUNK),
        grid=(batch, seq // tm),
        in_specs=[blk(0), blk(0), blk(1), blk(2), _resident((1, HGRN_DV))],
        out_specs=blk(0),
        out_shape=jax.ShapeDtypeStruct((batch, seq, HGRN_WIDTH), BF16),
        scratch_shapes=[pltpu.VMEM((HGRN_HEADS, HGRN_DV, HGRN_DK), F32),
                        pltpu.VMEM((HGRN_CHUNK, HGRN_WIDTH), BF16),
                        pltpu.VMEM((HGRN_CHUNK, HGRN_WIDTH), BF16),
                        pltpu.VMEM((HGRN_HEADS, HGRN_CHUNK, HGRN_CHUNK), BF16),
                        pltpu.VMEM((1, HGRN_WIDTH), F32)],
        compiler_params=_params("arbitrary", "arbitrary"),
        name="hgrn2",
    )(f3, rest3, rest3, rest3, norm_g)
    return out.reshape(batch * seq, HGRN_WIDTH)


def _mix_out_body(x_ref, yat_ref, yh_ref, ga_ref, gh_ref, wpa_ref, wph_ref, wo_ref, g_ref, b_ref, o_ref):
    pa = lax.dot_general(yat_ref[...], wpa_ref[...], (((0,), (0,)), ((), ())), preferred_element_type=F32)
    ph = jnp.dot(yh_ref[...], wph_ref[...], preferred_element_type=F32)
    merged = ga_ref[...].astype(F32) * pa + gh_ref[...].astype(F32) * ph
    mix = jnp.dot(merged.astype(BF16), wo_ref[...], preferred_element_type=F32)
    y = DEEPNORM_ALPHA * x_ref[...] + mix
    o_ref[...] = _layer_norm(y, g_ref[...], b_ref[...])


def _mix_out(x2d, y_attn_t, y_hgrn, rest, w_pa, w_ph, w_out, g, b, *, tm):
    n = x2d.shape[0]
    row = pl.BlockSpec((tm, D_MODEL), lambda i: (i, 0))
    gate = lambda col: pl.BlockSpec((tm, D_MODEL), lambda i: (i, col))
    sq = _resident((D_MODEL, D_MODEL))
    return pl.pallas_call(
        _mix_out_body,
        grid=(n // tm,),
        in_specs=[row, pl.BlockSpec((ATTN_WIDTH, tm), lambda i: (0, i)), row, gate(3), gate(4), sq, sq, sq,
                  _resident((1, D_MODEL)), _resident((1, D_MODEL))],
        out_specs=row,
        out_shape=jax.ShapeDtypeStruct((n, D_MODEL), F32),
        compiler_params=_params("arbitrary"),
        name="mix_out",
    )(x2d, y_attn_t, y_hgrn, rest, rest, w_pa, w_ph, w_out, g, b)


def _tile(n, want):
    t = min(n, want)
    assert n % t == 0, (n, t)
    return t


def kernel(x, ln1_g, ln1_b, ffn1_w1, ffn1_w3, ffn1_w2, ln2_g, ln2_b, w_in, b_in, attn_sinks, hgrn_lb_logits,
           hgrn_norm_g, w_proj_attn, w_proj_hgrn, w_out, ln3_g, ln3_b, ffn2_w1, ffn2_w3, ffn2_w2):
    batch, seq, d = x.shape
    assert d == D_MODEL and seq % (ATTN_TILE_BLOCKS * ATTN_BLOCK) == 0
    n = batch * seq
    tm = _tile(seq, 512)
    lb_all = jnp.cumsum(jax.nn.softmax(hgrn_lb_logits.astype(F32), axis=0), axis=0)
    rope = _rope_tables(seq)
    bf = lambda w: w.astype(BF16)
    h = x.reshape(n, d)
    for l in range(DEPTH):
        h = _ffn_ln(h, bf(ffn1_w1[l]), bf(ffn1_w3[l]), bf(ffn1_w2[l]), ln1_g[l:l + 1], ln1_b[l:l + 1], tm=tm)
        qt, vt, k, f, rest = _in_proj(h, w_in[l], b_in[l], lb_all[l:l + 1], rope, tm=tm, seq=seq)
        y_attn_t = _swa_attn(attn_sinks[l], qt, k, vt, batch=batch, seq=seq)
        y_hgrn = _hgrn2(f, rest, hgrn_norm_g[l:l + 1], batch=batch, seq=seq, tm=tm)
        h = _mix_out(h, y_attn_t, y_hgrn, rest, bf(w_proj_attn[l]), bf(w_proj_hgrn[l]), bf(w_out[l]),
                     ln2_g[l:l + 1], ln2_b[l:l + 1], tm=tm)
        h = _ffn_ln(h, bf(ffn2_w1[l]), bf(ffn2_w3[l]), bf(ffn2_w2[l]), ln3_g[l:l + 1], ln3_b[l:l + 1], tm=tm)
    return h.reshape(batch, seq, d)
```

```python
import functools

import jax
import jax.numpy as jnp
from jax import lax
from jax.experimental import pallas as pl
from jax.experimental.pallas import tpu as pltpu

D_MODEL = 1024
DEPTH = 1
N_Q_HEADS = 16
N_KV_HEADS = 4
HEAD_DIM = 64
GQA_GROUP = N_Q_HEADS // N_KV_HEADS
WINDOW = 128
ATTN_BLOCK = 128
ROPE_THETA = 500000.0
ROPE_DIM = HEAD_DIM // 4
ROPE_HALF = ROPE_DIM // 2
HGRN_HEADS = 8
HGRN_DK = 128
HGRN_DV = 128
HGRN_CHUNK = 64
D_FF = 2816
ATTN_WIDTH = N_Q_HEADS * HEAD_DIM
KV_WIDTH = N_KV_HEADS * HEAD_DIM
HGRN_WIDTH = HGRN_HEADS * HGRN_DK
D_IN = ATTN_WIDTH + 2 * KV_WIDTH + 4 * HGRN_WIDTH + 2 * D_MODEL
DEEPNORM_ALPHA = (2 * DEPTH) ** 0.25
LN_EPS = 1e-5
RMS_EPS = 1e-6
NEG_INF = -1e30
LOG2_E = 1.4426950408889634

LANES = 128
F32_SUBLANES = 8
VMEM_LIMIT_BYTES = 56 * 1024 * 1024

F32 = jnp.float32
BF16 = jnp.bfloat16

assert WINDOW == ATTN_BLOCK and ROPE_HALF == F32_SUBLANES


def _layer_norm(y, g, b):
    mu = jnp.mean(y, axis=-1, keepdims=True)
    yc = y - mu
    var = jnp.mean(yc * yc, axis=-1, keepdims=True)
    return yc * lax.rsqrt(var + LN_EPS) * g + b


def _resident(shape):
    return pl.BlockSpec(shape, lambda *_: (0,) * len(shape), pipeline_mode=pl.Buffered(1))


def _params(*semantics):
    return pltpu.CompilerParams(dimension_semantics=semantics, vmem_limit_bytes=VMEM_LIMIT_BYTES)


LN_SLABS = 8


def _zero_after(x):
    bits = pltpu.bitcast(x, jnp.uint32)
    tiles = [bits[r:r + F32_SUBLANES, c:c + LANES]
             for r in range(0, x.shape[0], F32_SUBLANES) for c in range(0, x.shape[1], LANES)]
    folded = functools.reduce(jnp.bitwise_or, tiles)
    return pltpu.bitcast((folded >> 16) >> 16, F32)


def _ffn_ln_body(x_ref, w1_ref, w3_ref, w2_ref, g_ref, b_ref, o_ref, acc_ref, y_ref, *, ff_chunk):
    @pl.when(pl.program_id(0) == 0)
    def _():
        y_ref[...] = jnp.zeros_like(y_ref)

    tm = x_ref.shape[0]
    n_slabs, slab_rows = LN_SLABS, tm // LN_SLABS
    n_chunks = D_FF // ff_chunk
    assert n_slabs <= n_chunks - 1
    x = x_ref[...]
    xb = x.astype(BF16)
    for c in range(n_chunks):
        cols = slice(c * ff_chunk, (c + 1) * ff_chunk)
        h1 = jnp.dot(xb, w1_ref[:, cols], preferred_element_type=F32)
        h3 = jnp.dot(xb, w3_ref[:, cols], preferred_element_type=F32)
        a = ((h1 * jnp.tanh(h1) + h1) * h3).astype(BF16)
        part = jnp.dot(a, w2_ref[cols, :], preferred_element_type=F32)
        if c == 0:
            acc_ref[...] = DEEPNORM_ALPHA * x + part
        elif c < n_chunks - 1:
            acc_ref[...] += part
        else:
            y_ref[...] = acc_ref[...] + part
        if c < n_slabs:
            rows = slice(c * slab_rows, (c + 1) * slab_rows)
            ln = _layer_norm(y_ref[rows, :], g_ref[...], b_ref[...])
            o_ref[rows, :] = ln
            acc_ref[0:F32_SUBLANES, 0:LANES] += _zero_after(ln)


def _ffn_ln(x2d, w1, w3, w2, g, b, *, tm, ff_chunk=256):
    n = x2d.shape[0]
    tiles = n // tm
    return pl.pallas_call(
        functools.partial(_ffn_ln_body, ff_chunk=ff_chunk),
        grid=(tiles + 1,),
        in_specs=[pl.BlockSpec((tm, D_MODEL), lambda i: (jnp.minimum(i, tiles - 1), 0)),
                  _resident((D_MODEL, D_FF)), _resident((D_MODEL, D_FF)), _resident((D_FF, D_MODEL)),
                  _resident((1, D_MODEL)), _resident((1, D_MODEL))],
        out_specs=pl.BlockSpec((tm, D_MODEL), lambda i: (jnp.maximum(i - 1, 0), 0)),
        out_shape=jax.ShapeDtypeStruct((n, D_MODEL), F32),
        scratch_shapes=[pltpu.VMEM((tm, D_MODEL), F32), pltpu.VMEM((tm, D_MODEL), F32)],
        compiler_params=_params("arbitrary"),
        name="ffn_ln",
    )(x2d, w1, w3, w2, g, b)


PROJ_CHUNK = 512
QV_ROWS = ATTN_WIDTH + KV_WIDTH
QV_CHUNK = 256
REST_WIDTH = 3 * HGRN_WIDTH + 2 * D_MODEL
TOK_WIDTH = HGRN_WIDTH + REST_WIDTH + KV_WIDTH


def _rope_lanes(t, c, s_lo, s_hi):
    return t * c + pltpu.roll(t, ROPE_HALF, axis=1) * s_hi + pltpu.roll(t, LANES - ROPE_HALF, axis=1) * s_lo


def _in_proj_body(x_ref, wt_ref, bt_ref, w_ref, b_ref, fa_ref, fb_ref, c_ref, slo_ref, shi_ref, cos_ref, sin_ref,
                  qt_ref, vt_ref, k_ref, f_ref, rest_ref):
    xb = x_ref[...].astype(BF16)
    q_scale = HEAD_DIM ** -0.5 * LOG2_E
    cos_t, sin_t = cos_ref[...], sin_ref[...]
    for j in range(QV_ROWS // QV_CHUNK):
        rows = slice(j * QV_CHUNK, (j + 1) * QV_CHUNK)
        pt = lax.dot_general(wt_ref[rows, :], xb, (((1,), (1,)), ((), ())), preferred_element_type=F32)
        pt = pt + bt_ref[rows, :]
        if j * QV_CHUNK < ATTN_WIDTH:
            for h in range(QV_CHUNK // HEAD_DIM):
                r0 = h * HEAD_DIM
                t1 = pt[r0:r0 + ROPE_HALF]
                t2 = pt[r0 + ROPE_HALF:r0 + ROPE_DIM]
                rot = jnp.concatenate([t1 * cos_t - t2 * sin_t, t2 * cos_t + t1 * sin_t,
                                       pt[r0 + ROPE_DIM:r0 + HEAD_DIM]], axis=0)
                qt_ref[j * QV_CHUNK + r0:j * QV_CHUNK + r0 + HEAD_DIM, :] = (rot * q_scale).astype(BF16)
        else:
            vt_ref[...] = pt.astype(BF16)
    c, s_lo, s_hi = c_ref[...], slo_ref[...], shi_ref[...]
    lo = 0
    while lo < TOK_WIDTH:
        width = min(PROJ_CHUNK, TOK_WIDTH - lo)
        cols = slice(lo, lo + width)
        p = jnp.dot(xb, w_ref[:, cols], preferred_element_type=F32) + b_ref[:, cols]
        kind = _tok_kind(lo)
        if kind == "forget":
            f_ref[:, cols] = fa_ref[:, cols] + fb_ref[:, cols] * jnp.tanh(p)
        elif kind == "silu":
            rest_ref[:, lo - HGRN_WIDTH:lo - HGRN_WIDTH + width] = (p * jnp.tanh(p) + p).astype(BF16)
        elif kind == "sigmoid":
            rest_ref[:, lo - HGRN_WIDTH:lo - HGRN_WIDTH + width] = (0.5 * jnp.tanh(p) + 0.5).astype(BF16)
        elif kind == "plain":
            rest_ref[:, lo - HGRN_WIDTH:lo - HGRN_WIDTH + width] = p.astype(BF16)
        else:
            for u in range(width // LANES):
                t = p[:, u * LANES:(u + 1) * LANES]
                k_ref[:, u * LANES:(u + 1) * LANES] = _rope_lanes(t, c, s_lo, s_hi).astype(BF16)
        lo += width


_TOK_GROUPS = (("forget", HGRN_WIDTH), ("silu", HGRN_WIDTH), ("plain", HGRN_WIDTH), ("silu", HGRN_WIDTH),
               ("sigmoid", D_MODEL), ("sigmoid", D_MODEL), ("rope_k", KV_WIDTH))


def _tok_kind(col):
    for kind, width in _TOK_GROUPS:
        if col < width:
            return kind
        col -= width
    raise ValueError(col)


def _tok_half_scale():
    return jnp.concatenate([jnp.full((w,), 0.5 if kind in ("forget", "silu", "sigmoid") else 1.0, F32)
                            for kind, w in _TOK_GROUPS])[None, :]


def _in_proj(x2d, w_in, b_in, lb, rope, *, tm, seq):
    n = x2d.shape[0]
    tiles_per_seq = seq // tm
    q_end, k_end, v_end = ATTN_WIDTH, ATTN_WIDTH + KV_WIDTH, ATTN_WIDTH + 2 * KV_WIDTH
    w_t = jnp.concatenate([w_in[:, :q_end], w_in[:, k_end:v_end]], axis=1).T.astype(BF16)
    b_t = jnp.concatenate([b_in[:q_end], b_in[k_end:v_end]])[:, None]
    half = _tok_half_scale()
    w_tok = (jnp.concatenate([w_in[:, v_end:], w_in[:, q_end:k_end]], axis=1) * half).astype(BF16)
    b_tok = jnp.concatenate([b_in[v_end:], b_in[q_end:k_end]])[None, :] * half
    f_a = lb + 0.5 * (1.0 - lb)
    f_b = 0.5 * (1.0 - lb)
    c, s_lo, s_hi, cos_t, sin_t = rope
    row = lambda w: pl.BlockSpec((tm, w), lambda i: (i, 0))
    col = lambda r: pl.BlockSpec((r, tm), lambda i: (0, i))
    tab = pl.BlockSpec((tm, LANES), lambda i: (i % tiles_per_seq, 0))
    tab_t = pl.BlockSpec((ROPE_HALF, tm), lambda i: (0, i % tiles_per_seq))
    return pl.pallas_call(
        _in_proj_body,
        grid=(n // tm,),
        in_specs=[row(D_MODEL), _resident((QV_ROWS, D_MODEL)), _resident((QV_ROWS, 1)),
                  _resident((D_MODEL, TOK_WIDTH)), _resident((1, TOK_WIDTH)),
                  _resident((1, HGRN_WIDTH)), _resident((1, HGRN_WIDTH)), tab, tab, tab, tab_t, tab_t],
        out_specs=[col(ATTN_WIDTH), col(KV_WIDTH), row(KV_WIDTH), row(HGRN_WIDTH), row(REST_WIDTH)],
        out_shape=[jax.ShapeDtypeStruct((ATTN_WIDTH, n), BF16),
                   jax.ShapeDtypeStruct((KV_WIDTH, n), BF16),
                   jax.ShapeDtypeStruct((n, KV_WIDTH), BF16),
                   jax.ShapeDtypeStruct((n, HGRN_WIDTH), F32),
                   jax.ShapeDtypeStruct((n, REST_WIDTH), BF16)],
        compiler_params=_params("arbitrary"),
        name="in_proj",
    )(x2d, w_t, b_t, w_tok, b_tok, f_a, f_b, c, s_lo, s_hi, cos_t, sin_t)


def _rope_tables(seq):
    pos = jnp.arange(seq, dtype=F32)
    inv_freq = ROPE_THETA ** (-jnp.arange(0, ROPE_DIM, 2, dtype=F32) / ROPE_DIM)
    ang = pos[:, None] * inv_freq[None, :]
    cos, sin = jnp.cos(ang), jnp.sin(ang)
    ones = jnp.ones((seq, HEAD_DIM - ROPE_DIM), F32)
    c = jnp.concatenate([cos, cos, ones], axis=1)
    s_lo = jnp.concatenate([-sin, jnp.zeros((seq, HEAD_DIM - ROPE_HALF), F32)], axis=1)
    s_hi = jnp.concatenate([jnp.zeros((seq, ROPE_HALF), F32), sin, 0.0 * ones], axis=1)
    rep = LANES // HEAD_DIM
    return tuple(jnp.tile(t, (1, rep)) for t in (c, s_lo, s_hi)) + (cos.T, sin.T)


ATTN_TILE_BLOCKS = 4


def _swa_body(sink_ref, qt_ref, k_ref, kh_ref, vt_ref, vth_ref, o_ref):
    first_tile = pl.program_id(1) == 0
    halo_bias = jnp.where(first_tile, NEG_INF, 0.0).astype(F32)
    wide = GQA_GROUP * ATTN_BLOCK
    key_i = lax.broadcasted_iota(jnp.int32, (ATTN_BLOCK, wide), 0)
    qry_i = lax.broadcasted_iota(jnp.int32, (ATTN_BLOCK, wide), 1) % ATTN_BLOCK
    from_prev = key_i > qry_i
    keep_prev = from_prev.astype(BF16)
    keep_cur = 1.0 - keep_prev
    zero_half = jnp.zeros((HEAD_DIM, ATTN_BLOCK), BF16)
    units = [(qb, kh) for qb in range(ATTN_TILE_BLOCKS) for kh in range(N_KV_HEADS)]

    def heads_of(kh):
        return [kh * GQA_GROUP + g for g in range(GQA_GROUP)]

    def toks_of(qb):
        return slice(qb * ATTN_BLOCK, (qb + 1) * ATTN_BLOCK)

    def scores(qb, kh):
        toks = toks_of(qb)
        k_prev = kh_ref[...] if qb == 0 else k_ref[(qb - 1) * ATTN_BLOCK:qb * ATTN_BLOCK, :]
        keys = jnp.concatenate([k_prev, k_ref[toks, :]], axis=0)
        pair = slice((kh // 2) * LANES, (kh // 2 + 1) * LANES)
        cols = []
        for h in heads_of(kh):
            qh = qt_ref[h * HEAD_DIM:(h + 1) * HEAD_DIM, toks]
            cols.append(jnp.concatenate([qh, zero_half] if kh % 2 == 0 else [zero_half, qh], axis=0))
        q_rhs = jnp.concatenate(cols, axis=1)
        return jnp.dot(keys[:, pair], q_rhs, preferred_element_type=F32)

    def finish(qb, kh, s):
        toks = toks_of(qb)
        heads = heads_of(kh)
        s_prev = s[:ATTN_BLOCK]
        if qb == 0:
            s_prev = s_prev + halo_bias
        c = jnp.where(from_prev, s_prev, s[ATTN_BLOCK:])
        sink = jnp.concatenate([jnp.full((1, ATTN_BLOCK), sink_ref[h] * LOG2_E, F32) for h in heads], axis=1)
        m = jnp.maximum(jnp.max(c, axis=0, keepdims=True), sink)
        p = jnp.exp2(c - m)
        denom = jnp.sum(p, axis=0, keepdims=True) + jnp.exp2(sink - m)
        pb = p.astype(BF16)
        p_both = jnp.concatenate([pb * keep_prev, pb * keep_cur], axis=0)
        vt_prev = vth_ref[...] if qb == 0 else vt_ref[:, (qb - 1) * ATTN_BLOCK:qb * ATTN_BLOCK]
        vt_both = jnp.concatenate([vt_prev[kh * HEAD_DIM:(kh + 1) * HEAD_DIM, :],
                                   vt_ref[kh * HEAD_DIM:(kh + 1) * HEAD_DIM, toks]], axis=1)
        ot = jnp.dot(vt_both, p_both, preferred_element_type=F32) * (1.0 / denom)
        for g, h in enumerate(heads):
            o_ref[h * HEAD_DIM:(h + 1) * HEAD_DIM, toks] = ot[:, g * ATTN_BLOCK:(g + 1) * ATTN_BLOCK].astype(BF16)

    s_next = scores(*units[0])
    for i, unit in enumerate(units):
        s_cur = s_next
        if i + 1 < len(units):
            s_next = scores(*units[i + 1])
        finish(*unit, s_cur)


def _swa_attn(sinks, qt, k, vt, *, batch, seq):
    tq = ATTN_TILE_BLOCKS * ATTN_BLOCK
    tiles = seq // tq
    main = lambda b, i: b * tiles + i
    halo = lambda b, i: jnp.maximum((b * tiles + i) * ATTN_TILE_BLOCKS - 1, 0)
    return pl.pallas_call(
        _swa_body,
        grid=(batch, tiles),
        in_specs=[pl.BlockSpec(memory_space=pltpu.SMEM),
                  pl.BlockSpec((ATTN_WIDTH, tq), lambda b, i: (0, main(b, i))),
                  pl.BlockSpec((tq, KV_WIDTH), lambda b, i: (main(b, i), 0)),
                  pl.BlockSpec((ATTN_BLOCK, KV_WIDTH), lambda b, i: (halo(b, i), 0)),
                  pl.BlockSpec((KV_WIDTH, tq), lambda b, i: (0, main(b, i))),
                  pl.BlockSpec((KV_WIDTH, ATTN_BLOCK), lambda b, i: (0, halo(b, i)))],
        out_specs=pl.BlockSpec((ATTN_WIDTH, tq), lambda b, i: (0, main(b, i))),
        out_shape=jax.ShapeDtypeStruct((ATTN_WIDTH, batch * seq), BF16),
        compiler_params=_params("arbitrary", "arbitrary"),
        name="swa_attn",
    )(sinks, qt, k, k, vt, vt)


def _cumprod_rows(f):
    row = lax.broadcasted_iota(jnp.int32, (F32_SUBLANES, LANES), 0)
    blocks, carry = [], None
    for j in range(f.shape[0] // F32_SUBLANES):
        x = f[j * F32_SUBLANES:(j + 1) * F32_SUBLANES]
        shift = 1
        while shift < F32_SUBLANES:
            x = x * jnp.where(row >= shift, pltpu.roll(x, shift, axis=0), 1.0)
            shift *= 2
        if carry is not None:
            x = x * carry
        carry = x[F32_SUBLANES - 1:F32_SUBLANES, :]
        blocks.append(x)
    return jnp.concatenate(blocks, axis=0)


def _hgrn_body(f_ref, q_ref, v_ref, og_ref, ng_ref, o_ref,
               state_ref, qd_ref, ke_ref, sc_ref, dec_ref, *, n_chunks):
    @pl.when(pl.program_id(1) == 0)
    def _():
        state_ref[...] = jnp.zeros_like(state_ref)

    c = HGRN_CHUNK
    ti = lax.broadcasted_iota(jnp.int32, (c, c), 0)
    si = lax.broadcasted_iota(jnp.int32, (c, c), 1)
    causal = ti >= si
    ng = ng_ref[...]
    nt_dims = (((1,), (1,)), ((), ()))

    def gates(ci):
        rows = pl.ds(pl.multiple_of(ci * c, c), c)
        for h in range(HGRN_HEADS):
            lanes = slice(h * HGRN_DK, (h + 1) * HGRN_DK)
            f = f_ref[0, rows, lanes]
            k = 1.0 - f
            decay = _cumprod_rows(f)
            inv = 1.0 / decay
            d_last = decay[c - 1:c, :]
            q_dec = (q_ref[0, rows, lanes].astype(F32) * decay).astype(BF16)
            k_inv = (k * inv).astype(BF16)
            sc = lax.dot_general(q_dec, k_inv, nt_dims, preferred_element_type=F32)
            sc_ref[h] = jnp.where(causal, sc, 0.0).astype(BF16)
            qd_ref[:, lanes] = q_dec
            ke_ref[:, lanes] = (k * (d_last * inv)).astype(BF16)
            dec_ref[:, lanes] = d_last

    def outputs(ci):
        rows = pl.ds(pl.multiple_of(ci * c, c), c)
        for h in range(HGRN_HEADS):
            lanes = slice(h * HGRN_DK, (h + 1) * HGRN_DK)
            v = v_ref[0, rows, lanes]
            st = state_ref[h]
            o = jnp.dot(sc_ref[h], v, preferred_element_type=F32)
            o += lax.dot_general(qd_ref[:, lanes], st.astype(BF16), nt_dims, preferred_element_type=F32)
            upd = lax.dot_general(v, ke_ref[:, lanes], (((0,), (0,)), ((), ())), preferred_element_type=F32)
            state_ref[h] = st * dec_ref[:, lanes] + upd
            o = o * lax.rsqrt(jnp.mean(o * o, axis=-1, keepdims=True) + RMS_EPS) * ng
            o_ref[0, rows, lanes] = (o * og_ref[0, rows, lanes].astype(F32)).astype(BF16)

    gates(0)

    def step(ci, carry):
        outputs(ci)
        gates(ci + 1)
        return carry

    lax.fori_loop(0, n_chunks - 1, step, 0)
    outputs(n_chunks - 1)


def _hgrn2(f, rest, norm_g, *, batch, seq, tm):
    f3 = f.reshape(batch, seq, HGRN_WIDTH)
    rest3 = rest.reshape(batch, seq, REST_WIDTH)
    blk = lambda col: pl.BlockSpec((1, tm, HGRN_WIDTH), lambda b, i: (b, i, col))
    out = pl.pallas_call(
        functools.partial(_hgrn_body, n_chunks=tm // HGRN_CHUNK),
        grid=(batch, seq // tm),
        in_specs=[blk(0), blk(0), blk(1), blk(2), _resident((1, HGRN_DV))],
        out_specs=blk(0),
        out_shape=jax.ShapeDtypeStruct((batch, seq, HGRN_WIDTH), BF16),
        scratch_shapes=[pltpu.VMEM((HGRN_HEADS, HGRN_DV, HGRN_DK), F32),
                        pltpu.VMEM((HGRN_CHUNK, HGRN_WIDTH), BF16),
                        pltpu.VMEM((HGRN_CHUNK, HGRN_WIDTH), BF16),
                        pltpu.VMEM((HGRN_HEADS, HGRN_CHUNK, HGRN_CHUNK), BF16),
                        pltpu.VMEM((1, HGRN_WIDTH), F32)],
        compiler_params=_params("arbitrary", "arbitrary"),
        name="hgrn2",
    )(f3, rest3, rest3, rest3, norm_g)
    return out.reshape(batch * seq, HGRN_WIDTH)


def _mix_out_body(x_ref, yat_ref, yh_ref, ga_ref, gh_ref, wpa_ref, wph_ref, wo_ref, g_ref, b_ref, o_ref):
    pa = lax.dot_general(yat_ref[...], wpa_ref[...], (((0,), (0,)), ((), ())), preferred_element_type=F32)
    ph = jnp.dot(yh_ref[...], wph_ref[...], preferred_element_type=F32)
    merged = ga_ref[...].astype(F32) * pa + gh_ref[...].astype(F32) * ph
    mix = jnp.dot(merged.astype(BF16), wo_ref[...], preferred_element_type=F32)
    y = DEEPNORM_ALPHA * x_ref[...] + mix
    o_ref[...] = _layer_norm(y, g_ref[...], b_ref[...])


def _mix_out(x2d, y_attn_t, y_hgrn, rest, w_pa, w_ph, w_out, g, b, *, tm):
    n = x2d.shape[0]
    row = pl.BlockSpec((tm, D_MODEL), lambda i: (i, 0))
    gate = lambda col: pl.BlockSpec((tm, D_MODEL), lambda i: (i, col))
    sq = _resident((D_MODEL, D_MODEL))
    return pl.pallas_call(
        _mix_out_body,
        grid=(n // tm,),
        in_specs=[row, pl.BlockSpec((ATTN_WIDTH, tm), lambda i: (0, i)), row, gate(3), gate(4), sq, sq, sq,
                  _resident((1, D_MODEL)), _resident((1, D_MODEL))],
        out_specs=row,
        out_shape=jax.ShapeDtypeStruct((n, D_MODEL), F32),
        compiler_params=_params("arbitrary"),
        name="mix_out",
    )(x2d, y_attn_t, y_hgrn, rest, rest, w_pa, w_ph, w_out, g, b)


def _tile(n, want):
    t = min(n, want)
    assert n % t == 0, (n, t)
    return t


def kernel(x, ln1_g, ln1_b, ffn1_w1, ffn1_w3, ffn1_w2, ln2_g, ln2_b, w_in, b_in, attn_sinks, hgrn_lb_logits,
           hgrn_norm_g, w_proj_attn, w_proj_hgrn, w_out, ln3_g, ln3_b, ffn2_w1, ffn2_w3, ffn2_w2):
    batch, seq, d = x.shape
    assert d == D_MODEL and seq % (ATTN_TILE_BLOCKS * ATTN_BLOCK) == 0
    n = batch * seq
    tm = _tile(seq, 512)
    lb_all = jnp.cumsum(jax.nn.softmax(hgrn_lb_logits.astype(F32), axis=0), axis=0)
    rope = _rope_tables(seq)
    bf = lambda w: w.astype(BF16)
    h = x.reshape(n, d)
    for l in range(DEPTH):
        h = _ffn_ln(h, bf(0.5 * ffn1_w1[l]), bf(ffn1_w3[l]), bf(0.5 * ffn1_w2[l]), ln1_g[l:l + 1], ln1_b[l:l + 1], tm=tm)
        qt, vt, k, f, rest = _in_proj(h, w_in[l], b_in[l], lb_all[l:l + 1], rope, tm=tm, seq=seq)
        y_attn_t = _swa_attn(attn_sinks[l], qt, k, vt, batch=batch, seq=seq)
        y_hgrn = _hgrn2(f, rest, hgrn_norm_g[l:l + 1], batch=batch, seq=seq, tm=tm)
        h = _mix_out(h, y_attn_t, y_hgrn, rest, bf(w_proj_attn[l]), bf(w_proj_hgrn[l]), bf(w_out[l]),
                     ln2_g[l:l + 1], ln2_b[l:l + 1], tm=tm)
        h = _ffn_ln(h, bf(0.5 * ffn2_w1[l]), bf(ffn2_w3[l]), bf(0.5 * ffn2_w2[l]), ln3_g[l:l + 1], ln3_b[l:l + 1], tm=tm)
    return h.reshape(batch, seq, d)
```

```python
import functools

import jax
import jax.numpy as jnp
from jax import lax
from jax.experimental import pallas as pl
from jax.experimental.pallas import tpu as pltpu

D_MODEL = 1024
DEPTH = 1
N_Q_HEADS = 16
N_KV_HEADS = 4
HEAD_DIM = 64
GQA_GROUP = N_Q_HEADS // N_KV_HEADS
WINDOW = 128
ATTN_BLOCK = 128
ROPE_THETA = 500000.0
ROPE_DIM = HEAD_DIM // 4
ROPE_HALF = ROPE_DIM // 2
HGRN_HEADS = 8
HGRN_DK = 128
HGRN_DV = 128
HGRN_CHUNK = 64
D_FF = 2816
ATTN_WIDTH = N_Q_HEADS * HEAD_DIM
KV_WIDTH = N_KV_HEADS * HEAD_DIM
HGRN_WIDTH = HGRN_HEADS * HGRN_DK
D_IN = ATTN_WIDTH + 2 * KV_WIDTH + 4 * HGRN_WIDTH + 2 * D_MODEL
DEEPNORM_ALPHA = (2 * DEPTH) ** 0.25
LN_EPS = 1e-5
RMS_EPS = 1e-6
NEG_INF = -1e30
LOG2_E = 1.4426950408889634

LANES = 128
F32_SUBLANES = 8
VMEM_LIMIT_BYTES = 56 * 1024 * 1024

F32 = jnp.float32
BF16 = jnp.bfloat16

assert WINDOW == ATTN_BLOCK and ROPE_HALF == F32_SUBLANES


def _layer_norm(y, g, b):
    mu = jnp.mean(y, axis=-1, keepdims=True)
    yc = y - mu
    var = jnp.mean(yc * yc, axis=-1, keepdims=True)
    return yc * lax.rsqrt(var + LN_EPS) * g + b


def _resident(shape):
    return pl.BlockSpec(shape, lambda *_: (0,) * len(shape), pipeline_mode=pl.Buffered(1))


def _params(*semantics):
    return pltpu.CompilerParams(dimension_semantics=semantics, vmem_limit_bytes=VMEM_LIMIT_BYTES)


LN_SLABS = 8


def _zero_after(x):
    bits = pltpu.bitcast(x, jnp.uint32)
    tiles = [bits[r:r + F32_SUBLANES, c:c + LANES]
             for r in range(0, x.shape[0], F32_SUBLANES) for c in range(0, x.shape[1], LANES)]
    folded = functools.reduce(jnp.bitwise_or, tiles)
    return pltpu.bitcast((folded >> 16) >> 16, F32)


def _ffn_ln_body(x_ref, w1_ref, w3_ref, w2_ref, g_ref, b_ref, o_ref, acc_ref, y_ref, *, ff_chunk):
    @pl.when(pl.program_id(0) == 0)
    def _():
        y_ref[...] = jnp.zeros_like(y_ref)

    tm = x_ref.shape[0]
    n_slabs, slab_rows = LN_SLABS, tm // LN_SLABS
    n_chunks = D_FF // ff_chunk
    assert n_slabs <= n_chunks - 1
    x = x_ref[...]
    xb = x.astype(BF16)
    for c in range(n_chunks):
        cols = slice(c * ff_chunk, (c + 1) * ff_chunk)
        h1 = jnp.dot(xb, w1_ref[:, cols], preferred_element_type=F32)
        h3 = jnp.dot(xb, w3_ref[:, cols], preferred_element_type=F32)
        a = ((h1 * jnp.tanh(h1) + h1) * h3).astype(BF16)
        part = jnp.dot(a, w2_ref[cols, :], preferred_element_type=F32)
        if c == 0:
            acc_ref[...] = DEEPNORM_ALPHA * x + part
        elif c < n_chunks - 1:
            acc_ref[...] += part
        else:
            y_ref[...] = acc_ref[...] + part
        if c < n_slabs:
            rows = slice(c * slab_rows, (c + 1) * slab_rows)
            ln = _layer_norm(y_ref[rows, :], g_ref[...], b_ref[...])
            o_ref[rows, :] = ln
            acc_ref[0:F32_SUBLANES, 0:LANES] += _zero_after(ln)


def _ffn_ln(x2d, w1, w3, w2, g, b, *, tm, ff_chunk=256):
    n = x2d.shape[0]
    tiles = n // tm
    return pl.pallas_call(
        functools.partial(_ffn_ln_body, ff_chunk=ff_chunk),
        grid=(tiles + 1,),
        in_specs=[pl.BlockSpec((tm, D_MODEL), lambda i: (jnp.minimum(i, tiles - 1), 0)),
                  _resident((D_MODEL, D_FF)), _resident((D_MODEL, D_FF)), _resident((D_FF, D_MODEL)),
                  _resident((1, D_MODEL)), _resident((1, D_MODEL))],
        out_specs=pl.BlockSpec((tm, D_MODEL), lambda i: (jnp.maximum(i - 1, 0), 0)),
        out_shape=jax.ShapeDtypeStruct((n, D_MODEL), F32),
        scratch_shapes=[pltpu.VMEM((tm, D_MODEL), F32), pltpu.VMEM((tm, D_MODEL), F32)],
        compiler_params=_params("arbitrary"),
        name="ffn_ln",
    )(x2d, w1, w3, w2, g, b)


PROJ_CHUNK = 512
QV_ROWS = ATTN_WIDTH + KV_WIDTH
QV_CHUNK = 256
REST_WIDTH = 3 * HGRN_WIDTH + 2 * D_MODEL
TOK_WIDTH = HGRN_WIDTH + REST_WIDTH + KV_WIDTH


def _rope_lanes(t, c, s_lo, s_hi):
    return t * c + pltpu.roll(t, ROPE_HALF, axis=1) * s_hi + pltpu.roll(t, LANES - ROPE_HALF, axis=1) * s_lo


def _in_proj_body(x_ref, wt_ref, bt_ref, w_ref, b_ref, fa_ref, fb_ref, c_ref, slo_ref, shi_ref, cos_ref, sin_ref,
                  qt_ref, vt_ref, k_ref, f_ref, rest_ref):
    xb = x_ref[...].astype(BF16)
    q_scale = HEAD_DIM ** -0.5 * LOG2_E
    cos_t, sin_t = cos_ref[...], sin_ref[...]
    for j in range(QV_ROWS // QV_CHUNK):
        rows = slice(j * QV_CHUNK, (j + 1) * QV_CHUNK)
        pt = lax.dot_general(wt_ref[rows, :], xb, (((1,), (1,)), ((), ())), preferred_element_type=F32)
        pt = pt + bt_ref[rows, :]
        if j * QV_CHUNK < ATTN_WIDTH:
            for h in range(QV_CHUNK // HEAD_DIM):
                r0 = h * HEAD_DIM
                t1 = pt[r0:r0 + ROPE_HALF]
                t2 = pt[r0 + ROPE_HALF:r0 + ROPE_DIM]
                rot = jnp.concatenate([t1 * cos_t - t2 * sin_t, t2 * cos_t + t1 * sin_t,
                                       pt[r0 + ROPE_DIM:r0 + HEAD_DIM]], axis=0)
                qt_ref[j * QV_CHUNK + r0:j * QV_CHUNK + r0 + HEAD_DIM, :] = (rot * q_scale).astype(BF16)
        else:
            vt_ref[...] = pt.astype(BF16)
    c, s_lo, s_hi = c_ref[...], slo_ref[...], shi_ref[...]
    lo = 0
    while lo < TOK_WIDTH:
        width = min(PROJ_CHUNK, TOK_WIDTH - lo)
        cols = slice(lo, lo + width)
        p = jnp.dot(xb, w_ref[:, cols], preferred_element_type=F32) + b_ref[:, cols]
        kind = _tok_kind(lo)
        if kind == "forget":
            f_ref[:, cols] = fa_ref[:, cols] + fb_ref[:, cols] * jnp.tanh(p)
        elif kind == "silu":
            rest_ref[:, lo - HGRN_WIDTH:lo - HGRN_WIDTH + width] = (p * jnp.tanh(p) + p).astype(BF16)
        elif kind == "sigmoid":
            rest_ref[:, lo - HGRN_WIDTH:lo - HGRN_WIDTH + width] = (0.5 * jnp.tanh(p) + 0.5).astype(BF16)
        elif kind == "plain":
            rest_ref[:, lo - HGRN_WIDTH:lo - HGRN_WIDTH + width] = p.astype(BF16)
        else:
            for u in range(width // LANES):
                t = p[:, u * LANES:(u + 1) * LANES]
                k_ref[:, u * LANES:(u + 1) * LANES] = _rope_lanes(t, c, s_lo, s_hi).astype(BF16)
        lo += width


_TOK_GROUPS = (("forget", HGRN_WIDTH), ("silu", HGRN_WIDTH), ("plain", HGRN_WIDTH), ("silu", HGRN_WIDTH),
               ("sigmoid", D_MODEL), ("sigmoid", D_MODEL), ("rope_k", KV_WIDTH))


def _tok_kind(col):
    for kind, width in _TOK_GROUPS:
        if col < width:
            return kind
        col -= width
    raise ValueError(col)


def _tok_half_scale():
    return jnp.concatenate([jnp.full((w,), 0.5 if kind in ("forget", "silu", "sigmoid") else 1.0, F32)
                            for kind, w in _TOK_GROUPS])[None, :]


def _in_proj(x2d, w_in, b_in, lb, rope, *, tm, seq):
    n = x2d.shape[0]
    tiles_per_seq = seq // tm
    q_end, k_end, v_end = ATTN_WIDTH, ATTN_WIDTH + KV_WIDTH, ATTN_WIDTH + 2 * KV_WIDTH
    w_t = jnp.concatenate([w_in[:, :q_end], w_in[:, k_end:v_end]], axis=1).T.astype(BF16)
    b_t = jnp.concatenate([b_in[:q_end], b_in[k_end:v_end]])[:, None]
    half = _tok_half_scale()
    w_tok = (jnp.concatenate([w_in[:, v_end:], w_in[:, q_end:k_end]], axis=1) * half).astype(BF16)
    b_tok = jnp.concatenate([b_in[v_end:], b_in[q_end:k_end]])[None, :] * half
    f_a = lb + 0.5 * (1.0 - lb)
    f_b = 0.5 * (1.0 - lb)
    c, s_lo, s_hi, cos_t, sin_t = rope
    row = lambda w: pl.BlockSpec((tm, w), lambda i: (i, 0))
    col = lambda r: pl.BlockSpec((r, tm), lambda i: (0, i))
    tab = pl.BlockSpec((tm, LANES), lambda i: (i % tiles_per_seq, 0))
    tab_t = pl.BlockSpec((ROPE_HALF, tm), lambda i: (0, i % tiles_per_seq))
    return pl.pallas_call(
        _in_proj_body,
        grid=(n // tm,),
        in_specs=[row(D_MODEL), _resident((QV_ROWS, D_MODEL)), _resident((QV_ROWS, 1)),
                  _resident((D_MODEL, TOK_WIDTH)), _resident((1, TOK_WIDTH)),
                  _resident((1, HGRN_WIDTH)), _resident((1, HGRN_WIDTH)), tab, tab, tab, tab_t, tab_t],
        out_specs=[col(ATTN_WIDTH), col(KV_WIDTH), row(KV_WIDTH), row(HGRN_WIDTH), row(REST_WIDTH)],
        out_shape=[jax.ShapeDtypeStruct((ATTN_WIDTH, n), BF16),
                   jax.ShapeDtypeStruct((KV_WIDTH, n), BF16),
                   jax.ShapeDtypeStruct((n, KV_WIDTH), BF16),
                   jax.ShapeDtypeStruct((n, HGRN_WIDTH), F32),
                   jax.ShapeDtypeStruct((n, REST_WIDTH), BF16)],
        compiler_params=_params("arbitrary"),
        name="in_proj",
    )(x2d, w_t, b_t, w_tok, b_tok, f_a, f_b, c, s_lo, s_hi, cos_t, sin_t)


def _rope_tables(seq):
    pos = jnp.arange(seq, dtype=F32)
    inv_freq = ROPE_THETA ** (-jnp.arange(0, ROPE_DIM, 2, dtype=F32) / ROPE_DIM)
    ang = pos[:, None] * inv_freq[None, :]
    cos, sin = jnp.cos(ang), jnp.sin(ang)
    ones = jnp.ones((seq, HEAD_DIM - ROPE_DIM), F32)
    c = jnp.concatenate([cos, cos, ones], axis=1)
    s_lo = jnp.concatenate([-sin, jnp.zeros((seq, HEAD_DIM - ROPE_HALF), F32)], axis=1)
    s_hi = jnp.concatenate([jnp.zeros((seq, ROPE_HALF), F32), sin, 0.0 * ones], axis=1)
    rep = LANES // HEAD_DIM
    return tuple(jnp.tile(t, (1, rep)) for t in (c, s_lo, s_hi)) + (cos.T, sin.T)


ATTN_TILE_BLOCKS = 4


def _swa_body(sink_ref, qt_ref, k_ref, kh_ref, vt_ref, vth_ref, o_ref):
    first_tile = pl.program_id(1) == 0
    halo_bias = jnp.where(first_tile, NEG_INF, 0.0).astype(F32)
    wide = GQA_GROUP * ATTN_BLOCK
    key_i = lax.broadcasted_iota(jnp.int32, (ATTN_BLOCK, wide), 0)
    qry_i = lax.broadcasted_iota(jnp.int32, (ATTN_BLOCK, wide), 1) % ATTN_BLOCK
    from_prev = key_i > qry_i
    keep_prev = from_prev.astype(BF16)
    keep_cur = 1.0 - keep_prev
    zero_half = jnp.zeros((HEAD_DIM, ATTN_BLOCK), BF16)
    units = [(qb, kh) for qb in range(ATTN_TILE_BLOCKS) for kh in range(N_KV_HEADS)]

    def heads_of(kh):
        return [kh * GQA_GROUP + g for g in range(GQA_GROUP)]

    def toks_of(qb):
        return slice(qb * ATTN_BLOCK, (qb + 1) * ATTN_BLOCK)

    def scores(qb, kh):
        toks = toks_of(qb)
        k_prev = kh_ref[...] if qb == 0 else k_ref[(qb - 1) * ATTN_BLOCK:qb * ATTN_BLOCK, :]
        keys = jnp.concatenate([k_prev, k_ref[toks, :]], axis=0)
        pair = slice((kh // 2) * LANES, (kh // 2 + 1) * LANES)
        cols = []
        for h in heads_of(kh):
            qh = qt_ref[h * HEAD_DIM:(h + 1) * HEAD_DIM, toks]
            cols.append(jnp.concatenate([qh, zero_half] if kh % 2 == 0 else [zero_half, qh], axis=0))
        q_rhs = jnp.concatenate(cols, axis=1)
        return jnp.dot(keys[:, pair], q_rhs, preferred_element_type=F32)

    def finish(qb, kh, s):
        toks = toks_of(qb)
        heads = heads_of(kh)
        s_prev = s[:ATTN_BLOCK]
        if qb == 0:
            s_prev = s_prev + halo_bias
        c = jnp.where(from_prev, s_prev, s[ATTN_BLOCK:])
        sink = jnp.concatenate([jnp.full((1, ATTN_BLOCK), sink_ref[h] * LOG2_E, F32) for h in heads], axis=1)
        m = jnp.maximum(jnp.max(c, axis=0, keepdims=True), sink)
        p = jnp.exp2(c - m)
        denom = jnp.sum(p, axis=0, keepdims=True) + jnp.exp2(sink - m)
        pb = p.astype(BF16)
        p_both = jnp.concatenate([pb * keep_prev, pb * keep_cur], axis=0)
        vt_prev = vth_ref[...] if qb == 0 else vt_ref[:, (qb - 1) * ATTN_BLOCK:qb * ATTN_BLOCK]
        vt_both = jnp.concatenate([vt_prev[kh * HEAD_DIM:(kh + 1) * HEAD_DIM, :],
                                   vt_ref[kh * HEAD_DIM:(kh + 1) * HEAD_DIM, toks]], axis=1)
        ot = jnp.dot(vt_both, p_both, preferred_element_type=F32) * (1.0 / denom)
        for g, h in enumerate(heads):
            o_ref[h * HEAD_DIM:(h + 1) * HEAD_DIM, toks] = ot[:, g * ATTN_BLOCK:(g + 1) * ATTN_BLOCK].astype(BF16)

    s_next = scores(*units[0])
    for i, unit in enumerate(units):
        s_cur = s_next
        if i + 1 < len(units):
            s_next = scores(*units[i + 1])
        finish(*unit, s_cur)


def _swa_attn(sinks, qt, k, vt, *, batch, seq):
    tq = ATTN_TILE_BLOCKS * ATTN_BLOCK
    tiles = seq // tq
    main = lambda b, i: b * tiles + i
    halo = lambda b, i: jnp.maximum((b * tiles + i) * ATTN_TILE_BLOCKS - 1, 0)
    return pl.pallas_call(
        _swa_body,
        grid=(batch, tiles),
        in_specs=[pl.BlockSpec(memory_space=pltpu.SMEM),
                  pl.BlockSpec((ATTN_WIDTH, tq), lambda b, i: (0, main(b, i))),
                  pl.BlockSpec((tq, KV_WIDTH), lambda b, i: (main(b, i), 0)),
                  pl.BlockSpec((ATTN_BLOCK, KV_WIDTH), lambda b, i: (halo(b, i), 0)),
                  pl.BlockSpec((KV_WIDTH, tq), lambda b, i: (0, main(b, i))),
                  pl.BlockSpec((KV_WIDTH, ATTN_BLOCK), lambda b, i: (0, halo(b, i)))],
        out_specs=pl.BlockSpec((ATTN_WIDTH, tq), lambda b, i: (0, main(b, i))),
        out_shape=jax.ShapeDtypeStruct((ATTN_WIDTH, batch * seq), BF16),
        compiler_params=_params("arbitrary", "arbitrary"),
        name="swa_attn",
    )(sinks, qt, k, k, vt, vt)


def _cumprod_rows(f):
    row = lax.broadcasted_iota(jnp.int32, (F32_SUBLANES, LANES), 0)
    blocks, carry = [], None
    for j in range(f.shape[0] // F32_SUBLANES):
        x = f[j * F32_SUBLANES:(j + 1) * F32_SUBLANES]
        shift = 1
        while shift < F32_SUBLANES:
            x = x * jnp.where(row >= shift, pltpu.roll(x, shift, axis=0), 1.0)
            shift *= 2
        if carry is not None:
            x = x * carry
        carry = x[F32_SUBLANES - 1:F32_SUBLANES, :]
        blocks.append(x)
    return jnp.concatenate(blocks, axis=0)


def _hgrn_body(f_ref, q_ref, v_ref, og_ref, ng_ref, o_ref,
               state_ref, qd_ref, ke_ref, sc_ref, dec_ref, *, n_chunks):
    @pl.when(pl.program_id(1) == 0)
    def _():
        state_ref[...] = jnp.zeros_like(state_ref)

    c = HGRN_CHUNK
    ti = lax.broadcasted_iota(jnp.int32, (c, c), 0)
    si = lax.broadcasted_iota(jnp.int32, (c, c), 1)
    causal = ti >= si
    ng = ng_ref[...]
    nt_dims = (((1,), (1,)), ((), ()))

    def gates(ci):
        rows = pl.ds(pl.multiple_of(ci * c, c), c)
        for h in range(HGRN_HEADS):
            lanes = slice(h * HGRN_DK, (h + 1) * HGRN_DK)
            f = f_ref[0, rows, lanes]
            k = 1.0 - f
            decay = _cumprod_rows(f)
            inv = 1.0 / decay
            d_last = decay[c - 1:c, :]
            q_dec = (q_ref[0, rows, lanes].astype(F32) * decay).astype(BF16)
            k_inv = (k * inv).astype(BF16)
            sc = lax.dot_general(q_dec, k_inv, nt_dims, preferred_element_type=F32)
            sc_ref[h] = jnp.where(causal, sc, 0.0).astype(BF16)
            qd_ref[:, lanes] = q_dec
            ke_ref[:, lanes] = (k * (d_last * inv)).astype(BF16)
            dec_ref[:, lanes] = d_last

    def outputs(ci):
        rows = pl.ds(pl.multiple_of(ci * c, c), c)
        for h in range(HGRN_HEADS):
            lanes = slice(h * HGRN_DK, (h + 1) * HGRN_DK)
            v = v_ref[0, rows, lanes]
            st = state_ref[h]
            o = jnp.dot(sc_ref[h], v, preferred_element_type=F32)
            o += lax.dot_general(qd_ref[:, lanes], st.astype(BF16), nt_dims, preferred_element_type=F32)
            upd = lax.dot_general(v, ke_ref[:, lanes], (((0,), (0,)), ((), ())), preferred_element_type=F32)
            state_ref[h] = st * dec_ref[:, lanes] + upd
            o = o * lax.rsqrt(jnp.mean(o * o, axis=-1, keepdims=True) + RMS_EPS) * ng
            o_ref[0, rows, lanes] = (o * og_ref[0, rows, lanes].astype(F32)).astype(BF16)

    gates(0)

    def step(ci, carry):
        outputs(ci)
        gates(ci + 1)
        return carry

    lax.fori_loop(0, n_chunks - 1, step, 0)
    outputs(n_chunks - 1)


def _hgrn2(f, rest, norm_g, *, batch, seq, tm):
    f3 = f.reshape(batch, seq, HGRN_WIDTH)
    rest3 = rest.reshape(batch, seq, REST_WIDTH)
    blk = lambda col: pl.BlockSpec((1, tm, HGRN_WIDTH), lambda b, i: (b, i, col))
    out = pl.pallas_call(
        functools.partial(_hgrn_body, n_chunks=tm // HGRN_CHUNK),
        grid=(batch, seq // tm),
        in_specs=[blk(0), blk(0), blk(1), blk(2), _resident((1, HGRN_DV))],
        out_specs=blk(0),
        out_shape=jax.ShapeDtypeStruct((batch, seq, HGRN_WIDTH), BF16),
        scratch_shapes=[pltpu.VMEM((HGRN_HEADS, HGRN_DV, HGRN_DK), F32),
                        pltpu.VMEM((HGRN_CHUNK, HGRN_WIDTH), BF16),
                        pltpu.VMEM((HGRN_CHUNK, HGRN_WIDTH), BF16),
                        pltpu.VMEM((HGRN_HEADS, HGRN_CHUNK, HGRN_CHUNK), BF16),
                        pltpu.VMEM((1, HGRN_WIDTH), F32)],
        compiler_params=_params("arbitrary", "arbitrary"),
        name="hgrn2",
    )(f3, rest3, rest3, rest3, norm_g)
    return out.reshape(batch * seq, HGRN_WIDTH)


MIX_CHUNK = 256


def _mix_out_body(x_ref, yat_ref, yh_ref, ga_ref, gh_ref, wpa_ref, wph_ref, wo_ref, g_ref, b_ref, o_ref, y_ref):
    @pl.when(pl.program_id(0) == 0)
    def _():
        y_ref[...] = jnp.zeros_like(y_ref)

    tm = x_ref.shape[0]
    n_chunks = D_MODEL // MIX_CHUNK
    slab_rows = tm // LN_SLABS
    slabs_per_chunk = LN_SLABS // n_chunks
    yat, yh = yat_ref[...], yh_ref[...]
    merged = []
    for c in range(n_chunks):
        cols = slice(c * MIX_CHUNK, (c + 1) * MIX_CHUNK)
        pa = lax.dot_general(yat, wpa_ref[:, cols], (((0,), (0,)), ((), ())), preferred_element_type=F32)
        ph = jnp.dot(yh, wph_ref[:, cols], preferred_element_type=F32)
        m = ga_ref[:, cols].astype(F32) * pa + gh_ref[:, cols].astype(F32) * ph
        zero = jnp.zeros((F32_SUBLANES, LANES), F32)
        for s in range(c * slabs_per_chunk, (c + 1) * slabs_per_chunk):
            rows = slice(s * slab_rows, (s + 1) * slab_rows)
            ln = _layer_norm(y_ref[rows, :], g_ref[...], b_ref[...])
            o_ref[rows, :] = ln
            zero = zero + _zero_after(ln)
        top = m[0:F32_SUBLANES] + jnp.concatenate([zero] * (MIX_CHUNK // LANES), axis=1)
        merged.append(jnp.concatenate([top, m[F32_SUBLANES:]], axis=0).astype(BF16))
    merged = jnp.concatenate(merged, axis=1)
    for c in range(n_chunks):
        cols = slice(c * MIX_CHUNK, (c + 1) * MIX_CHUNK)
        mix = jnp.dot(merged, wo_ref[:, cols], preferred_element_type=F32)
        y_ref[:, cols] = DEEPNORM_ALPHA * x_ref[:, cols] + mix


def _mix_out(x2d, y_attn_t, y_hgrn, rest, w_pa, w_ph, w_out, g, b, *, tm):
    n = x2d.shape[0]
    tiles = n // tm
    cur = lambda i: jnp.minimum(i, tiles - 1)
    row = pl.BlockSpec((tm, D_MODEL), lambda i: (cur(i), 0))
    gate = lambda col: pl.BlockSpec((tm, D_MODEL), lambda i: (cur(i), col))
    sq = _resident((D_MODEL, D_MODEL))
    return pl.pallas_call(
        _mix_out_body,
        grid=(tiles + 1,),
        in_specs=[row, pl.BlockSpec((ATTN_WIDTH, tm), lambda i: (0, cur(i))), row, gate(3), gate(4), sq, sq, sq,
                  _resident((1, D_MODEL)), _resident((1, D_MODEL))],
        out_specs=pl.BlockSpec((tm, D_MODEL), lambda i: (jnp.maximum(i - 1, 0), 0)),
        out_shape=jax.ShapeDtypeStruct((n, D_MODEL), F32),
        scratch_shapes=[pltpu.VMEM((tm, D_MODEL), F32)],
        compiler_params=_params("arbitrary"),
        name="mix_out",
    )(x2d, y_attn_t, y_hgrn, rest, rest, w_pa, w_ph, w_out, g, b)


def _tile(n, want):
    t = min(n, want)
    assert n % t == 0, (n, t)
    return t


def kernel(x, ln1_g, ln1_b, ffn1_w1, ffn1_w3, ffn1_w2, ln2_g, ln2_b, w_in, b_in, attn_sinks, hgrn_lb_logits,
           hgrn_norm_g, w_proj_attn, w_proj_hgrn, w_out, ln3_g, ln3_b, ffn2_w1, ffn2_w3, ffn2_w2):
    batch, seq, d = x.shape
    assert d == D_MODEL and seq % (ATTN_TILE_BLOCKS * ATTN_BLOCK) == 0
    n = batch * seq
    tm = _tile(seq, 512)
    lb_all = jnp.cumsum(jax.nn.softmax(hgrn_lb_logits.astype(F32), axis=0), axis=0)
    rope = _rope_tables(seq)
    bf = lambda w: w.astype(BF16)
    h = x.reshape(n, d)
    for l in range(DEPTH):
        h = _ffn_ln(h, bf(0.5 * ffn1_w1[l]), bf(ffn1_w3[l]), bf(0.5 * ffn1_w2[l]), ln1_g[l:l + 1], ln1_b[l:l + 1], tm=tm)
        qt, vt, k, f, rest = _in_proj(h, w_in[l], b_in[l], lb_all[l:l + 1], rope, tm=tm, seq=seq)
        y_attn_t = _swa_attn(attn_sinks[l], qt, k, vt, batch=batch, seq=seq)
        y_hgrn = _hgrn2(f, rest, hgrn_norm_g[l:l + 1], batch=batch, seq=seq, tm=tm)
        h = _mix_out(h, y_attn_t, y_hgrn, rest, bf(w_proj_attn[l]), bf(w_proj_hgrn[l]), bf(w_out[l]),
                     ln2_g[l:l + 1], ln2_b[l:l + 1], tm=tm)
        h = _ffn_ln(h, bf(0.5 * ffn2_w1[l]), bf(ffn2_w3[l]), bf(0.5 * ffn2_w2[l]), ln3_g[l:l + 1], ln3_b[l:l + 1], tm=tm)
    return h.reshape(batch, seq, d)
```

```python
import functools

import jax
import jax.numpy as jnp
from jax import lax
from jax.experimental import pallas as pl
from jax.experimental.pallas import tpu as pltpu

D_MODEL = 1024
DEPTH = 1
N_Q_HEADS = 16
N_KV_HEADS = 4
HEAD_DIM = 64
GQA_GROUP = N_Q_HEADS // N_KV_HEADS
WINDOW = 128
ATTN_BLOCK = 128
ROPE_THETA = 500000.0
ROPE_DIM = HEAD_DIM // 4
ROPE_HALF = ROPE_DIM // 2
HGRN_HEADS = 8
HGRN_DK = 128
HGRN_DV = 128
HGRN_CHUNK = 64
D_FF = 2816
ATTN_WIDTH = N_Q_HEADS * HEAD_DIM
KV_WIDTH = N_KV_HEADS * HEAD_DIM
HGRN_WIDTH = HGRN_HEADS * HGRN_DK
D_IN = ATTN_WIDTH + 2 * KV_WIDTH + 4 * HGRN_WIDTH + 2 * D_MODEL
DEEPNORM_ALPHA = (2 * DEPTH) ** 0.25
LN_EPS = 1e-5
RMS_EPS = 1e-6
NEG_INF = -1e30
LOG2_E = 1.4426950408889634

LANES = 128
F32_SUBLANES = 8
VMEM_LIMIT_BYTES = 56 * 1024 * 1024

F32 = jnp.float32
BF16 = jnp.bfloat16

assert WINDOW == ATTN_BLOCK and ROPE_HALF == F32_SUBLANES


def _layer_norm(y, g, b):
    mu = jnp.mean(y, axis=-1, keepdims=True)
    yc = y - mu
    var = jnp.mean(yc * yc, axis=-1, keepdims=True)
    return yc * lax.rsqrt(var + LN_EPS) * g + b


def _resident(shape):
    return pl.BlockSpec(shape, lambda *_: (0,) * len(shape), pipeline_mode=pl.Buffered(1))


def _params(*semantics):
    return pltpu.CompilerParams(dimension_semantics=semantics, vmem_limit_bytes=VMEM_LIMIT_BYTES)


LN_SLABS = 8


def _zero_after(x):
    bits = pltpu.bitcast(x, jnp.uint32)
    tiles = [bits[r:r + F32_SUBLANES, c:c + LANES]
             for r in range(0, x.shape[0], F32_SUBLANES) for c in range(0, x.shape[1], LANES)]
    folded = functools.reduce(jnp.bitwise_or, tiles)
    return pltpu.bitcast((folded >> 16) >> 16, F32)


def _ffn_ln_body(x_ref, w1_ref, w3_ref, w2_ref, g_ref, b_ref, o_ref, acc_ref, y_ref, *, ff_chunk):
    @pl.when(pl.program_id(0) == 0)
    def _():
        y_ref[...] = jnp.zeros_like(y_ref)

    tm = x_ref.shape[0]
    n_slabs, slab_rows = LN_SLABS, tm // LN_SLABS
    n_chunks = D_FF // ff_chunk
    assert n_slabs <= n_chunks - 1
    x = x_ref[...]
    xb = x.astype(BF16)
    for c in range(n_chunks):
        cols = slice(c * ff_chunk, (c + 1) * ff_chunk)
        h1 = jnp.dot(xb, w1_ref[:, cols], preferred_element_type=F32)
        h3 = jnp.dot(xb, w3_ref[:, cols], preferred_element_type=F32)
        a = ((h1 * jnp.tanh(h1) + h1) * h3).astype(BF16)
        part = jnp.dot(a, w2_ref[cols, :], preferred_element_type=F32)
        if c == 0:
            acc_ref[...] = DEEPNORM_ALPHA * x + part
        elif c < n_chunks - 1:
            acc_ref[...] += part
        else:
            y_ref[...] = acc_ref[...] + part
        if c < n_slabs:
            rows = slice(c * slab_rows, (c + 1) * slab_rows)
            ln = _layer_norm(y_ref[rows, :], g_ref[...], b_ref[...])
            o_ref[rows, :] = ln
            acc_ref[0:F32_SUBLANES, 0:LANES] += _zero_after(ln)


def _ffn_ln(x2d, w1, w3, w2, g, b, *, tm, ff_chunk=256):
    n = x2d.shape[0]
    tiles = n // tm
    return pl.pallas_call(
        functools.partial(_ffn_ln_body, ff_chunk=ff_chunk),
        grid=(tiles + 1,),
        in_specs=[pl.BlockSpec((tm, D_MODEL), lambda i: (jnp.minimum(i, tiles - 1), 0)),
                  _resident((D_MODEL, D_FF)), _resident((D_MODEL, D_FF)), _resident((D_FF, D_MODEL)),
                  _resident((1, D_MODEL)), _resident((1, D_MODEL))],
        out_specs=pl.BlockSpec((tm, D_MODEL), lambda i: (jnp.maximum(i - 1, 0), 0)),
        out_shape=jax.ShapeDtypeStruct((n, D_MODEL), F32),
        scratch_shapes=[pltpu.VMEM((tm, D_MODEL), F32), pltpu.VMEM((tm, D_MODEL), F32)],
        compiler_params=_params("arbitrary"),
        name="ffn_ln",
    )(x2d, w1, w3, w2, g, b)


PROJ_CHUNK = 512
QV_ROWS = ATTN_WIDTH + KV_WIDTH
QV_CHUNK = 256
TOK_WIDTH = 4 * HGRN_WIDTH + 2 * D_MODEL + KV_WIDTH


def _rope_lanes(t, c, s_lo, s_hi):
    return t * c + pltpu.roll(t, ROPE_HALF, axis=1) * s_hi + pltpu.roll(t, LANES - ROPE_HALF, axis=1) * s_lo


def _in_proj_body(x_ref, wt_ref, bt_ref, w_ref, b_ref, fa_ref, fb_ref, c_ref, slo_ref, shi_ref, cos_ref, sin_ref, ng_ref,
                  qt_ref, vt_ref, k_ref, gate_ref, yh_ref,
                  f_s, q_s, v_s, og_s, state_ref, qd_ref, ke_ref, sc_ref, dec_ref, *, tiles_per_seq):
    @pl.when(pl.program_id(0) % tiles_per_seq == 0)
    def _():
        state_ref[...] = jnp.zeros_like(state_ref)

    xb = x_ref[...].astype(BF16)
    tm = xb.shape[0]
    q_scale = HEAD_DIM ** -0.5 * LOG2_E

    def feature_major(j):
        cos_t, sin_t = cos_ref[...], sin_ref[...]
        rows = slice(j * QV_CHUNK, (j + 1) * QV_CHUNK)
        pt = lax.dot_general(wt_ref[rows, :], xb, (((1,), (1,)), ((), ())), preferred_element_type=F32)
        pt = pt + bt_ref[rows, :]
        if j * QV_CHUNK < ATTN_WIDTH:
            for h in range(QV_CHUNK // HEAD_DIM):
                r0 = h * HEAD_DIM
                t1 = pt[r0:r0 + ROPE_HALF]
                t2 = pt[r0 + ROPE_HALF:r0 + ROPE_DIM]
                rot = jnp.concatenate([t1 * cos_t - t2 * sin_t, t2 * cos_t + t1 * sin_t,
                                       pt[r0 + ROPE_DIM:r0 + HEAD_DIM]], axis=0)
                qt_ref[j * QV_CHUNK + r0:j * QV_CHUNK + r0 + HEAD_DIM, :] = (rot * q_scale).astype(BF16)
        else:
            vt_ref[...] = pt.astype(BF16)

    def token_major(lo):
        width = min(PROJ_CHUNK, TOK_WIDTH - lo)
        cols = slice(lo, lo + width)
        p = jnp.dot(xb, w_ref[:, cols], preferred_element_type=F32) + b_ref[:, cols]
        kind, off = _tok_kind(lo)
        dst = slice(off, off + width)
        if kind == "forget":
            f_s[:, dst] = fa_ref[:, dst] + fb_ref[:, dst] * jnp.tanh(p)
        elif kind == "hgrn_q":
            q_s[:, dst] = (p * jnp.tanh(p) + p).astype(BF16)
        elif kind == "hgrn_v":
            v_s[:, dst] = p.astype(BF16)
        elif kind == "hgrn_og":
            og_s[:, dst] = (p * jnp.tanh(p) + p).astype(BF16)
        elif kind in ("gate_a", "gate_h"):
            base = 0 if kind == "gate_a" else D_MODEL
            gate_ref[:, base + off:base + off + width] = (0.5 * jnp.tanh(p) + 0.5).astype(BF16)
        else:
            c, s_lo, s_hi = c_ref[...], slo_ref[...], shi_ref[...]
            for u in range(width // LANES):
                t = p[:, u * LANES:(u + 1) * LANES]
                k_ref[:, u * LANES:(u + 1) * LANES] = _rope_lanes(t, c, s_lo, s_hi).astype(BF16)

    hgrn_cols = 4 * HGRN_WIDTH
    for lo in range(0, hgrn_cols, PROJ_CHUNK):
        token_major(lo)
    rest = [functools.partial(token_major, lo) for lo in range(hgrn_cols, TOK_WIDTH, PROJ_CHUNK)]
    rest += [functools.partial(feature_major, j) for j in range(QV_ROWS // QV_CHUNK)]

    cz = HGRN_CHUNK
    n_chunks = tm // cz
    ti = lax.broadcasted_iota(jnp.int32, (cz, cz), 0)
    si = lax.broadcasted_iota(jnp.int32, (cz, cz), 1)
    causal = ti >= si
    ng = ng_ref[...]
    nt_dims = (((1,), (1,)), ((), ()))

    def gates(ci):
        rows = slice(ci * cz, (ci + 1) * cz)
        for h in range(HGRN_HEADS):
            lanes = slice(h * HGRN_DK, (h + 1) * HGRN_DK)
            f = f_s[rows, lanes]
            k = 1.0 - f
            decay = _cumprod_rows(f)
            inv = 1.0 / decay
            d_last = decay[cz - 1:cz, :]
            q_dec = (q_s[rows, lanes].astype(F32) * decay).astype(BF16)
            k_inv = (k * inv).astype(BF16)
            sc = lax.dot_general(q_dec, k_inv, nt_dims, preferred_element_type=F32)
            sc_ref[h] = jnp.where(causal, sc, 0.0).astype(BF16)
            qd_ref[:, lanes] = q_dec
            ke_ref[:, lanes] = (k * (d_last * inv)).astype(BF16)
            dec_ref[:, lanes] = d_last

    def outputs(ci):
        rows = slice(ci * cz, (ci + 1) * cz)
        for h in range(HGRN_HEADS):
            lanes = slice(h * HGRN_DK, (h + 1) * HGRN_DK)
            v = v_s[rows, lanes]
            st = state_ref[h]
            o = jnp.dot(sc_ref[h], v, preferred_element_type=F32)
            o += lax.dot_general(qd_ref[:, lanes], st.astype(BF16), nt_dims, preferred_element_type=F32)
            upd = lax.dot_general(v, ke_ref[:, lanes], (((0,), (0,)), ((), ())), preferred_element_type=F32)
            state_ref[h] = st * dec_ref[:, lanes] + upd
            o = o * lax.rsqrt(jnp.mean(o * o, axis=-1, keepdims=True) + RMS_EPS) * ng
            yh_ref[rows, lanes] = (o * og_s[rows, lanes].astype(F32)).astype(BF16)

    per_chunk = -(-len(rest) // n_chunks)
    gates(0)
    for ci in range(n_chunks):
        for piece in rest[ci * per_chunk:(ci + 1) * per_chunk]:
            piece()
        outputs(ci)
        if ci + 1 < n_chunks:
            gates(ci + 1)
    for piece in rest[n_chunks * per_chunk:]:
        piece()


_TOK_GROUPS = (("forget", HGRN_WIDTH), ("hgrn_q", HGRN_WIDTH), ("hgrn_v", HGRN_WIDTH), ("hgrn_og", HGRN_WIDTH),
               ("gate_a", D_MODEL), ("gate_h", D_MODEL), ("rope_k", KV_WIDTH))
_TANH_KINDS = ("forget", "hgrn_q", "hgrn_og", "gate_a", "gate_h")


def _tok_kind(col):
    for kind, width in _TOK_GROUPS:
        if col < width:
            return kind, col
        col -= width
    raise ValueError(col)


def _tok_half_scale():
    return jnp.concatenate([jnp.full((w,), 0.5 if kind in _TANH_KINDS else 1.0, F32)
                            for kind, w in _TOK_GROUPS])[None, :]


def _in_proj(x2d, w_in, b_in, lb, norm_g, rope, *, tm, seq):
    n = x2d.shape[0]
    tiles_per_seq = seq // tm
    q_end, k_end, v_end = ATTN_WIDTH, ATTN_WIDTH + KV_WIDTH, ATTN_WIDTH + 2 * KV_WIDTH
    w_t = jnp.concatenate([w_in[:, :q_end], w_in[:, k_end:v_end]], axis=1).T.astype(BF16)
    b_t = jnp.concatenate([b_in[:q_end], b_in[k_end:v_end]])[:, None]
    half = _tok_half_scale()
    w_tok = (jnp.concatenate([w_in[:, v_end:], w_in[:, q_end:k_end]], axis=1) * half).astype(BF16)
    b_tok = jnp.concatenate([b_in[v_end:], b_in[q_end:k_end]])[None, :] * half
    f_a = lb + 0.5 * (1.0 - lb)
    f_b = 0.5 * (1.0 - lb)
    c, s_lo, s_hi, cos_t, sin_t = rope
    row = lambda w: pl.BlockSpec((tm, w), lambda i: (i, 0))
    col = lambda r: pl.BlockSpec((r, tm), lambda i: (0, i))
    tab = pl.BlockSpec((tm, LANES), lambda i: (i % tiles_per_seq, 0))
    tab_t = pl.BlockSpec((ROPE_HALF, tm), lambda i: (0, i % tiles_per_seq))
    return pl.pallas_call(
        functools.partial(_in_proj_body, tiles_per_seq=tiles_per_seq),
        grid=(n // tm,),
        in_specs=[row(D_MODEL), _resident((QV_ROWS, D_MODEL)), _resident((QV_ROWS, 1)),
                  _resident((D_MODEL, TOK_WIDTH)), _resident((1, TOK_WIDTH)),
                  _resident((1, HGRN_WIDTH)), _resident((1, HGRN_WIDTH)), tab, tab, tab, tab_t, tab_t,
                  _resident((1, HGRN_DV))],
        out_specs=[col(ATTN_WIDTH), col(KV_WIDTH), row(KV_WIDTH), row(2 * D_MODEL), row(HGRN_WIDTH)],
        out_shape=[jax.ShapeDtypeStruct((ATTN_WIDTH, n), BF16),
                   jax.ShapeDtypeStruct((KV_WIDTH, n), BF16),
                   jax.ShapeDtypeStruct((n, KV_WIDTH), BF16),
                   jax.ShapeDtypeStruct((n, 2 * D_MODEL), BF16),
                   jax.ShapeDtypeStruct((n, HGRN_WIDTH), BF16)],
        scratch_shapes=[pltpu.VMEM((tm, HGRN_WIDTH), F32),
                        pltpu.VMEM((tm, HGRN_WIDTH), BF16),
                        pltpu.VMEM((tm, HGRN_WIDTH), BF16),
                        pltpu.VMEM((tm, HGRN_WIDTH), BF16),
                        pltpu.VMEM((HGRN_HEADS, HGRN_DV, HGRN_DK), F32),
                        pltpu.VMEM((HGRN_CHUNK, HGRN_WIDTH), BF16),
                        pltpu.VMEM((HGRN_CHUNK, HGRN_WIDTH), BF16),
                        pltpu.VMEM((HGRN_HEADS, HGRN_CHUNK, HGRN_CHUNK), BF16),
                        pltpu.VMEM((1, HGRN_WIDTH), F32)],
        compiler_params=_params("arbitrary"),
        name="in_proj",
    )(x2d, w_t, b_t, w_tok, b_tok, f_a, f_b, c, s_lo, s_hi, cos_t, sin_t, norm_g)


def _rope_tables(seq):
    pos = jnp.arange(seq, dtype=F32)
    inv_freq = ROPE_THETA ** (-jnp.arange(0, ROPE_DIM, 2, dtype=F32) / ROPE_DIM)
    ang = pos[:, None] * inv_freq[None, :]
    cos, sin = jnp.cos(ang), jnp.sin(ang)
    ones = jnp.ones((seq, HEAD_DIM - ROPE_DIM), F32)
    c = jnp.concatenate([cos, cos, ones], axis=1)
    s_lo = jnp.concatenate([-sin, jnp.zeros((seq, HEAD_DIM - ROPE_HALF), F32)], axis=1)
    s_hi = jnp.concatenate([jnp.zeros((seq, ROPE_HALF), F32), sin, 0.0 * ones], axis=1)
    rep = LANES // HEAD_DIM
    return tuple(jnp.tile(t, (1, rep)) for t in (c, s_lo, s_hi)) + (cos.T, sin.T)


ATTN_TILE_BLOCKS = 4


def _swa_body(sink_ref, qt_ref, k_ref, kh_ref, vt_ref, vth_ref, o_ref):
    first_tile = pl.program_id(1) == 0
    halo_bias = jnp.where(first_tile, NEG_INF, 0.0).astype(F32)
    wide = GQA_GROUP * ATTN_BLOCK
    key_i = lax.broadcasted_iota(jnp.int32, (ATTN_BLOCK, wide), 0)
    qry_i = lax.broadcasted_iota(jnp.int32, (ATTN_BLOCK, wide), 1) % ATTN_BLOCK
    from_prev = key_i > qry_i
    keep_prev = from_prev.astype(BF16)
    keep_cur = 1.0 - keep_prev
    zero_half = jnp.zeros((HEAD_DIM, ATTN_BLOCK), BF16)
    units = [(qb, kh) for qb in range(ATTN_TILE_BLOCKS) for kh in range(N_KV_HEADS)]

    def heads_of(kh):
        return [kh * GQA_GROUP + g for g in range(GQA_GROUP)]

    def toks_of(qb):
        return slice(qb * ATTN_BLOCK, (qb + 1) * ATTN_BLOCK)

    def scores(qb, kh):
        toks = toks_of(qb)
        k_prev = kh_ref[...] if qb == 0 else k_ref[(qb - 1) * ATTN_BLOCK:qb * ATTN_BLOCK, :]
        keys = jnp.concatenate([k_prev, k_ref[toks, :]], axis=0)
        pair = slice((kh // 2) * LANES, (kh // 2 + 1) * LANES)
        cols = []
        for h in heads_of(kh):
            qh = qt_ref[h * HEAD_DIM:(h + 1) * HEAD_DIM, toks]
            cols.append(jnp.concatenate([qh, zero_half] if kh % 2 == 0 else [zero_half, qh], axis=0))
        q_rhs = jnp.concatenate(cols, axis=1)
        return jnp.dot(keys[:, pair], q_rhs, preferred_element_type=F32)

    def finish(qb, kh, s):
        toks = toks_of(qb)
        heads = heads_of(kh)
        s_prev = s[:ATTN_BLOCK]
        if qb == 0:
            s_prev = s_prev + halo_bias
        c = jnp.where(from_prev, s_prev, s[ATTN_BLOCK:])
        sink = jnp.concatenate([jnp.full((1, ATTN_BLOCK), sink_ref[h] * LOG2_E, F32) for h in heads], axis=1)
        m = jnp.maximum(jnp.max(c, axis=0, keepdims=True), sink)
        p = jnp.exp2(c - m)
        denom = jnp.sum(p, axis=0, keepdims=True) + jnp.exp2(sink - m)
        pb = p.astype(BF16)
        p_both = jnp.concatenate([pb * keep_prev, pb * keep_cur], axis=0)
        vt_prev = vth_ref[...] if qb == 0 else vt_ref[:, (qb - 1) * ATTN_BLOCK:qb * ATTN_BLOCK]
        vt_both = jnp.concatenate([vt_prev[kh * HEAD_DIM:(kh + 1) * HEAD_DIM, :],
                                   vt_ref[kh * HEAD_DIM:(kh + 1) * HEAD_DIM, toks]], axis=1)
        ot = jnp.dot(vt_both, p_both, preferred_element_type=F32) * (1.0 / denom)
        for g, h in enumerate(heads):
            o_ref[h * HEAD_DIM:(h + 1) * HEAD_DIM, toks] = ot[:, g * ATTN_BLOCK:(g + 1) * ATTN_BLOCK].astype(BF16)

    s_next = scores(*units[0])
    for i, unit in enumerate(units):
        s_cur = s_next
        if i + 1 < len(units):
            s_next = scores(*units[i + 1])
        finish(*unit, s_cur)


def _swa_attn(sinks, qt, k, vt, *, batch, seq):
    tq = ATTN_TILE_BLOCKS * ATTN_BLOCK
    tiles = seq // tq
    main = lambda b, i: b * tiles + i
    halo = lambda b, i: jnp.maximum((b * tiles + i) * ATTN_TILE_BLOCKS - 1, 0)
    return pl.pallas_call(
        _swa_body,
        grid=(batch, tiles),
        in_specs=[pl.BlockSpec(memory_space=pltpu.SMEM),
                  pl.BlockSpec((ATTN_WIDTH, tq), lambda b, i: (0, main(b, i))),
                  pl.BlockSpec((tq, KV_WIDTH), lambda b, i: (main(b, i), 0)),
                  pl.BlockSpec((ATTN_BLOCK, KV_WIDTH), lambda b, i: (halo(b, i), 0)),
                  pl.BlockSpec((KV_WIDTH, tq), lambda b, i: (0, main(b, i))),
                  pl.BlockSpec((KV_WIDTH, ATTN_BLOCK), lambda b, i: (0, halo(b, i)))],
        out_specs=pl.BlockSpec((ATTN_WIDTH, tq), lambda b, i: (0, main(b, i))),
        out_shape=jax.ShapeDtypeStruct((ATTN_WIDTH, batch * seq), BF16),
        compiler_params=_params("arbitrary", "arbitrary"),
        name="swa_attn",
    )(sinks, qt, k, k, vt, vt)


def _cumprod_rows(f):
    row = lax.broadcasted_iota(jnp.int32, (F32_SUBLANES, LANES), 0)
    blocks, carry = [], None
    for j in range(f.shape[0] // F32_SUBLANES):
        x = f[j * F32_SUBLANES:(j + 1) * F32_SUBLANES]
        shift = 1
        while shift < F32_SUBLANES:
            x = x * jnp.where(row >= shift, pltpu.roll(x, shift, axis=0), 1.0)
            shift *= 2
        if carry is not None:
            x = x * carry
        carry = x[F32_SUBLANES - 1:F32_SUBLANES, :]
        blocks.append(x)
    return jnp.concatenate(blocks, axis=0)


MIX_CHUNK = 256


def _mix_out_body(x_ref, yat_ref, yh_ref, ga_ref, gh_ref, wpa_ref, wph_ref, wo_ref, g_ref, b_ref, o_ref, y_ref):
    @pl.when(pl.program_id(0) == 0)
    def _():
        y_ref[...] = jnp.zeros_like(y_ref)

    tm = x_ref.shape[0]
    n_chunks = D_MODEL // MIX_CHUNK
    slab_rows = tm // LN_SLABS
    slabs_per_chunk = LN_SLABS // n_chunks
    yat, yh = yat_ref[...], yh_ref[...]
    merged = []
    for c in range(n_chunks):
        cols = slice(c * MIX_CHUNK, (c + 1) * MIX_CHUNK)
        pa = lax.dot_general(yat, wpa_ref[:, cols], (((0,), (0,)), ((), ())), preferred_element_type=F32)
        ph = jnp.dot(yh, wph_ref[:, cols], preferred_element_type=F32)
        m = ga_ref[:, cols].astype(F32) * pa + gh_ref[:, cols].astype(F32) * ph
        zero = jnp.zeros((F32_SUBLANES, LANES), F32)
        for s in range(c * slabs_per_chunk, (c + 1) * slabs_per_chunk):
            rows = slice(s * slab_rows, (s + 1) * slab_rows)
            ln = _layer_norm(y_ref[rows, :], g_ref[...], b_ref[...])
            o_ref[rows, :] = ln
            zero = zero + _zero_after(ln)
        top = m[0:F32_SUBLANES] + jnp.concatenate([zero] * (MIX_CHUNK // LANES), axis=1)
        merged.append(jnp.concatenate([top, m[F32_SUBLANES:]], axis=0).astype(BF16))
    merged = jnp.concatenate(merged, axis=1)
    for c in range(n_chunks):
        cols = slice(c * MIX_CHUNK, (c + 1) * MIX_CHUNK)
        mix = jnp.dot(merged, wo_ref[:, cols], preferred_element_type=F32)
        y_ref[:, cols] = DEEPNORM_ALPHA * x_ref[:, cols] + mix


def _mix_out(x2d, y_attn_t, y_hgrn, gates, w_pa, w_ph, w_out, g, b, *, tm):
    n = x2d.shape[0]
    tiles = n // tm
    cur = lambda i: jnp.minimum(i, tiles - 1)
    row = pl.BlockSpec((tm, D_MODEL), lambda i: (cur(i), 0))
    gate = lambda col: pl.BlockSpec((tm, D_MODEL), lambda i: (cur(i), col))
    sq = _resident((D_MODEL, D_MODEL))
    return pl.pallas_call(
        _mix_out_body,
        grid=(tiles + 1,),
        in_specs=[row, pl.BlockSpec((ATTN_WIDTH, tm), lambda i: (0, cur(i))), row, gate(0), gate(1), sq, sq, sq,
                  _resident((1, D_MODEL)), _resident((1, D_MODEL))],
        out_specs=pl.BlockSpec((tm, D_MODEL), lambda i: (jnp.maximum(i - 1, 0), 0)),
        out_shape=jax.ShapeDtypeStruct((n, D_MODEL), F32),
        scratch_shapes=[pltpu.VMEM((tm, D_MODEL), F32)],
        compiler_params=_params("arbitrary"),
        name="mix_out",
    )(x2d, y_attn_t, y_hgrn, gates, gates, w_pa, w_ph, w_out, g, b)


def _tile(n, want):
    t = min(n, want)
    assert n % t == 0, (n, t)
    return t


def kernel(x, ln1_g, ln1_b, ffn1_w1, ffn1_w3, ffn1_w2, ln2_g, ln2_b, w_in, b_in, attn_sinks, hgrn_lb_logits,
           hgrn_norm_g, w_proj_attn, w_proj_hgrn, w_out, ln3_g, ln3_b, ffn2_w1, ffn2_w3, ffn2_w2):
    batch, seq, d = x.shape
    assert d == D_MODEL and seq % (ATTN_TILE_BLOCKS * ATTN_BLOCK) == 0
    n = batch * seq
    tm = _tile(seq, 512)
    lb_all = jnp.cumsum(jax.nn.softmax(hgrn_lb_logits.astype(F32), axis=0), axis=0)
    rope = _rope_tables(seq)
    bf = lambda w: w.astype(BF16)
    h = x.reshape(n, d)
    for l in range(DEPTH):
        h = _ffn_ln(h, bf(0.5 * ffn1_w1[l]), bf(ffn1_w3[l]), bf(0.5 * ffn1_w2[l]), ln1_g[l:l + 1], ln1_b[l:l + 1], tm=tm)
        qt, vt, k, gates, y_hgrn = _in_proj(h, w_in[l], b_in[l], lb_all[l:l + 1], hgrn_norm_g[l:l + 1], rope,
                                            tm=tm, seq=seq)
        y_attn_t = _swa_attn(attn_sinks[l], qt, k, vt, batch=batch, seq=seq)
        h = _mix_out(h, y_attn_t, y_hgrn, gates, bf(w_proj_attn[l]), bf(w_proj_hgrn[l]), bf(w_out[l]),
                     ln2_g[l:l + 1], ln2_b[l:l + 1], tm=tm)
        h = _ffn_ln(h, bf(0.5 * ffn2_w1[l]), bf(ffn2_w3[l]), bf(0.5 * ffn2_w2[l]), ln3_g[l:l + 1], ln3_b[l:l + 1], tm=tm)
    return h.reshape(batch, seq, d)
```

```python
import functools

import jax
import jax.numpy as jnp
from jax import lax
from jax.experimental import pallas as pl
from jax.experimental.pallas import tpu as pltpu

D_MODEL = 1024
DEPTH = 1
N_Q_HEADS = 16
N_KV_HEADS = 4
HEAD_DIM = 64
GQA_GROUP = N_Q_HEADS // N_KV_HEADS
WINDOW = 128
ATTN_BLOCK = 128
ROPE_THETA = 500000.0
ROPE_DIM = HEAD_DIM // 4
ROPE_HALF = ROPE_DIM // 2
HGRN_HEADS = 8
HGRN_DK = 128
HGRN_DV = 128
HGRN_CHUNK = 64
D_FF = 2816
ATTN_WIDTH = N_Q_HEADS * HEAD_DIM
KV_WIDTH = N_KV_HEADS * HEAD_DIM
HGRN_WIDTH = HGRN_HEADS * HGRN_DK
D_IN = ATTN_WIDTH + 2 * KV_WIDTH + 4 * HGRN_WIDTH + 2 * D_MODEL
DEEPNORM_ALPHA = (2 * DEPTH) ** 0.25
LN_EPS = 1e-5
RMS_EPS = 1e-6
NEG_INF = -1e30
LOG2_E = 1.4426950408889634

LANES = 128
F32_SUBLANES = 8
VMEM_LIMIT_BYTES = 56 * 1024 * 1024

F32 = jnp.float32
BF16 = jnp.bfloat16

assert WINDOW == ATTN_BLOCK and ROPE_HALF == F32_SUBLANES


def _layer_norm(y, g, b):
    mu = jnp.mean(y, axis=-1, keepdims=True)
    yc = y - mu
    var = jnp.mean(yc * yc, axis=-1, keepdims=True)
    return yc * lax.rsqrt(var + LN_EPS) * g + b


def _resident(shape):
    return pl.BlockSpec(shape, lambda *_: (0,) * len(shape), pipeline_mode=pl.Buffered(1))


def _params(*semantics):
    return pltpu.CompilerParams(dimension_semantics=semantics, vmem_limit_bytes=VMEM_LIMIT_BYTES)


LN_SLABS = 8


def _zero_after(x):
    bits = pltpu.bitcast(x, jnp.uint32)
    tiles = [bits[r:r + F32_SUBLANES, c:c + LANES]
             for r in range(0, x.shape[0], F32_SUBLANES) for c in range(0, x.shape[1], LANES)]
    folded = functools.reduce(jnp.bitwise_or, tiles)
    return pltpu.bitcast((folded >> 16) >> 16, F32)


def _ffn_ln_body(x_ref, w1_ref, w3_ref, w2_ref, g_ref, b_ref, o_ref, acc_ref, y_ref, *, ff_chunk):
    @pl.when(pl.program_id(0) == 0)
    def _():
        y_ref[...] = jnp.zeros_like(y_ref)

    tm = x_ref.shape[0]
    n_slabs, slab_rows = LN_SLABS, tm // LN_SLABS
    n_chunks = D_FF // ff_chunk
    assert n_slabs <= n_chunks - 1
    x = x_ref[...]
    xb = x.astype(BF16)
    for c in range(n_chunks):
        cols = slice(c * ff_chunk, (c + 1) * ff_chunk)
        h1 = jnp.dot(xb, w1_ref[:, cols], preferred_element_type=F32)
        h3 = jnp.dot(xb, w3_ref[:, cols], preferred_element_type=F32)
        a = ((h1 * jnp.tanh(h1) + h1) * h3).astype(BF16)
        part = jnp.dot(a, w2_ref[cols, :], preferred_element_type=F32)
        if c == 0:
            acc_ref[...] = DEEPNORM_ALPHA * x + part
        elif c < n_chunks - 1:
            acc_ref[...] += part
        else:
            y_ref[...] = acc_ref[...] + part
        if c < n_slabs:
            rows = slice(c * slab_rows, (c + 1) * slab_rows)
            ln = _layer_norm(y_ref[rows, :], g_ref[...], b_ref[...])
            o_ref[rows, :] = ln
            acc_ref[0:F32_SUBLANES, 0:LANES] += _zero_after(ln)


def _ffn_ln(x2d, w1, w3, w2, g, b, *, tm, ff_chunk=256):
    n = x2d.shape[0]
    tiles = n // tm
    return pl.pallas_call(
        functools.partial(_ffn_ln_body, ff_chunk=ff_chunk),
        grid=(tiles + 1,),
        in_specs=[pl.BlockSpec((tm, D_MODEL), lambda i: (jnp.minimum(i, tiles - 1), 0)),
                  _resident((D_MODEL, D_FF)), _resident((D_MODEL, D_FF)), _resident((D_FF, D_MODEL)),
                  _resident((1, D_MODEL)), _resident((1, D_MODEL))],
        out_specs=pl.BlockSpec((tm, D_MODEL), lambda i: (jnp.maximum(i - 1, 0), 0)),
        out_shape=jax.ShapeDtypeStruct((n, D_MODEL), F32),
        scratch_shapes=[pltpu.VMEM((tm, D_MODEL), F32), pltpu.VMEM((tm, D_MODEL), F32)],
        compiler_params=_params("arbitrary"),
        name="ffn_ln",
    )(x2d, w1, w3, w2, g, b)


PROJ_CHUNK = 512
QV_ROWS = ATTN_WIDTH + KV_WIDTH
QV_CHUNK = 256
TOK_WIDTH = 4 * HGRN_WIDTH + 2 * D_MODEL + KV_WIDTH


def _rope_lanes(t, c, s_lo, s_hi):
    return t * c + pltpu.roll(t, ROPE_HALF, axis=1) * s_hi + pltpu.roll(t, LANES - ROPE_HALF, axis=1) * s_lo


def _in_proj_body(x_ref, wt_ref, bt_ref, w_ref, b_ref, fa_ref, fb_ref, c_ref, slo_ref, shi_ref, cos_ref, sin_ref, ng_ref,
                  qt_ref, vt_ref, k_ref, gate_ref, yh_ref,
                  f_s, q_s, v_s, og_s, state_ref, qd_ref, ke_ref, sc_ref, dec_ref, *, tiles_per_seq):
    @pl.when(pl.program_id(0) % tiles_per_seq == 0)
    def _():
        state_ref[...] = jnp.zeros_like(state_ref)

    xb = x_ref[...].astype(BF16)
    tm = xb.shape[0]
    q_scale = HEAD_DIM ** -0.5 * LOG2_E

    def feature_major(j):
        cos_t, sin_t = cos_ref[...], sin_ref[...]
        rows = slice(j * QV_CHUNK, (j + 1) * QV_CHUNK)
        pt = lax.dot_general(wt_ref[rows, :], xb, (((1,), (1,)), ((), ())), preferred_element_type=F32)
        pt = pt + bt_ref[rows, :]
        if j * QV_CHUNK < ATTN_WIDTH:
            for h in range(QV_CHUNK // HEAD_DIM):
                r0 = h * HEAD_DIM
                t1 = pt[r0:r0 + ROPE_HALF]
                t2 = pt[r0 + ROPE_HALF:r0 + ROPE_DIM]
                rot = jnp.concatenate([t1 * cos_t - t2 * sin_t, t2 * cos_t + t1 * sin_t,
                                       pt[r0 + ROPE_DIM:r0 + HEAD_DIM]], axis=0)
                qt_ref[j * QV_CHUNK + r0:j * QV_CHUNK + r0 + HEAD_DIM, :] = (rot * q_scale).astype(BF16)
        else:
            vt_ref[...] = pt.astype(BF16)

    def token_major(lo):
        width = min(PROJ_CHUNK, TOK_WIDTH - lo)
        cols = slice(lo, lo + width)
        p = jnp.dot(xb, w_ref[:, cols], preferred_element_type=F32) + b_ref[:, cols]
        kind, off = _tok_kind(lo)
        dst = slice(off, off + width)
        if kind == "forget":
            f_s[:, dst] = fa_ref[:, dst] + fb_ref[:, dst] * jnp.tanh(p)
        elif kind == "hgrn_q":
            q_s[:, dst] = (p * jnp.tanh(p) + p).astype(BF16)
        elif kind == "hgrn_v":
            v_s[:, dst] = p.astype(BF16)
        elif kind == "hgrn_og":
            og_s[:, dst] = (p * jnp.tanh(p) + p).astype(BF16)
        elif kind in ("gate_a", "gate_h"):
            base = 0 if kind == "gate_a" else D_MODEL
            gate_ref[:, base + off:base + off + width] = (0.5 * jnp.tanh(p) + 0.5).astype(BF16)
        else:
            c, s_lo, s_hi = c_ref[...], slo_ref[...], shi_ref[...]
            for u in range(width // LANES):
                t = p[:, u * LANES:(u + 1) * LANES]
                k_ref[:, u * LANES:(u + 1) * LANES] = _rope_lanes(t, c, s_lo, s_hi).astype(BF16)

    hgrn_cols = 4 * HGRN_WIDTH
    for lo in range(0, hgrn_cols, PROJ_CHUNK):
        token_major(lo)
    rest = [functools.partial(token_major, lo) for lo in range(hgrn_cols, TOK_WIDTH, PROJ_CHUNK)]
    rest += [functools.partial(feature_major, j) for j in range(QV_ROWS // QV_CHUNK)]

    cz = HGRN_CHUNK
    n_chunks = tm // cz
    ti = lax.broadcasted_iota(jnp.int32, (cz, cz), 0)
    si = lax.broadcasted_iota(jnp.int32, (cz, cz), 1)
    causal = ti >= si
    ng = ng_ref[...]
    nt_dims = (((1,), (1,)), ((), ()))

    def gates(ci):
        rows = slice(ci * cz, (ci + 1) * cz)
        for h in range(HGRN_HEADS):
            lanes = slice(h * HGRN_DK, (h + 1) * HGRN_DK)
            f = f_s[rows, lanes]
            k = 1.0 - f
            decay = _cumprod_rows(f)
            inv = 1.0 / decay
            d_last = decay[cz - 1:cz, :]
            q_dec = (q_s[rows, lanes].astype(F32) * decay).astype(BF16)
            k_inv = (k * inv).astype(BF16)
            sc = lax.dot_general(q_dec, k_inv, nt_dims, preferred_element_type=F32)
            sc_ref[h] = jnp.where(causal, sc, 0.0).astype(BF16)
            qd_ref[:, lanes] = q_dec
            ke_ref[:, lanes] = (k * (d_last * inv)).astype(BF16)
            dec_ref[:, lanes] = d_last

    def outputs(ci):
        rows = slice(ci * cz, (ci + 1) * cz)
        for h in range(HGRN_HEADS):
            lanes = slice(h * HGRN_DK, (h + 1) * HGRN_DK)
            v = v_s[rows, lanes]
            st = state_ref[h]
            o = jnp.dot(sc_ref[h], v, preferred_element_type=F32)
            o += lax.dot_general(qd_ref[:, lanes], st.astype(BF16), nt_dims, preferred_element_type=F32)
            upd = lax.dot_general(v, ke_ref[:, lanes], (((0,), (0,)), ((), ())), preferred_element_type=F32)
            state_ref[h] = st * dec_ref[:, lanes] + upd
            o = o * lax.rsqrt(jnp.mean(o * o, axis=-1, keepdims=True) + RMS_EPS) * ng
            yh_ref[rows, lanes] = (o * og_s[rows, lanes].astype(F32)).astype(BF16)

    per_chunk = -(-len(rest) // n_chunks)
    gates(0)
    for ci in range(n_chunks):
        for piece in rest[ci * per_chunk:(ci + 1) * per_chunk]:
            piece()
        outputs(ci)
        if ci + 1 < n_chunks:
            gates(ci + 1)
    for piece in rest[n_chunks * per_chunk:]:
        piece()


_TOK_GROUPS = (("forget", HGRN_WIDTH), ("hgrn_q", HGRN_WIDTH), ("hgrn_v", HGRN_WIDTH), ("hgrn_og", HGRN_WIDTH),
               ("gate_a", D_MODEL), ("gate_h", D_MODEL), ("rope_k", KV_WIDTH))
_TANH_KINDS = ("forget", "hgrn_q", "hgrn_og", "gate_a", "gate_h")


def _tok_kind(col):
    for kind, width in _TOK_GROUPS:
        if col < width:
            return kind, col
        col -= width
    raise ValueError(col)


def _tok_half_scale():
    return jnp.concatenate([jnp.full((w,), 0.5 if kind in _TANH_KINDS else 1.0, F32)
                            for kind, w in _TOK_GROUPS])[None, :]


def _in_proj(x2d, w_in, b_in, lb, norm_g, rope, *, tm, seq):
    n = x2d.shape[0]
    tiles_per_seq = seq // tm
    q_end, k_end, v_end = ATTN_WIDTH, ATTN_WIDTH + KV_WIDTH, ATTN_WIDTH + 2 * KV_WIDTH
    w_t = jnp.concatenate([w_in[:, :q_end], w_in[:, k_end:v_end]], axis=1).T.astype(BF16)
    b_t = jnp.concatenate([b_in[:q_end], b_in[k_end:v_end]])[:, None]
    half = _tok_half_scale()
    w_tok = (jnp.concatenate([w_in[:, v_end:], w_in[:, q_end:k_end]], axis=1) * half).astype(BF16)
    b_tok = jnp.concatenate([b_in[v_end:], b_in[q_end:k_end]])[None, :] * half
    f_a = lb + 0.5 * (1.0 - lb)
    f_b = 0.5 * (1.0 - lb)
    c, s_lo, s_hi, cos_t, sin_t = rope
    row = lambda w: pl.BlockSpec((tm, w), lambda i: (i, 0))
    col = lambda r: pl.BlockSpec((r, tm), lambda i: (0, i))
    tab = pl.BlockSpec((tm, LANES), lambda i: (i % tiles_per_seq, 0))
    tab_t = pl.BlockSpec((ROPE_HALF, tm), lambda i: (0, i % tiles_per_seq))
    return pl.pallas_call(
        functools.partial(_in_proj_body, tiles_per_seq=tiles_per_seq),
        grid=(n // tm,),
        in_specs=[row(D_MODEL), _resident((QV_ROWS, D_MODEL)), _resident((QV_ROWS, 1)),
                  _resident((D_MODEL, TOK_WIDTH)), _resident((1, TOK_WIDTH)),
                  _resident((1, HGRN_WIDTH)), _resident((1, HGRN_WIDTH)), tab, tab, tab, tab_t, tab_t,
                  _resident((1, HGRN_DV))],
        out_specs=[col(ATTN_WIDTH), col(KV_WIDTH), row(KV_WIDTH), row(2 * D_MODEL), row(HGRN_WIDTH)],
        out_shape=[jax.ShapeDtypeStruct((ATTN_WIDTH, n), BF16),
                   jax.ShapeDtypeStruct((KV_WIDTH, n), BF16),
                   jax.ShapeDtypeStruct((n, KV_WIDTH), BF16),
                   jax.ShapeDtypeStruct((n, 2 * D_MODEL), BF16),
                   jax.ShapeDtypeStruct((n, HGRN_WIDTH), BF16)],
        scratch_shapes=[pltpu.VMEM((tm, HGRN_WIDTH), F32),
                        pltpu.VMEM((tm, HGRN_WIDTH), BF16),
                        pltpu.VMEM((tm, HGRN_WIDTH), BF16),
                        pltpu.VMEM((tm, HGRN_WIDTH), BF16),
                        pltpu.VMEM((HGRN_HEADS, HGRN_DV, HGRN_DK), F32),
                        pltpu.VMEM((HGRN_CHUNK, HGRN_WIDTH), BF16),
                        pltpu.VMEM((HGRN_CHUNK, HGRN_WIDTH), BF16),
                        pltpu.VMEM((HGRN_HEADS, HGRN_CHUNK, HGRN_CHUNK), BF16),
                        pltpu.VMEM((1, HGRN_WIDTH), F32)],
        compiler_params=_params("arbitrary"),
        name="in_proj",
    )(x2d, w_t, b_t, w_tok, b_tok, f_a, f_b, c, s_lo, s_hi, cos_t, sin_t, norm_g)


def _rope_tables(seq):
    pos = jnp.arange(seq, dtype=F32)
    inv_freq = ROPE_THETA ** (-jnp.arange(0, ROPE_DIM, 2, dtype=F32) / ROPE_DIM)
    ang = pos[:, None] * inv_freq[None, :]
    cos, sin = jnp.cos(ang), jnp.sin(ang)
    ones = jnp.ones((seq, HEAD_DIM - ROPE_DIM), F32)
    c = jnp.concatenate([cos, cos, ones], axis=1)
    s_lo = jnp.concatenate([-sin, jnp.zeros((seq, HEAD_DIM - ROPE_HALF), F32)], axis=1)
    s_hi = jnp.concatenate([jnp.zeros((seq, ROPE_HALF), F32), sin, 0.0 * ones], axis=1)
    rep = LANES // HEAD_DIM
    return tuple(jnp.tile(t, (1, rep)) for t in (c, s_lo, s_hi)) + (cos.T, sin.T)


ATTN_TILE_BLOCKS = 4
MIX_CHUNK = 256


def _attn_mix_body(sink_ref, x_ref, qt_ref, k_ref, kh_ref, vt_ref, vth_ref, yh_ref, ga_ref, gh_ref,
                   wpa_ref, wph_ref, wo_ref, g_ref, b_ref, o_ref, y_ref, yat_ref, *, tiles, tiles_per_seq):
    step = pl.program_id(0)

    @pl.when(step == 0)
    def _():
        y_ref[...] = jnp.zeros_like(y_ref)

    tile = jnp.minimum(step, tiles - 1)
    first_tile = tile % tiles_per_seq == 0
    halo_bias = jnp.where(first_tile, NEG_INF, 0.0).astype(F32)
    wide = GQA_GROUP * ATTN_BLOCK
    key_i = lax.broadcasted_iota(jnp.int32, (ATTN_BLOCK, wide), 0)
    qry_i = lax.broadcasted_iota(jnp.int32, (ATTN_BLOCK, wide), 1) % ATTN_BLOCK
    from_prev = key_i > qry_i
    keep_prev = from_prev.astype(BF16)
    keep_cur = 1.0 - keep_prev
    zero_half = jnp.zeros((HEAD_DIM, ATTN_BLOCK), BF16)
    units = [(qb, kh) for qb in range(ATTN_TILE_BLOCKS) for kh in range(N_KV_HEADS)]

    def heads_of(kh):
        return [kh * GQA_GROUP + g for g in range(GQA_GROUP)]

    def toks_of(qb):
        return slice(qb * ATTN_BLOCK, (qb + 1) * ATTN_BLOCK)

    def scores(qb, kh):
        toks = toks_of(qb)
        k_prev = kh_ref[...] if qb == 0 else k_ref[(qb - 1) * ATTN_BLOCK:qb * ATTN_BLOCK, :]
        keys = jnp.concatenate([k_prev, k_ref[toks, :]], axis=0)
        pair = slice((kh // 2) * LANES, (kh // 2 + 1) * LANES)
        cols = []
        for h in heads_of(kh):
            qh = qt_ref[h * HEAD_DIM:(h + 1) * HEAD_DIM, toks]
            cols.append(jnp.concatenate([qh, zero_half] if kh % 2 == 0 else [zero_half, qh], axis=0))
        q_rhs = jnp.concatenate(cols, axis=1)
        return jnp.dot(keys[:, pair], q_rhs, preferred_element_type=F32)

    def finish(qb, kh, s):
        toks = toks_of(qb)
        heads = heads_of(kh)
        s_prev = s[:ATTN_BLOCK]
        if qb == 0:
            s_prev = s_prev + halo_bias
        c = jnp.where(from_prev, s_prev, s[ATTN_BLOCK:])
        sink = jnp.concatenate([jnp.full((1, ATTN_BLOCK), sink_ref[h] * LOG2_E, F32) for h in heads], axis=1)
        m = jnp.maximum(jnp.max(c, axis=0, keepdims=True), sink)
        p = jnp.exp2(c - m)
        denom = jnp.sum(p, axis=0, keepdims=True) + jnp.exp2(sink - m)
        pb = p.astype(BF16)
        p_both = jnp.concatenate([pb * keep_prev, pb * keep_cur], axis=0)
        vt_prev = vth_ref[...] if qb == 0 else vt_ref[:, (qb - 1) * ATTN_BLOCK:qb * ATTN_BLOCK]
        vt_both = jnp.concatenate([vt_prev[kh * HEAD_DIM:(kh + 1) * HEAD_DIM, :],
                                   vt_ref[kh * HEAD_DIM:(kh + 1) * HEAD_DIM, toks]], axis=1)
        ot = jnp.dot(vt_both, p_both, preferred_element_type=F32) * (1.0 / denom)
        for g, h in enumerate(heads):
            yat_ref[h * HEAD_DIM:(h + 1) * HEAD_DIM, toks] = ot[:, g * ATTN_BLOCK:(g + 1) * ATTN_BLOCK].astype(BF16)

    tm = x_ref.shape[0]
    n_chunks = D_MODEL // MIX_CHUNK
    chunk_cols = [slice(c * MIX_CHUNK, (c + 1) * MIX_CHUNK) for c in range(n_chunks)]
    yh = yh_ref[...]

    ph = []
    units_per_chunk = len(units) // n_chunks
    s_next = scores(*units[0])
    for i, unit in enumerate(units):
        s_cur = s_next
        if i + 1 < len(units):
            s_next = scores(*units[i + 1])
        if i % units_per_chunk == 0:
            ph.append(jnp.dot(yh, wph_ref[:, chunk_cols[i // units_per_chunk]], preferred_element_type=F32))
        finish(*unit, s_cur)

    slab_rows = tm // LN_SLABS
    slabs_per_chunk = LN_SLABS // n_chunks
    yat = yat_ref[...]
    merged = []
    for c, cols in enumerate(chunk_cols):
        pa = lax.dot_general(yat, wpa_ref[:, cols], (((0,), (0,)), ((), ())), preferred_element_type=F32)
        m = ga_ref[:, cols].astype(F32) * pa + gh_ref[:, cols].astype(F32) * ph[c]
        zero = jnp.zeros((F32_SUBLANES, LANES), F32)
        for sl in range(c * slabs_per_chunk, (c + 1) * slabs_per_chunk):
            rows = slice(sl * slab_rows, (sl + 1) * slab_rows)
            ln = _layer_norm(y_ref[rows, :], g_ref[...], b_ref[...])
            o_ref[rows, :] = ln
            zero = zero + _zero_after(ln)
        top = m[0:F32_SUBLANES] + jnp.concatenate([zero] * (MIX_CHUNK // LANES), axis=1)
        merged.append(jnp.concatenate([top, m[F32_SUBLANES:]], axis=0).astype(BF16))
    merged = jnp.concatenate(merged, axis=1)
    for cols in chunk_cols:
        mix = jnp.dot(merged, wo_ref[:, cols], preferred_element_type=F32)
        y_ref[:, cols] = DEEPNORM_ALPHA * x_ref[:, cols] + mix


def _attn_mix(sinks, x2d, qt, k, vt, y_hgrn, gates, w_pa, w_ph, w_out, g, b, *, seq):
    n = x2d.shape[0]
    tm = ATTN_TILE_BLOCKS * ATTN_BLOCK
    tiles = n // tm
    cur = lambda i: jnp.minimum(i, tiles - 1)
    halo = lambda i: jnp.maximum(cur(i) * ATTN_TILE_BLOCKS - 1, 0)
    row = lambda w, col=0: pl.BlockSpec((tm, w), lambda i: (cur(i), col))
    sq = _resident((D_MODEL, D_MODEL))
    return pl.pallas_call(
        functools.partial(_attn_mix_body, tiles=tiles, tiles_per_seq=seq // tm),
        grid=(tiles + 1,),
        in_specs=[pl.BlockSpec(memory_space=pltpu.SMEM),
                  row(D_MODEL),
                  pl.BlockSpec((ATTN_WIDTH, tm), lambda i: (0, cur(i))),
                  row(KV_WIDTH),
                  pl.BlockSpec((ATTN_BLOCK, KV_WIDTH), lambda i: (halo(i), 0)),
                  pl.BlockSpec((KV_WIDTH, tm), lambda i: (0, cur(i))),
                  pl.BlockSpec((KV_WIDTH, ATTN_BLOCK), lambda i: (0, halo(i))),
                  row(HGRN_WIDTH), row(D_MODEL, 0), row(D_MODEL, 1), sq, sq, sq,
                  _resident((1, D_MODEL)), _resident((1, D_MODEL))],
        out_specs=pl.BlockSpec((tm, D_MODEL), lambda i: (jnp.maximum(i - 1, 0), 0)),
        out_shape=jax.ShapeDtypeStruct((n, D_MODEL), F32),
        scratch_shapes=[pltpu.VMEM((tm, D_MODEL), F32),
                        pltpu.VMEM((ATTN_WIDTH, tm), BF16)],
        compiler_params=_params("arbitrary"),
        name="attn_mix",
    )(sinks, x2d, qt, k, k, vt, vt, y_hgrn, gates, gates, w_pa, w_ph, w_out, g, b)


def _cumprod_rows(f):
    row = lax.broadcasted_iota(jnp.int32, (F32_SUBLANES, LANES), 0)
    blocks, carry = [], None
    for j in range(f.shape[0] // F32_SUBLANES):
        x = f[j * F32_SUBLANES:(j + 1) * F32_SUBLANES]
        shift = 1
        while shift < F32_SUBLANES:
            x = x * jnp.where(row >= shift, pltpu.roll(x, shift, axis=0), 1.0)
            shift *= 2
        if carry is not None:
            x = x * carry
        carry = x[F32_SUBLANES - 1:F32_SUBLANES, :]
        blocks.append(x)
    return jnp.concatenate(blocks, axis=0)


def _tile(n, want):
    t = min(n, want)
    assert n % t == 0, (n, t)
    return t


def kernel(x, ln1_g, ln1_b, ffn1_w1, ffn1_w3, ffn1_w2, ln2_g, ln2_b, w_in, b_in, attn_sinks, hgrn_lb_logits,
           hgrn_norm_g, w_proj_attn, w_proj_hgrn, w_out, ln3_g, ln3_b, ffn2_w1, ffn2_w3, ffn2_w2):
    batch, seq, d = x.shape
    assert d == D_MODEL and seq % (ATTN_TILE_BLOCKS * ATTN_BLOCK) == 0
    n = batch * seq
    tm = _tile(seq, 512)
    lb_all = jnp.cumsum(jax.nn.softmax(hgrn_lb_logits.astype(F32), axis=0), axis=0)
    rope = _rope_tables(seq)
    bf = lambda w: w.astype(BF16)
    h = x.reshape(n, d)
    for l in range(DEPTH):
        h = _ffn_ln(h, bf(0.5 * ffn1_w1[l]), bf(ffn1_w3[l]), bf(0.5 * ffn1_w2[l]), ln1_g[l:l + 1], ln1_b[l:l + 1], tm=tm)
        qt, vt, k, gates, y_hgrn = _in_proj(h, w_in[l], b_in[l], lb_all[l:l + 1], hgrn_norm_g[l:l + 1], rope,
                                            tm=tm, seq=seq)
        h = _attn_mix(attn_sinks[l], h, qt, k, vt, y_hgrn, gates, bf(w_proj_attn[l]), bf(w_proj_hgrn[l]),
                      bf(w_out[l]), ln2_g[l:l + 1], ln2_b[l:l + 1], seq=seq)
        h = _ffn_ln(h, bf(0.5 * ffn2_w1[l]), bf(ffn2_w3[l]), bf(0.5 * ffn2_w2[l]), ln3_g[l:l + 1], ln3_b[l:l + 1], tm=tm)
    return h.reshape(batch, seq, d)
```

```python
import functools

import jax
import jax.numpy as jnp
from jax import lax
from jax.experimental import pallas as pl
from jax.experimental.pallas import tpu as pltpu

D_MODEL = 1024
DEPTH = 1
N_Q_HEADS = 16
N_KV_HEADS = 4
HEAD_DIM = 64
GQA_GROUP = N_Q_HEADS // N_KV_HEADS
WINDOW = 128
ATTN_BLOCK = 128
ROPE_THETA = 500000.0
ROPE_DIM = HEAD_DIM // 4
ROPE_HALF = ROPE_DIM // 2
HGRN_HEADS = 8
HGRN_DK = 128
HGRN_DV = 128
HGRN_CHUNK = 64
D_FF = 2816
ATTN_WIDTH = N_Q_HEADS * HEAD_DIM
KV_WIDTH = N_KV_HEADS * HEAD_DIM
HGRN_WIDTH = HGRN_HEADS * HGRN_DK
D_IN = ATTN_WIDTH + 2 * KV_WIDTH + 4 * HGRN_WIDTH + 2 * D_MODEL
DEEPNORM_ALPHA = (2 * DEPTH) ** 0.25
LN_EPS = 1e-5
RMS_EPS = 1e-6
NEG_INF = -1e30
LOG2_E = 1.4426950408889634

LANES = 128
F32_SUBLANES = 8
VMEM_LIMIT_BYTES = 56 * 1024 * 1024

F32 = jnp.float32
BF16 = jnp.bfloat16

assert WINDOW == ATTN_BLOCK and ROPE_HALF == F32_SUBLANES


def _layer_norm(y, g, b):
    mu = jnp.mean(y, axis=-1, keepdims=True)
    yc = y - mu
    var = jnp.mean(yc * yc, axis=-1, keepdims=True)
    return yc * lax.rsqrt(var + LN_EPS) * g + b


def _resident(shape):
    return pl.BlockSpec(shape, lambda *_: (0,) * len(shape), pipeline_mode=pl.Buffered(1))


def _params(*semantics):
    return pltpu.CompilerParams(dimension_semantics=semantics, vmem_limit_bytes=VMEM_LIMIT_BYTES)


LN_SLABS = 8


def _zero_after(x):
    bits = pltpu.bitcast(x, jnp.uint32)
    tiles = [bits[r:r + F32_SUBLANES, c:c + LANES]
             for r in range(0, x.shape[0], F32_SUBLANES) for c in range(0, x.shape[1], LANES)]
    folded = functools.reduce(jnp.bitwise_or, tiles)
    return pltpu.bitcast((folded >> 16) >> 16, F32)


def _ffn_ln_body(x_ref, w1_ref, w3_ref, w2_ref, g_ref, b_ref, o_ref, acc_ref, y_ref, *, ff_chunk):
    @pl.when(pl.program_id(0) == 0)
    def _():
        y_ref[...] = jnp.zeros_like(y_ref)

    tm = x_ref.shape[0]
    n_slabs, slab_rows = LN_SLABS, tm // LN_SLABS
    n_chunks = D_FF // ff_chunk
    assert n_slabs <= n_chunks - 1
    x = x_ref[...]
    xb = x.astype(BF16)
    for c in range(n_chunks):
        cols = slice(c * ff_chunk, (c + 1) * ff_chunk)
        h1 = jnp.dot(xb, w1_ref[:, cols], preferred_element_type=F32)
        h3 = jnp.dot(xb, w3_ref[:, cols], preferred_element_type=F32)
        a = ((h1 * jnp.tanh(h1) + h1) * h3).astype(BF16)
        part = jnp.dot(a, w2_ref[cols, :], preferred_element_type=F32)
        if c == 0:
            acc_ref[...] = DEEPNORM_ALPHA * x + part
        elif c < n_chunks - 1:
            acc_ref[...] += part
        else:
            y_ref[...] = acc_ref[...] + part
        if c < n_slabs:
            rows = slice(c * slab_rows, (c + 1) * slab_rows)
            ln = _layer_norm(y_ref[rows, :], g_ref[...], b_ref[...])
            o_ref[rows, :] = ln
            acc_ref[0:F32_SUBLANES, 0:LANES] += _zero_after(ln)


def _ffn_ln(x2d, w1, w3, w2, g, b, *, tm, ff_chunk=256):
    n = x2d.shape[0]
    tiles = n // tm
    return pl.pallas_call(
        functools.partial(_ffn_ln_body, ff_chunk=ff_chunk),
        grid=(tiles + 1,),
        in_specs=[pl.BlockSpec((tm, D_MODEL), lambda i: (jnp.minimum(i, tiles - 1), 0)),
                  _resident((D_MODEL, D_FF)), _resident((D_MODEL, D_FF)), _resident((D_FF, D_MODEL)),
                  _resident((1, D_MODEL)), _resident((1, D_MODEL))],
        out_specs=pl.BlockSpec((tm, D_MODEL), lambda i: (jnp.maximum(i - 1, 0), 0)),
        out_shape=jax.ShapeDtypeStruct((n, D_MODEL), F32),
        scratch_shapes=[pltpu.VMEM((tm, D_MODEL), F32), pltpu.VMEM((tm, D_MODEL), F32)],
        compiler_params=_params("arbitrary"),
        name="ffn_ln",
    )(x2d, w1, w3, w2, g, b)


PROJ_CHUNK = 512
QV_ROWS = ATTN_WIDTH + KV_WIDTH
QV_CHUNK = 256
TOK_WIDTH = 4 * HGRN_WIDTH + 2 * D_MODEL + KV_WIDTH


def _rope_lanes(t, c, s_lo, s_hi):
    return t * c + pltpu.roll(t, ROPE_HALF, axis=1) * s_hi + pltpu.roll(t, LANES - ROPE_HALF, axis=1) * s_lo


def _in_proj_body(x_ref, wt_ref, bt_ref, w_ref, b_ref, fa_ref, fb_ref, c_ref, slo_ref, shi_ref, cos_ref, sin_ref, ng_ref,
                  qt_ref, vt_ref, k_ref, gate_ref, yh_ref,
                  f_s, q_s, v_s, og_s, state_ref, qd_ref, ke_ref, sc_ref, dec_ref, *, tiles_per_seq):
    @pl.when(pl.program_id(0) % tiles_per_seq == 0)
    def _():
        state_ref[...] = jnp.zeros_like(state_ref)

    xb = x_ref[...].astype(BF16)
    tm = xb.shape[0]
    q_scale = HEAD_DIM ** -0.5 * LOG2_E

    def feature_major(j):
        cos_t, sin_t = cos_ref[...], sin_ref[...]
        rows = slice(j * QV_CHUNK, (j + 1) * QV_CHUNK)
        pt = lax.dot_general(wt_ref[rows, :], xb, (((1,), (1,)), ((), ())), preferred_element_type=F32)
        pt = pt + bt_ref[rows, :]
        if j * QV_CHUNK < ATTN_WIDTH:
            for h in range(QV_CHUNK // HEAD_DIM):
                r0 = h * HEAD_DIM
                t1 = pt[r0:r0 + ROPE_HALF]
                t2 = pt[r0 + ROPE_HALF:r0 + ROPE_DIM]
                rot = jnp.concatenate([t1 * cos_t - t2 * sin_t, t2 * cos_t + t1 * sin_t,
                                       pt[r0 + ROPE_DIM:r0 + HEAD_DIM]], axis=0)
                qt_ref[j * QV_CHUNK + r0:j * QV_CHUNK + r0 + HEAD_DIM, :] = (rot * q_scale).astype(BF16)
        else:
            vt_ref[...] = pt.astype(BF16)

    def token_major(lo):
        width = min(PROJ_CHUNK, TOK_WIDTH - lo)
        cols = slice(lo, lo + width)
        p = jnp.dot(xb, w_ref[:, cols], preferred_element_type=F32) + b_ref[:, cols]
        kind, off = _tok_kind(lo)
        dst = slice(off, off + width)
        if kind == "forget":
            f_s[:, dst] = fa_ref[:, dst] + fb_ref[:, dst] * jnp.tanh(p)
        elif kind == "hgrn_q":
            q_s[:, dst] = (p * jnp.tanh(p) + p).astype(BF16)
        elif kind == "hgrn_v":
            v_s[:, dst] = p.astype(BF16)
        elif kind == "hgrn_og":
            og_s[:, dst] = (p * jnp.tanh(p) + p).astype(BF16)
        elif kind in ("gate_a", "gate_h"):
            base = 0 if kind == "gate_a" else D_MODEL
            gate_ref[:, base + off:base + off + width] = (0.5 * jnp.tanh(p) + 0.5).astype(BF16)
        else:
            c, s_lo, s_hi = c_ref[...], slo_ref[...], shi_ref[...]
            for u in range(width // LANES):
                t = p[:, u * LANES:(u + 1) * LANES]
                k_ref[:, u * LANES:(u + 1) * LANES] = _rope_lanes(t, c, s_lo, s_hi).astype(BF16)

    hgrn_cols = 4 * HGRN_WIDTH
    for lo in range(0, hgrn_cols, PROJ_CHUNK):
        token_major(lo)
    rest = [functools.partial(token_major, lo) for lo in range(hgrn_cols, TOK_WIDTH, PROJ_CHUNK)]
    rest += [functools.partial(feature_major, j) for j in range(QV_ROWS // QV_CHUNK)]

    cz = HGRN_CHUNK
    n_chunks = tm // cz
    ti = lax.broadcasted_iota(jnp.int32, (cz, cz), 0)
    si = lax.broadcasted_iota(jnp.int32, (cz, cz), 1)
    causal = ti >= si
    ng = ng_ref[...]
    nt_dims = (((1,), (1,)), ((), ()))

    def gates(ci):
        rows = slice(ci * cz, (ci + 1) * cz)
        for h in range(HGRN_HEADS):
            lanes = slice(h * HGRN_DK, (h + 1) * HGRN_DK)
            f = f_s[rows, lanes]
            k = 1.0 - f
            decay = _cumprod_rows(f)
            inv = 1.0 / decay
            d_last = decay[cz - 1:cz, :]
            q_dec = (q_s[rows, lanes].astype(F32) * decay).astype(BF16)
            k_inv = (k * inv).astype(BF16)
            sc = lax.dot_general(q_dec, k_inv, nt_dims, preferred_element_type=F32)
            sc_ref[h] = jnp.where(causal, sc, 0.0).astype(BF16)
            qd_ref[:, lanes] = q_dec
            ke_ref[:, lanes] = (k * (d_last * inv)).astype(BF16)
            dec_ref[:, lanes] = d_last

    def outputs(ci):
        rows = slice(ci * cz, (ci + 1) * cz)
        for h in range(HGRN_HEADS):
            lanes = slice(h * HGRN_DK, (h + 1) * HGRN_DK)
            v = v_s[rows, lanes]
            st = state_ref[h]
            o = jnp.dot(sc_ref[h], v, preferred_element_type=F32)
            o += lax.dot_general(qd_ref[:, lanes], st.astype(BF16), nt_dims, preferred_element_type=F32)
            upd = lax.dot_general(v, ke_ref[:, lanes], (((0,), (0,)), ((), ())), preferred_element_type=F32)
            state_ref[h] = st * dec_ref[:, lanes] + upd
            o = o * lax.rsqrt(jnp.mean(o * o, axis=-1, keepdims=True) + RMS_EPS) * ng
            yh_ref[rows, lanes] = (o * og_s[rows, lanes].astype(F32)).astype(BF16)

    per_chunk = -(-len(rest) // n_chunks)
    gates(0)
    for ci in range(n_chunks):
        for piece in rest[ci * per_chunk:(ci + 1) * per_chunk]:
            piece()
        outputs(ci)
        if ci + 1 < n_chunks:
            gates(ci + 1)
    for piece in rest[n_chunks * per_chunk:]:
        piece()


_TOK_GROUPS = (("forget", HGRN_WIDTH), ("hgrn_q", HGRN_WIDTH), ("hgrn_v", HGRN_WIDTH), ("hgrn_og", HGRN_WIDTH),
               ("gate_a", D_MODEL), ("gate_h", D_MODEL), ("rope_k", KV_WIDTH))
_TANH_KINDS = ("forget", "hgrn_q", "hgrn_og", "gate_a", "gate_h")


def _tok_kind(col):
    for kind, width in _TOK_GROUPS:
        if col < width:
            return kind, col
        col -= width
    raise ValueError(col)


def _tok_half_scale():
    return jnp.concatenate([jnp.full((w,), 0.5 if kind in _TANH_KINDS else 1.0, F32)
                            for kind, w in _TOK_GROUPS])[None, :]


def _in_proj(x2d, w_in, b_in, lb, norm_g, rope, *, tm, seq):
    n = x2d.shape[0]
    tiles_per_seq = seq // tm
    q_end, k_end, v_end = ATTN_WIDTH, ATTN_WIDTH + KV_WIDTH, ATTN_WIDTH + 2 * KV_WIDTH
    w_t = jnp.concatenate([w_in[:, :q_end], w_in[:, k_end:v_end]], axis=1).T.astype(BF16)
    b_t = jnp.concatenate([b_in[:q_end], b_in[k_end:v_end]])[:, None]
    half = _tok_half_scale()
    w_tok = (jnp.concatenate([w_in[:, v_end:], w_in[:, q_end:k_end]], axis=1) * half).astype(BF16)
    b_tok = jnp.concatenate([b_in[v_end:], b_in[q_end:k_end]])[None, :] * half
    f_a = lb + 0.5 * (1.0 - lb)
    f_b = 0.5 * (1.0 - lb)
    c, s_lo, s_hi, cos_t, sin_t = rope
    row = lambda w: pl.BlockSpec((tm, w), lambda i: (i, 0))
    col = lambda r: pl.BlockSpec((r, tm), lambda i: (0, i))
    tab = pl.BlockSpec((tm, LANES), lambda i: (i % tiles_per_seq, 0))
    tab_t = pl.BlockSpec((ROPE_HALF, tm), lambda i: (0, i % tiles_per_seq))
    return pl.pallas_call(
        functools.partial(_in_proj_body, tiles_per_seq=tiles_per_seq),
        grid=(n // tm,),
        in_specs=[row(D_MODEL), _resident((QV_ROWS, D_MODEL)), _resident((QV_ROWS, 1)),
                  _resident((D_MODEL, TOK_WIDTH)), _resident((1, TOK_WIDTH)),
                  _resident((1, HGRN_WIDTH)), _resident((1, HGRN_WIDTH)), tab, tab, tab, tab_t, tab_t,
                  _resident((1, HGRN_DV))],
        out_specs=[col(ATTN_WIDTH), col(KV_WIDTH), row(KV_WIDTH), row(2 * D_MODEL), row(HGRN_WIDTH)],
        out_shape=[jax.ShapeDtypeStruct((ATTN_WIDTH, n), BF16),
                   jax.ShapeDtypeStruct((KV_WIDTH, n), BF16),
                   jax.ShapeDtypeStruct((n, KV_WIDTH), BF16),
                   jax.ShapeDtypeStruct((n, 2 * D_MODEL), BF16),
                   jax.ShapeDtypeStruct((n, HGRN_WIDTH), BF16)],
        scratch_shapes=[pltpu.VMEM((tm, HGRN_WIDTH), F32),
                        pltpu.VMEM((tm, HGRN_WIDTH), BF16),
                        pltpu.VMEM((tm, HGRN_WIDTH), BF16),
                        pltpu.VMEM((tm, HGRN_WIDTH), BF16),
                        pltpu.VMEM((HGRN_HEADS, HGRN_DV, HGRN_DK), F32),
                        pltpu.VMEM((HGRN_CHUNK, HGRN_WIDTH), BF16),
                        pltpu.VMEM((HGRN_CHUNK, HGRN_WIDTH), BF16),
                        pltpu.VMEM((HGRN_HEADS, HGRN_CHUNK, HGRN_CHUNK), BF16),
                        pltpu.VMEM((1, HGRN_WIDTH), F32)],
        compiler_params=_params("arbitrary"),
        name="in_proj",
    )(x2d, w_t, b_t, w_tok, b_tok, f_a, f_b, c, s_lo, s_hi, cos_t, sin_t, norm_g)


def _rope_tables(seq):
    pos = jnp.arange(seq, dtype=F32)
    inv_freq = ROPE_THETA ** (-jnp.arange(0, ROPE_DIM, 2, dtype=F32) / ROPE_DIM)
    ang = pos[:, None] * inv_freq[None, :]
    cos, sin = jnp.cos(ang), jnp.sin(ang)
    ones = jnp.ones((seq, HEAD_DIM - ROPE_DIM), F32)
    c = jnp.concatenate([cos, cos, ones], axis=1)
    s_lo = jnp.concatenate([-sin, jnp.zeros((seq, HEAD_DIM - ROPE_HALF), F32)], axis=1)
    s_hi = jnp.concatenate([jnp.zeros((seq, ROPE_HALF), F32), sin, 0.0 * ones], axis=1)
    rep = LANES // HEAD_DIM
    return tuple(jnp.tile(t, (1, rep)) for t in (c, s_lo, s_hi)) + (cos.T, sin.T)


ATTN_TILE_BLOCKS = 8
MIX_CHUNK = 256


def _attn_mix_body(sink_ref, x_ref, qt_ref, k_ref, kh_ref, vt_ref, vth_ref, yh_ref, ga_ref, gh_ref,
                   wpa_ref, wph_ref, wo_ref, g_ref, b_ref, o_ref, y_ref, yat_ref, *, tiles, tiles_per_seq):
    step = pl.program_id(0)

    @pl.when(step == 0)
    def _():
        y_ref[...] = jnp.zeros_like(y_ref)

    tile = jnp.minimum(step, tiles - 1)
    first_tile = tile % tiles_per_seq == 0
    halo_bias = jnp.where(first_tile, NEG_INF, 0.0).astype(F32)
    wide = GQA_GROUP * ATTN_BLOCK
    key_i = lax.broadcasted_iota(jnp.int32, (ATTN_BLOCK, wide), 0)
    qry_i = lax.broadcasted_iota(jnp.int32, (ATTN_BLOCK, wide), 1) % ATTN_BLOCK
    from_prev = key_i > qry_i
    keep_prev = from_prev.astype(BF16)
    keep_cur = 1.0 - keep_prev
    zero_half = jnp.zeros((HEAD_DIM, ATTN_BLOCK), BF16)
    units = [(qb, kh) for qb in range(ATTN_TILE_BLOCKS) for kh in range(N_KV_HEADS)]

    def heads_of(kh):
        return [kh * GQA_GROUP + g for g in range(GQA_GROUP)]

    def toks_of(qb):
        return slice(qb * ATTN_BLOCK, (qb + 1) * ATTN_BLOCK)

    def scores(qb, kh):
        toks = toks_of(qb)
        k_prev = kh_ref[...] if qb == 0 else k_ref[(qb - 1) * ATTN_BLOCK:qb * ATTN_BLOCK, :]
        keys = jnp.concatenate([k_prev, k_ref[toks, :]], axis=0)
        pair = slice((kh // 2) * LANES, (kh // 2 + 1) * LANES)
        cols = []
        for h in heads_of(kh):
            qh = qt_ref[h * HEAD_DIM:(h + 1) * HEAD_DIM, toks]
            cols.append(jnp.concatenate([qh, zero_half] if kh % 2 == 0 else [zero_half, qh], axis=0))
        q_rhs = jnp.concatenate(cols, axis=1)
        return jnp.dot(keys[:, pair], q_rhs, preferred_element_type=F32)

    def finish(qb, kh, s):
        toks = toks_of(qb)
        heads = heads_of(kh)
        s_prev = s[:ATTN_BLOCK]
        if qb == 0:
            s_prev = s_prev + halo_bias
        c = jnp.where(from_prev, s_prev, s[ATTN_BLOCK:])
        sink = jnp.concatenate([jnp.full((1, ATTN_BLOCK), sink_ref[h] * LOG2_E, F32) for h in heads], axis=1)
        m = jnp.maximum(jnp.max(c, axis=0, keepdims=True), sink)
        p = jnp.exp2(c - m)
        denom = jnp.sum(p, axis=0, keepdims=True) + jnp.exp2(sink - m)
        pb = p.astype(BF16)
        p_both = jnp.concatenate([pb * keep_prev, pb * keep_cur], axis=0)
        vt_prev = vth_ref[...] if qb == 0 else vt_ref[:, (qb - 1) * ATTN_BLOCK:qb * ATTN_BLOCK]
        vt_both = jnp.concatenate([vt_prev[kh * HEAD_DIM:(kh + 1) * HEAD_DIM, :],
                                   vt_ref[kh * HEAD_DIM:(kh + 1) * HEAD_DIM, toks]], axis=1)
        ot = jnp.dot(vt_both, p_both, preferred_element_type=F32) * (1.0 / denom)
        for g, h in enumerate(heads):
            yat_ref[h * HEAD_DIM:(h + 1) * HEAD_DIM, toks] = ot[:, g * ATTN_BLOCK:(g + 1) * ATTN_BLOCK].astype(BF16)

    tm = x_ref.shape[0]
    n_chunks = D_MODEL // MIX_CHUNK
    chunk_cols = [slice(c * MIX_CHUNK, (c + 1) * MIX_CHUNK) for c in range(n_chunks)]
    yh = yh_ref[...]

    ph = []
    units_per_chunk = len(units) // n_chunks
    s_next = scores(*units[0])
    for i, unit in enumerate(units):
        s_cur = s_next
        if i + 1 < len(units):
            s_next = scores(*units[i + 1])
        if i % units_per_chunk == 0:
            ph.append(jnp.dot(yh, wph_ref[:, chunk_cols[i // units_per_chunk]], preferred_element_type=F32))
        finish(*unit, s_cur)

    slab_rows = tm // LN_SLABS
    slabs_per_chunk = LN_SLABS // n_chunks
    yat = yat_ref[...]
    merged = []
    for c, cols in enumerate(chunk_cols):
        pa = lax.dot_general(yat, wpa_ref[:, cols], (((0,), (0,)), ((), ())), preferred_element_type=F32)
        m = ga_ref[:, cols].astype(F32) * pa + gh_ref[:, cols].astype(F32) * ph[c]
        zero = jnp.zeros((F32_SUBLANES, LANES), F32)
        for sl in range(c * slabs_per_chunk, (c + 1) * slabs_per_chunk):
            rows = slice(sl * slab_rows, (sl + 1) * slab_rows)
            ln = _layer_norm(y_ref[rows, :], g_ref[...], b_ref[...])
            o_ref[rows, :] = ln
            zero = zero + _zero_after(ln)
        top = m[0:F32_SUBLANES] + jnp.concatenate([zero] * (MIX_CHUNK // LANES), axis=1)
        merged.append(jnp.concatenate([top, m[F32_SUBLANES:]], axis=0).astype(BF16))
    merged = jnp.concatenate(merged, axis=1)
    for cols in chunk_cols:
        mix = jnp.dot(merged, wo_ref[:, cols], preferred_element_type=F32)
        y_ref[:, cols] = DEEPNORM_ALPHA * x_ref[:, cols] + mix


def _attn_mix(sinks, x2d, qt, k, vt, y_hgrn, gates, w_pa, w_ph, w_out, g, b, *, seq):
    n = x2d.shape[0]
    tm = ATTN_TILE_BLOCKS * ATTN_BLOCK
    tiles = n // tm
    cur = lambda i: jnp.minimum(i, tiles - 1)
    halo = lambda i: jnp.maximum(cur(i) * ATTN_TILE_BLOCKS - 1, 0)
    row = lambda w, col=0: pl.BlockSpec((tm, w), lambda i: (cur(i), col))
    sq = _resident((D_MODEL, D_MODEL))
    return pl.pallas_call(
        functools.partial(_attn_mix_body, tiles=tiles, tiles_per_seq=seq // tm),
        grid=(tiles + 1,),
        in_specs=[pl.BlockSpec(memory_space=pltpu.SMEM),
                  row(D_MODEL),
                  pl.BlockSpec((ATTN_WIDTH, tm), lambda i: (0, cur(i))),
                  row(KV_WIDTH),
                  pl.BlockSpec((ATTN_BLOCK, KV_WIDTH), lambda i: (halo(i), 0)),
                  pl.BlockSpec((KV_WIDTH, tm), lambda i: (0, cur(i))),
                  pl.BlockSpec((KV_WIDTH, ATTN_BLOCK), lambda i: (0, halo(i))),
                  row(HGRN_WIDTH), row(D_MODEL, 0), row(D_MODEL, 1), sq, sq, sq,
                  _resident((1, D_MODEL)), _resident((1, D_MODEL))],
        out_specs=pl.BlockSpec((tm, D_MODEL), lambda i: (jnp.maximum(i - 1, 0), 0)),
        out_shape=jax.ShapeDtypeStruct((n, D_MODEL), F32),
        scratch_shapes=[pltpu.VMEM((tm, D_MODEL), F32),
                        pltpu.VMEM((ATTN_WIDTH, tm), BF16)],
        compiler_params=_params("arbitrary"),
        name="attn_mix",
    )(sinks, x2d, qt, k, k, vt, vt, y_hgrn, gates, gates, w_pa, w_ph, w_out, g, b)


def _cumprod_rows(f):
    row = lax.broadcasted_iota(jnp.int32, (F32_SUBLANES, LANES), 0)
    blocks, carry = [], None
    for j in range(f.shape[0] // F32_SUBLANES):
        x = f[j * F32_SUBLANES:(j + 1) * F32_SUBLANES]
        shift = 1
        while shift < F32_SUBLANES:
            x = x * jnp.where(row >= shift, pltpu.roll(x, shift, axis=0), 1.0)
            shift *= 2
        if carry is not None:
            x = x * carry
        carry = x[F32_SUBLANES - 1:F32_SUBLANES, :]
        blocks.append(x)
    return jnp.concatenate(blocks, axis=0)


def _tile(n, want):
    t = min(n, want)
    assert n % t == 0, (n, t)
    return t


def kernel(x, ln1_g, ln1_b, ffn1_w1, ffn1_w3, ffn1_w2, ln2_g, ln2_b, w_in, b_in, attn_sinks, hgrn_lb_logits,
           hgrn_norm_g, w_proj_attn, w_proj_hgrn, w_out, ln3_g, ln3_b, ffn2_w1, ffn2_w3, ffn2_w2):
    batch, seq, d = x.shape
    assert d == D_MODEL and seq % (ATTN_TILE_BLOCKS * ATTN_BLOCK) == 0
    n = batch * seq
    tm = _tile(seq, 512)
    lb_all = jnp.cumsum(jax.nn.softmax(hgrn_lb_logits.astype(F32), axis=0), axis=0)
    rope = _rope_tables(seq)
    bf = lambda w: w.astype(BF16)
    h = x.reshape(n, d)
    for l in range(DEPTH):
        h = _ffn_ln(h, bf(0.5 * ffn1_w1[l]), bf(ffn1_w3[l]), bf(0.5 * ffn1_w2[l]), ln1_g[l:l + 1], ln1_b[l:l + 1], tm=tm)
        qt, vt, k, gates, y_hgrn = _in_proj(h, w_in[l], b_in[l], lb_all[l:l + 1], hgrn_norm_g[l:l + 1], rope,
                                            tm=tm, seq=seq)
        h = _attn_mix(attn_sinks[l], h, qt, k, vt, y_hgrn, gates, bf(w_proj_attn[l]), bf(w_proj_hgrn[l]),
                      bf(w_out[l]), ln2_g[l:l + 1], ln2_b[l:l + 1], seq=seq)
        h = _ffn_ln(h, bf(0.5 * ffn2_w1[l]), bf(ffn2_w3[l]), bf(0.5 * ffn2_w2[l]), ln3_g[l:l + 1], ln3_b[l:l + 1], tm=tm)
    return h.reshape(batch, seq, d)
```

```python
import functools

import jax
import jax.numpy as jnp
from jax import lax
from jax.experimental import pallas as pl
from jax.experimental.pallas import tpu as pltpu

D_MODEL = 1024
DEPTH = 1
N_Q_HEADS = 16
N_KV_HEADS = 4
HEAD_DIM = 64
GQA_GROUP = N_Q_HEADS // N_KV_HEADS
WINDOW = 128
ATTN_BLOCK = 128
ROPE_THETA = 500000.0
ROPE_DIM = HEAD_DIM // 4
ROPE_HALF = ROPE_DIM // 2
HGRN_HEADS = 8
HGRN_DK = 128
HGRN_DV = 128
HGRN_CHUNK = 64
D_FF = 2816
ATTN_WIDTH = N_Q_HEADS * HEAD_DIM
KV_WIDTH = N_KV_HEADS * HEAD_DIM
HGRN_WIDTH = HGRN_HEADS * HGRN_DK
D_IN = ATTN_WIDTH + 2 * KV_WIDTH + 4 * HGRN_WIDTH + 2 * D_MODEL
DEEPNORM_ALPHA = (2 * DEPTH) ** 0.25
LN_EPS = 1e-5
RMS_EPS = 1e-6
NEG_INF = -1e30
LOG2_E = 1.4426950408889634

LANES = 128
F32_SUBLANES = 8
VMEM_LIMIT_BYTES = 56 * 1024 * 1024

F32 = jnp.float32
BF16 = jnp.bfloat16

assert WINDOW == ATTN_BLOCK and ROPE_HALF == F32_SUBLANES


def _layer_norm(y, g, b):
    mu = jnp.mean(y, axis=-1, keepdims=True)
    yc = y - mu
    var = jnp.mean(yc * yc, axis=-1, keepdims=True)
    return yc * lax.rsqrt(var + LN_EPS) * g + b


def _resident(shape):
    return pl.BlockSpec(shape, lambda *_: (0,) * len(shape), pipeline_mode=pl.Buffered(1))


def _params(*semantics):
    return pltpu.CompilerParams(dimension_semantics=semantics, vmem_limit_bytes=VMEM_LIMIT_BYTES)


LN_SLABS = 8


def _zero_after(x):
    bits = pltpu.bitcast(x, jnp.uint32)
    tiles = [bits[r:r + F32_SUBLANES, c:c + LANES]
             for r in range(0, x.shape[0], F32_SUBLANES) for c in range(0, x.shape[1], LANES)]
    folded = functools.reduce(jnp.bitwise_or, tiles)
    return pltpu.bitcast((folded >> 16) >> 16, F32)


def _ffn_ln_body(x_ref, w1_ref, w3_ref, w2_ref, g_ref, b_ref, o_ref, acc_ref, y_ref, *, ff_chunk, tiles):
    step = pl.program_id(0)

    @pl.when(step == 0)
    def _():
        y_ref[...] = jnp.zeros_like(y_ref)

    @pl.when(step == tiles)
    def _():
        o_ref[...] = _layer_norm(y_ref[...], g_ref[...], b_ref[...])

    @pl.when(step < tiles)
    def _():
        tm = x_ref.shape[0]
        n_slabs, slab_rows = LN_SLABS, tm // LN_SLABS
        n_chunks = D_FF // ff_chunk
        assert n_slabs <= n_chunks - 1
        x = x_ref[...]
        xb = x.astype(BF16)
        for c in range(n_chunks):
            cols = slice(c * ff_chunk, (c + 1) * ff_chunk)
            h1 = jnp.dot(xb, w1_ref[:, cols], preferred_element_type=F32)
            h3 = jnp.dot(xb, w3_ref[:, cols], preferred_element_type=F32)
            a = ((h1 * jnp.tanh(h1) + h1) * h3).astype(BF16)
            part = jnp.dot(a, w2_ref[cols, :], preferred_element_type=F32)
            if c == 0:
                acc_ref[...] = DEEPNORM_ALPHA * x + part
            elif c < n_chunks - 1:
                acc_ref[...] += part
            else:
                y_ref[...] = acc_ref[...] + part
            if c < n_slabs:
                rows = slice(c * slab_rows, (c + 1) * slab_rows)
                ln = _layer_norm(y_ref[rows, :], g_ref[...], b_ref[...])
                o_ref[rows, :] = ln
                acc_ref[0:F32_SUBLANES, 0:LANES] += _zero_after(ln)


def _ffn_ln(x2d, w1, w3, w2, g, b, *, tm, ff_chunk=256):
    n = x2d.shape[0]
    tiles = n // tm
    return pl.pallas_call(
        functools.partial(_ffn_ln_body, ff_chunk=ff_chunk, tiles=tiles),
        grid=(tiles + 1,),
        in_specs=[pl.BlockSpec((tm, D_MODEL), lambda i: (jnp.minimum(i, tiles - 1), 0)),
                  _resident((D_MODEL, D_FF)), _resident((D_MODEL, D_FF)), _resident((D_FF, D_MODEL)),
                  _resident((1, D_MODEL)), _resident((1, D_MODEL))],
        out_specs=pl.BlockSpec((tm, D_MODEL), lambda i: (jnp.maximum(i - 1, 0), 0)),
        out_shape=jax.ShapeDtypeStruct((n, D_MODEL), F32),
        scratch_shapes=[pltpu.VMEM((tm, D_MODEL), F32), pltpu.VMEM((tm, D_MODEL), F32)],
        compiler_params=_params("arbitrary"),
        name="ffn_ln",
    )(x2d, w1, w3, w2, g, b)


PROJ_CHUNK = 512
QV_ROWS = ATTN_WIDTH + KV_WIDTH
QV_CHUNK = 256
TOK_WIDTH = 4 * HGRN_WIDTH + 2 * D_MODEL + KV_WIDTH


def _rope_lanes(t, c, s_lo, s_hi):
    return t * c + pltpu.roll(t, ROPE_HALF, axis=1) * s_hi + pltpu.roll(t, LANES - ROPE_HALF, axis=1) * s_lo


def _in_proj_body(x_ref, wt_ref, bt_ref, w_ref, b_ref, fa_ref, fb_ref, c_ref, slo_ref, shi_ref, cos_ref, sin_ref, ng_ref,
                  qt_ref, vt_ref, k_ref, gate_ref, yh_ref,
                  f_s, q_s, v_s, og_s, state_ref, qd_ref, ke_ref, sc_ref, dec_ref, *, tiles_per_seq):
    @pl.when(pl.program_id(0) % tiles_per_seq == 0)
    def _():
        state_ref[...] = jnp.zeros_like(state_ref)

    xb = x_ref[...].astype(BF16)
    tm = xb.shape[0]
    q_scale = HEAD_DIM ** -0.5 * LOG2_E

    def feature_major(j):
        cos_t, sin_t = cos_ref[...], sin_ref[...]
        rows = slice(j * QV_CHUNK, (j + 1) * QV_CHUNK)
        pt = lax.dot_general(wt_ref[rows, :], xb, (((1,), (1,)), ((), ())), preferred_element_type=F32)
        pt = pt + bt_ref[rows, :]
        if j * QV_CHUNK < ATTN_WIDTH:
            for h in range(QV_CHUNK // HEAD_DIM):
                r0 = h * HEAD_DIM
                t1 = pt[r0:r0 + ROPE_HALF]
                t2 = pt[r0 + ROPE_HALF:r0 + ROPE_DIM]
                rot = jnp.concatenate([t1 * cos_t - t2 * sin_t, t2 * cos_t + t1 * sin_t,
                                       pt[r0 + ROPE_DIM:r0 + HEAD_DIM]], axis=0)
                qt_ref[j * QV_CHUNK + r0:j * QV_CHUNK + r0 + HEAD_DIM, :] = (rot * q_scale).astype(BF16)
        else:
            vt_ref[...] = pt.astype(BF16)

    def token_major(lo):
        width = min(PROJ_CHUNK, TOK_WIDTH - lo)
        cols = slice(lo, lo + width)
        p = jnp.dot(xb, w_ref[:, cols], preferred_element_type=F32) + b_ref[:, cols]
        kind, off = _tok_kind(lo)
        dst = slice(off, off + width)
        if kind == "forget":
            f_s[:, dst] = fa_ref[:, dst] + fb_ref[:, dst] * jnp.tanh(p)
        elif kind == "hgrn_q":
            q_s[:, dst] = (p * jnp.tanh(p) + p).astype(BF16)
        elif kind == "hgrn_v":
            v_s[:, dst] = p.astype(BF16)
        elif kind == "hgrn_og":
            og_s[:, dst] = (p * jnp.tanh(p) + p).astype(BF16)
        elif kind in ("gate_a", "gate_h"):
            base = 0 if kind == "gate_a" else D_MODEL
            gate_ref[:, base + off:base + off + width] = (0.5 * jnp.tanh(p) + 0.5).astype(BF16)
        else:
            c, s_lo, s_hi = c_ref[...], slo_ref[...], shi_ref[...]
            for u in range(width // LANES):
                t = p[:, u * LANES:(u + 1) * LANES]
                k_ref[:, u * LANES:(u + 1) * LANES] = _rope_lanes(t, c, s_lo, s_hi).astype(BF16)

    hgrn_cols = 4 * HGRN_WIDTH
    for lo in range(0, hgrn_cols, PROJ_CHUNK):
        token_major(lo)
    rest = [functools.partial(token_major, lo) for lo in range(hgrn_cols, TOK_WIDTH, PROJ_CHUNK)]
    rest += [functools.partial(feature_major, j) for j in range(QV_ROWS // QV_CHUNK)]

    cz = HGRN_CHUNK
    n_chunks = tm // cz
    ti = lax.broadcasted_iota(jnp.int32, (cz, cz), 0)
    si = lax.broadcasted_iota(jnp.int32, (cz, cz), 1)
    causal = ti >= si
    ng = ng_ref[...]
    nt_dims = (((1,), (1,)), ((), ()))

    def gates(ci):
        rows = slice(ci * cz, (ci + 1) * cz)
        for h in range(HGRN_HEADS):
            lanes = slice(h * HGRN_DK, (h + 1) * HGRN_DK)
            f = f_s[rows, lanes]
            k = 1.0 - f
            decay = _cumprod_rows(f)
            inv = 1.0 / decay
            d_last = decay[cz - 1:cz, :]
            q_dec = (q_s[rows, lanes].astype(F32) * decay).astype(BF16)
            k_inv = (k * inv).astype(BF16)
            sc = lax.dot_general(q_dec, k_inv, nt_dims, preferred_element_type=F32)
            sc_ref[h] = jnp.where(causal, sc, 0.0).astype(BF16)
            qd_ref[:, lanes] = q_dec
            ke_ref[:, lanes] = (k * (d_last * inv)).astype(BF16)
            dec_ref[:, lanes] = d_last

    def outputs(ci):
        rows = slice(ci * cz, (ci + 1) * cz)
        for h in range(HGRN_HEADS):
            lanes = slice(h * HGRN_DK, (h + 1) * HGRN_DK)
            v = v_s[rows, lanes]
            st = state_ref[h]
            o = jnp.dot(sc_ref[h], v, preferred_element_type=F32)
            o += lax.dot_general(qd_ref[:, lanes], st.astype(BF16), nt_dims, preferred_element_type=F32)
            upd = lax.dot_general(v, ke_ref[:, lanes], (((0,), (0,)), ((), ())), preferred_element_type=F32)
            state_ref[h] = st * dec_ref[:, lanes] + upd
            o = o * lax.rsqrt(jnp.mean(o * o, axis=-1, keepdims=True) + RMS_EPS) * ng
            yh_ref[rows, lanes] = (o * og_s[rows, lanes].astype(F32)).astype(BF16)

    per_chunk = -(-len(rest) // n_chunks)
    gates(0)
    for ci in range(n_chunks):
        for piece in rest[ci * per_chunk:(ci + 1) * per_chunk]:
            piece()
        outputs(ci)
        if ci + 1 < n_chunks:
            gates(ci + 1)
    for piece in rest[n_chunks * per_chunk:]:
        piece()


_TOK_GROUPS = (("forget", HGRN_WIDTH), ("hgrn_q", HGRN_WIDTH), ("hgrn_v", HGRN_WIDTH), ("hgrn_og", HGRN_WIDTH),
               ("gate_a", D_MODEL), ("gate_h", D_MODEL), ("rope_k", KV_WIDTH))
_TANH_KINDS = ("forget", "hgrn_q", "hgrn_og", "gate_a", "gate_h")


def _tok_kind(col):
    for kind, width in _TOK_GROUPS:
        if col < width:
            return kind, col
        col -= width
    raise ValueError(col)


def _tok_half_scale():
    return jnp.concatenate([jnp.full((w,), 0.5 if kind in _TANH_KINDS else 1.0, F32)
                            for kind, w in _TOK_GROUPS])[None, :]


def _in_proj(x2d, w_in, b_in, lb, norm_g, rope, *, tm, seq):
    n = x2d.shape[0]
    tiles_per_seq = seq // tm
    q_end, k_end, v_end = ATTN_WIDTH, ATTN_WIDTH + KV_WIDTH, ATTN_WIDTH + 2 * KV_WIDTH
    w_t = jnp.concatenate([w_in[:, :q_end], w_in[:, k_end:v_end]], axis=1).T.astype(BF16)
    b_t = jnp.concatenate([b_in[:q_end], b_in[k_end:v_end]])[:, None]
    half = _tok_half_scale()
    w_tok = (jnp.concatenate([w_in[:, v_end:], w_in[:, q_end:k_end]], axis=1) * half).astype(BF16)
    b_tok = jnp.concatenate([b_in[v_end:], b_in[q_end:k_end]])[None, :] * half
    f_a = lb + 0.5 * (1.0 - lb)
    f_b = 0.5 * (1.0 - lb)
    c, s_lo, s_hi, cos_t, sin_t = rope
    row = lambda w: pl.BlockSpec((tm, w), lambda i: (i, 0))
    col = lambda r: pl.BlockSpec((r, tm), lambda i: (0, i))
    tab = pl.BlockSpec((tm, LANES), lambda i: (i % tiles_per_seq, 0))
    tab_t = pl.BlockSpec((ROPE_HALF, tm), lambda i: (0, i % tiles_per_seq))
    return pl.pallas_call(
        functools.partial(_in_proj_body, tiles_per_seq=tiles_per_seq),
        grid=(n // tm,),
        in_specs=[row(D_MODEL), _resident((QV_ROWS, D_MODEL)), _resident((QV_ROWS, 1)),
                  _resident((D_MODEL, TOK_WIDTH)), _resident((1, TOK_WIDTH)),
                  _resident((1, HGRN_WIDTH)), _resident((1, HGRN_WIDTH)), tab, tab, tab, tab_t, tab_t,
                  _resident((1, HGRN_DV))],
        out_specs=[col(ATTN_WIDTH), col(KV_WIDTH), row(KV_WIDTH), row(2 * D_MODEL), row(HGRN_WIDTH)],
        out_shape=[jax.ShapeDtypeStruct((ATTN_WIDTH, n), BF16),
                   jax.ShapeDtypeStruct((KV_WIDTH, n), BF16),
                   jax.ShapeDtypeStruct((n, KV_WIDTH), BF16),
                   jax.ShapeDtypeStruct((n, 2 * D_MODEL), BF16),
                   jax.ShapeDtypeStruct((n, HGRN_WIDTH), BF16)],
        scratch_shapes=[pltpu.VMEM((tm, HGRN_WIDTH), F32),
                        pltpu.VMEM((tm, HGRN_WIDTH), BF16),
                        pltpu.VMEM((tm, HGRN_WIDTH), BF16),
                        pltpu.VMEM((tm, HGRN_WIDTH), BF16),
                        pltpu.VMEM((HGRN_HEADS, HGRN_DV, HGRN_DK), F32),
                        pltpu.VMEM((HGRN_CHUNK, HGRN_WIDTH), BF16),
                        pltpu.VMEM((HGRN_CHUNK, HGRN_WIDTH), BF16),
                        pltpu.VMEM((HGRN_HEADS, HGRN_CHUNK, HGRN_CHUNK), BF16),
                        pltpu.VMEM((1, HGRN_WIDTH), F32)],
        compiler_params=_params("arbitrary"),
        name="in_proj",
    )(x2d, w_t, b_t, w_tok, b_tok, f_a, f_b, c, s_lo, s_hi, cos_t, sin_t, norm_g)


def _rope_tables(seq):
    pos = jnp.arange(seq, dtype=F32)
    inv_freq = ROPE_THETA ** (-jnp.arange(0, ROPE_DIM, 2, dtype=F32) / ROPE_DIM)
    ang = pos[:, None] * inv_freq[None, :]
    cos, sin = jnp.cos(ang), jnp.sin(ang)
    ones = jnp.ones((seq, HEAD_DIM - ROPE_DIM), F32)
    c = jnp.concatenate([cos, cos, ones], axis=1)
    s_lo = jnp.concatenate([-sin, jnp.zeros((seq, HEAD_DIM - ROPE_HALF), F32)], axis=1)
    s_hi = jnp.concatenate([jnp.zeros((seq, ROPE_HALF), F32), sin, 0.0 * ones], axis=1)
    rep = LANES // HEAD_DIM
    return tuple(jnp.tile(t, (1, rep)) for t in (c, s_lo, s_hi)) + (cos.T, sin.T)


ATTN_TILE_BLOCKS = 8
MIX_CHUNK = 256


def _attn_mix_body(sink_ref, x_ref, qt_ref, k_ref, kh_ref, vt_ref, vth_ref, yh_ref, ga_ref, gh_ref,
                   wpa_ref, wph_ref, wo_ref, g_ref, b_ref, o_ref, y_ref, yat_ref, *, tiles, tiles_per_seq):
    step = pl.program_id(0)

    @pl.when(step == 0)
    def _():
        y_ref[...] = jnp.zeros_like(y_ref)

    @pl.when(step == tiles)
    def _():
        o_ref[...] = _layer_norm(y_ref[...], g_ref[...], b_ref[...])

    @pl.when(step < tiles)
    def _():
        _attn_mix_tile(sink_ref, x_ref, qt_ref, k_ref, kh_ref, vt_ref, vth_ref, yh_ref, ga_ref, gh_ref,
                       wpa_ref, wph_ref, wo_ref, g_ref, b_ref, o_ref, y_ref, yat_ref,
                       first_tile=step % tiles_per_seq == 0)


def _attn_mix_tile(sink_ref, x_ref, qt_ref, k_ref, kh_ref, vt_ref, vth_ref, yh_ref, ga_ref, gh_ref,
                   wpa_ref, wph_ref, wo_ref, g_ref, b_ref, o_ref, y_ref, yat_ref, *, first_tile):
    halo_bias = jnp.where(first_tile, NEG_INF, 0.0).astype(F32)
    wide = GQA_GROUP * ATTN_BLOCK
    key_i = lax.broadcasted_iota(jnp.int32, (ATTN_BLOCK, wide), 0)
    qry_i = lax.broadcasted_iota(jnp.int32, (ATTN_BLOCK, wide), 1) % ATTN_BLOCK
    from_prev = key_i > qry_i
    keep_prev = from_prev.astype(BF16)
    keep_cur = 1.0 - keep_prev
    zero_half = jnp.zeros((HEAD_DIM, ATTN_BLOCK), BF16)
    units = [(qb, kh) for qb in range(ATTN_TILE_BLOCKS) for kh in range(N_KV_HEADS)]

    def heads_of(kh):
        return [kh * GQA_GROUP + g for g in range(GQA_GROUP)]

    def toks_of(qb):
        return slice(qb * ATTN_BLOCK, (qb + 1) * ATTN_BLOCK)

    def scores(qb, kh):
        toks = toks_of(qb)
        k_prev = kh_ref[...] if qb == 0 else k_ref[(qb - 1) * ATTN_BLOCK:qb * ATTN_BLOCK, :]
        keys = jnp.concatenate([k_prev, k_ref[toks, :]], axis=0)
        pair = slice((kh // 2) * LANES, (kh // 2 + 1) * LANES)
        cols = []
        for h in heads_of(kh):
            qh = qt_ref[h * HEAD_DIM:(h + 1) * HEAD_DIM, toks]
            cols.append(jnp.concatenate([qh, zero_half] if kh % 2 == 0 else [zero_half, qh], axis=0))
        q_rhs = jnp.concatenate(cols, axis=1)
        return jnp.dot(keys[:, pair], q_rhs, preferred_element_type=F32)

    def finish(qb, kh, s):
        toks = toks_of(qb)
        heads = heads_of(kh)
        s_prev = s[:ATTN_BLOCK]
        if qb == 0:
            s_prev = s_prev + halo_bias
        c = jnp.where(from_prev, s_prev, s[ATTN_BLOCK:])
        sink = jnp.concatenate([jnp.full((1, ATTN_BLOCK), sink_ref[h] * LOG2_E, F32) for h in heads], axis=1)
        m = jnp.maximum(jnp.max(c, axis=0, keepdims=True), sink)
        p = jnp.exp2(c - m)
        denom = jnp.sum(p, axis=0, keepdims=True) + jnp.exp2(sink - m)
        pb = p.astype(BF16)
        p_both = jnp.concatenate([pb * keep_prev, pb * keep_cur], axis=0)
        vt_prev = vth_ref[...] if qb == 0 else vt_ref[:, (qb - 1) * ATTN_BLOCK:qb * ATTN_BLOCK]
        vt_both = jnp.concatenate([vt_prev[kh * HEAD_DIM:(kh + 1) * HEAD_DIM, :],
                                   vt_ref[kh * HEAD_DIM:(kh + 1) * HEAD_DIM, toks]], axis=1)
        ot = jnp.dot(vt_both, p_both, preferred_element_type=F32) * (1.0 / denom)
        for g, h in enumerate(heads):
            yat_ref[h * HEAD_DIM:(h + 1) * HEAD_DIM, toks] = ot[:, g * ATTN_BLOCK:(g + 1) * ATTN_BLOCK].astype(BF16)

    tm = x_ref.shape[0]
    n_chunks = D_MODEL // MIX_CHUNK
    chunk_cols = [slice(c * MIX_CHUNK, (c + 1) * MIX_CHUNK) for c in range(n_chunks)]
    yh = yh_ref[...]

    ph = []
    units_per_chunk = len(units) // n_chunks
    s_next = scores(*units[0])
    for i, unit in enumerate(units):
        s_cur = s_next
        if i + 1 < len(units):
            s_next = scores(*units[i + 1])
        if i % units_per_chunk == 0:
            ph.append(jnp.dot(yh, wph_ref[:, chunk_cols[i // units_per_chunk]], preferred_element_type=F32))
        finish(*unit, s_cur)

    slab_rows = tm // LN_SLABS
    slabs_per_chunk = LN_SLABS // n_chunks
    yat = yat_ref[...]
    merged = []
    for c, cols in enumerate(chunk_cols):
        pa = lax.dot_general(yat, wpa_ref[:, cols], (((0,), (0,)), ((), ())), preferred_element_type=F32)
        m = ga_ref[:, cols].astype(F32) * pa + gh_ref[:, cols].astype(F32) * ph[c]
        zero = jnp.zeros((F32_SUBLANES, LANES), F32)
        for sl in range(c * slabs_per_chunk, (c + 1) * slabs_per_chunk):
            rows = slice(sl * slab_rows, (sl + 1) * slab_rows)
            ln = _layer_norm(y_ref[rows, :], g_ref[...], b_ref[...])
            o_ref[rows, :] = ln
            zero = zero + _zero_after(ln)
        top = m[0:F32_SUBLANES] + jnp.concatenate([zero] * (MIX_CHUNK // LANES), axis=1)
        merged.append(jnp.concatenate([top, m[F32_SUBLANES:]], axis=0).astype(BF16))
    merged = jnp.concatenate(merged, axis=1)
    for cols in chunk_cols:
        mix = jnp.dot(merged, wo_ref[:, cols], preferred_element_type=F32)
        y_ref[:, cols] = DEEPNORM_ALPHA * x_ref[:, cols] + mix


def _attn_mix(sinks, x2d, qt, k, vt, y_hgrn, gates, w_pa, w_ph, w_out, g, b, *, seq):
    n = x2d.shape[0]
    tm = ATTN_TILE_BLOCKS * ATTN_BLOCK
    tiles = n // tm
    cur = lambda i: jnp.minimum(i, tiles - 1)
    halo = lambda i: jnp.maximum(cur(i) * ATTN_TILE_BLOCKS - 1, 0)
    row = lambda w, col=0: pl.BlockSpec((tm, w), lambda i: (cur(i), col))
    sq = _resident((D_MODEL, D_MODEL))
    return pl.pallas_call(
        functools.partial(_attn_mix_body, tiles=tiles, tiles_per_seq=seq // tm),
        grid=(tiles + 1,),
        in_specs=[pl.BlockSpec(memory_space=pltpu.SMEM),
                  row(D_MODEL),
                  pl.BlockSpec((ATTN_WIDTH, tm), lambda i: (0, cur(i))),
                  row(KV_WIDTH),
                  pl.BlockSpec((ATTN_BLOCK, KV_WIDTH), lambda i: (halo(i), 0)),
                  pl.BlockSpec((KV_WIDTH, tm), lambda i: (0, cur(i))),
                  pl.BlockSpec((KV_WIDTH, ATTN_BLOCK), lambda i: (0, halo(i))),
                  row(HGRN_WIDTH), row(D_MODEL, 0), row(D_MODEL, 1), sq, sq, sq,
                  _resident((1, D_MODEL)), _resident((1, D_MODEL))],
        out_specs=pl.BlockSpec((tm, D_MODEL), lambda i: (jnp.maximum(i - 1, 0), 0)),
        out_shape=jax.ShapeDtypeStruct((n, D_MODEL), F32),
        scratch_shapes=[pltpu.VMEM((tm, D_MODEL), F32),
                        pltpu.VMEM((ATTN_WIDTH, tm), BF16)],
        compiler_params=_params("arbitrary"),
        name="attn_mix",
    )(sinks, x2d, qt, k, k, vt, vt, y_hgrn, gates, gates, w_pa, w_ph, w_out, g, b)


def _cumprod_rows(f):
    row = lax.broadcasted_iota(jnp.int32, (F32_SUBLANES, LANES), 0)
    blocks, carry = [], None
    for j in range(f.shape[0] // F32_SUBLANES):
        x = f[j * F32_SUBLANES:(j + 1) * F32_SUBLANES]
        shift = 1
        while shift < F32_SUBLANES:
            x = x * jnp.where(row >= shift, pltpu.roll(x, shift, axis=0), 1.0)
            shift *= 2
        if carry is not None:
            x = x * carry
        carry = x[F32_SUBLANES - 1:F32_SUBLANES, :]
        blocks.append(x)
    return jnp.concatenate(blocks, axis=0)


def _tile(n, want):
    t = min(n, want)
    assert n % t == 0, (n, t)
    return t


def kernel(x, ln1_g, ln1_b, ffn1_w1, ffn1_w3, ffn1_w2, ln2_g, ln2_b, w_in, b_in, attn_sinks, hgrn_lb_logits,
           hgrn_norm_g, w_proj_attn, w_proj_hgrn, w_out, ln3_g, ln3_b, ffn2_w1, ffn2_w3, ffn2_w2):
    batch, seq, d = x.shape
    assert d == D_MODEL and seq % (ATTN_TILE_BLOCKS * ATTN_BLOCK) == 0
    n = batch * seq
    tm = _tile(seq, 512)
    lb_all = jnp.cumsum(jax.nn.softmax(hgrn_lb_logits.astype(F32), axis=0), axis=0)
    rope = _rope_tables(seq)
    bf = lambda w: w.astype(BF16)
    h = x.reshape(n, d)
    for l in range(DEPTH):
        h = _ffn_ln(h, bf(0.5 * ffn1_w1[l]), bf(ffn1_w3[l]), bf(0.5 * ffn1_w2[l]), ln1_g[l:l + 1], ln1_b[l:l + 1], tm=tm)
        qt, vt, k, gates, y_hgrn = _in_proj(h, w_in[l], b_in[l], lb_all[l:l + 1], hgrn_norm_g[l:l + 1], rope,
                                            tm=tm, seq=seq)
        h = _attn_mix(attn_sinks[l], h, qt, k, vt, y_hgrn, gates, bf(w_proj_attn[l]), bf(w_proj_hgrn[l]),
                      bf(w_out[l]), ln2_g[l:l + 1], ln2_b[l:l + 1], seq=seq)
        h = _ffn_ln(h, bf(0.5 * ffn2_w1[l]), bf(ffn2_w3[l]), bf(0.5 * ffn2_w2[l]), ln3_g[l:l + 1], ln3_b[l:l + 1], tm=tm)
    return h.reshape(batch, seq, d)
```

```python
import functools

import jax
import jax.numpy as jnp
from jax import lax
from jax.experimental import pallas as pl
from jax.experimental.pallas import tpu as pltpu

D_MODEL = 1024
DEPTH = 1
N_Q_HEADS = 16
N_KV_HEADS = 4
HEAD_DIM = 64
GQA_GROUP = N_Q_HEADS // N_KV_HEADS
WINDOW = 128
ATTN_BLOCK = 128
ROPE_THETA = 500000.0
ROPE_DIM = HEAD_DIM // 4
ROPE_HALF = ROPE_DIM // 2
HGRN_HEADS = 8
HGRN_DK = 128
HGRN_DV = 128
HGRN_CHUNK = 64
D_FF = 2816
ATTN_WIDTH = N_Q_HEADS * HEAD_DIM
KV_WIDTH = N_KV_HEADS * HEAD_DIM
HGRN_WIDTH = HGRN_HEADS * HGRN_DK
D_IN = ATTN_WIDTH + 2 * KV_WIDTH + 4 * HGRN_WIDTH + 2 * D_MODEL
DEEPNORM_ALPHA = (2 * DEPTH) ** 0.25
LN_EPS = 1e-5
RMS_EPS = 1e-6
NEG_INF = -1e30
LOG2_E = 1.4426950408889634

LANES = 128
F32_SUBLANES = 8
VMEM_LIMIT_BYTES = 56 * 1024 * 1024

F32 = jnp.float32
BF16 = jnp.bfloat16

assert WINDOW == ATTN_BLOCK and ROPE_HALF == F32_SUBLANES


def _layer_norm(y, g, b):
    mu = jnp.mean(y, axis=-1, keepdims=True)
    yc = y - mu
    var = jnp.mean(yc * yc, axis=-1, keepdims=True)
    return yc * lax.rsqrt(var + LN_EPS) * g + b


def _resident(shape):
    return pl.BlockSpec(shape, lambda *_: (0,) * len(shape), pipeline_mode=pl.Buffered(1))


def _params(*semantics):
    return pltpu.CompilerParams(dimension_semantics=semantics, vmem_limit_bytes=VMEM_LIMIT_BYTES)


LN_SLABS = 8


def _zero_after(x):
    bits = pltpu.bitcast(x, jnp.uint32)
    tiles = [bits[r:r + F32_SUBLANES, c:c + LANES]
             for r in range(0, x.shape[0], F32_SUBLANES) for c in range(0, x.shape[1], LANES)]
    folded = functools.reduce(jnp.bitwise_or, tiles)
    return pltpu.bitcast((folded >> 16) >> 16, F32)


def _ffn_ln_body(x_ref, w1_ref, w3_ref, w2_ref, g_ref, b_ref, o_ref, acc_ref, y_ref, *, ff_chunk, tiles):
    step = pl.program_id(0)

    @pl.when(step == 0)
    def _():
        y_ref[...] = jnp.zeros_like(y_ref)

    @pl.when(step == tiles)
    def _():
        o_ref[...] = _layer_norm(y_ref[...], g_ref[...], b_ref[...])

    @pl.when(step < tiles)
    def _():
        tm = x_ref.shape[0]
        n_slabs, slab_rows = LN_SLABS, tm // LN_SLABS
        n_chunks = D_FF // ff_chunk
        assert n_slabs <= n_chunks - 1
        x = x_ref[...]
        xb = x.astype(BF16)
        for c in range(n_chunks):
            cols = slice(c * ff_chunk, (c + 1) * ff_chunk)
            h1 = jnp.dot(xb, w1_ref[:, cols], preferred_element_type=F32)
            h3 = jnp.dot(xb, w3_ref[:, cols], preferred_element_type=F32)
            a = ((h1 * jnp.tanh(h1) + h1) * h3).astype(BF16)
            part = jnp.dot(a, w2_ref[cols, :], preferred_element_type=F32)
            if c == 0:
                acc_ref[...] = DEEPNORM_ALPHA * x + part
            elif c < n_chunks - 1:
                acc_ref[...] += part
            else:
                y_ref[...] = acc_ref[...] + part
            if c < n_slabs:
                rows = slice(c * slab_rows, (c + 1) * slab_rows)
                ln = _layer_norm(y_ref[rows, :], g_ref[...], b_ref[...])
                o_ref[rows, :] = ln
                acc_ref[0:F32_SUBLANES, 0:LANES] += _zero_after(ln)


def _ffn_ln(x2d, w1, w3, w2, g, b, *, tm, ff_chunk=256):
    n = x2d.shape[0]
    tiles = n // tm
    return pl.pallas_call(
        functools.partial(_ffn_ln_body, ff_chunk=ff_chunk, tiles=tiles),
        grid=(tiles + 1,),
        in_specs=[pl.BlockSpec((tm, D_MODEL), lambda i: (jnp.minimum(i, tiles - 1), 0)),
                  _resident((D_MODEL, D_FF)), _resident((D_MODEL, D_FF)), _resident((D_FF, D_MODEL)),
                  _resident((1, D_MODEL)), _resident((1, D_MODEL))],
        out_specs=pl.BlockSpec((tm, D_MODEL), lambda i: (jnp.maximum(i - 1, 0), 0)),
        out_shape=jax.ShapeDtypeStruct((n, D_MODEL), F32),
        scratch_shapes=[pltpu.VMEM((tm, D_MODEL), F32), pltpu.VMEM((tm, D_MODEL), F32)],
        compiler_params=_params("arbitrary"),
        name="ffn_ln",
    )(x2d, w1, w3, w2, g, b)


PROJ_CHUNK = 512
QV_ROWS = ATTN_WIDTH + KV_WIDTH
QV_CHUNK = 256
TOK_WIDTH = 4 * HGRN_WIDTH + 2 * D_MODEL + KV_WIDTH


def _rope_lanes(t, c, s_lo, s_hi):
    return t * c + pltpu.roll(t, ROPE_HALF, axis=1) * s_hi + pltpu.roll(t, LANES - ROPE_HALF, axis=1) * s_lo


def _in_proj_body(x_ref, wt_ref, bt_ref, w_ref, b_ref, fa_ref, fb_ref, c_ref, slo_ref, shi_ref, cos_ref, sin_ref, ng_ref,
                  qt_ref, vt_ref, k_ref, gate_ref, yh_ref,
                  f_s, q_s, v_s, og_s, state_ref, qd_ref, ke_ref, sc_ref, dec_ref, *, tiles_per_seq):
    @pl.when(pl.program_id(0) % tiles_per_seq == 0)
    def _():
        state_ref[...] = jnp.zeros_like(state_ref)

    xb = x_ref[...].astype(BF16)
    tm = xb.shape[0]
    q_scale = HEAD_DIM ** -0.5 * LOG2_E

    def feature_major(j):
        cos_t, sin_t = cos_ref[...], sin_ref[...]
        rows = slice(j * QV_CHUNK, (j + 1) * QV_CHUNK)
        pt = lax.dot_general(wt_ref[rows, :], xb, (((1,), (1,)), ((), ())), preferred_element_type=F32)
        pt = pt + bt_ref[rows, :]
        if j * QV_CHUNK < ATTN_WIDTH:
            for h in range(QV_CHUNK // HEAD_DIM):
                r0 = h * HEAD_DIM
                t1 = pt[r0:r0 + ROPE_HALF]
                t2 = pt[r0 + ROPE_HALF:r0 + ROPE_DIM]
                rot = jnp.concatenate([t1 * cos_t - t2 * sin_t, t2 * cos_t + t1 * sin_t,
                                       pt[r0 + ROPE_DIM:r0 + HEAD_DIM]], axis=0)
                qt_ref[j * QV_CHUNK + r0:j * QV_CHUNK + r0 + HEAD_DIM, :] = (rot * q_scale).astype(BF16)
        else:
            vt_ref[...] = pt.astype(BF16)

    def token_major(lo):
        width = min(PROJ_CHUNK, TOK_WIDTH - lo)
        cols = slice(lo, lo + width)
        p = jnp.dot(xb, w_ref[:, cols], preferred_element_type=F32) + b_ref[:, cols]
        kind, off = _tok_kind(lo)
        dst = slice(off, off + width)
        def per_head(dst_ref, val):
            for u in range(width // HGRN_DK):
                dst_ref[(off + u * HGRN_DK) // HGRN_DK] = val[:, u * HGRN_DK:(u + 1) * HGRN_DK]

        if kind == "forget":
            per_head(f_s, fa_ref[:, dst] + fb_ref[:, dst] * jnp.tanh(p))
        elif kind == "hgrn_q":
            per_head(q_s, (p * jnp.tanh(p) + p).astype(BF16))
        elif kind == "hgrn_v":
            per_head(v_s, p.astype(BF16))
        elif kind == "hgrn_og":
            per_head(og_s, (p * jnp.tanh(p) + p).astype(BF16))
        elif kind in ("gate_a", "gate_h"):
            base = 0 if kind == "gate_a" else D_MODEL
            gate_ref[:, base + off:base + off + width] = (0.5 * jnp.tanh(p) + 0.5).astype(BF16)
        else:
            c, s_lo, s_hi = c_ref[...], slo_ref[...], shi_ref[...]
            for u in range(width // LANES):
                t = p[:, u * LANES:(u + 1) * LANES]
                k_ref[:, u * LANES:(u + 1) * LANES] = _rope_lanes(t, c, s_lo, s_hi).astype(BF16)

    hgrn_cols = 4 * HGRN_WIDTH
    for lo in range(0, hgrn_cols, PROJ_CHUNK):
        token_major(lo)
    rest = [functools.partial(token_major, lo) for lo in range(hgrn_cols, TOK_WIDTH, PROJ_CHUNK)]
    rest += [functools.partial(feature_major, j) for j in range(QV_ROWS // QV_CHUNK)]

    cz = HGRN_CHUNK
    n_chunks = tm // cz
    ti = lax.broadcasted_iota(jnp.int32, (cz, cz), 0)
    si = lax.broadcasted_iota(jnp.int32, (cz, cz), 1)
    causal = ti >= si
    ng = ng_ref[...]
    nt_dims = (((1,), (1,)), ((), ()))

    def gates(ci):
        rows = slice(ci * cz, (ci + 1) * cz)
        for h in range(HGRN_HEADS):
            lanes = slice(h * HGRN_DK, (h + 1) * HGRN_DK)
            f = f_s[h, rows, :]
            k = 1.0 - f
            decay = _cumprod_rows(f)
            inv = 1.0 / decay
            d_last = decay[cz - 1:cz, :]
            q_dec = (q_s[h, rows, :].astype(F32) * decay).astype(BF16)
            k_inv = (k * inv).astype(BF16)
            sc = lax.dot_general(q_dec, k_inv, nt_dims, preferred_element_type=F32)
            sc_ref[h] = jnp.where(causal, sc, 0.0).astype(BF16)
            qd_ref[h] = q_dec
            ke_ref[h] = (k * (d_last * inv)).astype(BF16)
            dec_ref[:, lanes] = d_last

    def outputs(ci):
        rows = slice(ci * cz, (ci + 1) * cz)
        for h in range(HGRN_HEADS):
            lanes = slice(h * HGRN_DK, (h + 1) * HGRN_DK)
            v = v_s[h, rows, :]
            st = state_ref[h]
            o = jnp.dot(sc_ref[h], v, preferred_element_type=F32)
            o += lax.dot_general(qd_ref[h], st.astype(BF16), nt_dims, preferred_element_type=F32)
            upd = lax.dot_general(v, ke_ref[h], (((0,), (0,)), ((), ())), preferred_element_type=F32)
            state_ref[h] = st * dec_ref[:, lanes] + upd
            o = o * lax.rsqrt(jnp.mean(o * o, axis=-1, keepdims=True) + RMS_EPS) * ng
            yh_ref[rows, lanes] = (o * og_s[h, rows, :].astype(F32)).astype(BF16)

    per_chunk = -(-len(rest) // n_chunks)
    gates(0)
    for ci in range(n_chunks):
        for piece in rest[ci * per_chunk:(ci + 1) * per_chunk]:
            piece()
        outputs(ci)
        if ci + 1 < n_chunks:
            gates(ci + 1)
    for piece in rest[n_chunks * per_chunk:]:
        piece()


_TOK_GROUPS = (("forget", HGRN_WIDTH), ("hgrn_q", HGRN_WIDTH), ("hgrn_v", HGRN_WIDTH), ("hgrn_og", HGRN_WIDTH),
               ("gate_a", D_MODEL), ("gate_h", D_MODEL), ("rope_k", KV_WIDTH))
_TANH_KINDS = ("forget", "hgrn_q", "hgrn_og", "gate_a", "gate_h")


def _tok_kind(col):
    for kind, width in _TOK_GROUPS:
        if col < width:
            return kind, col
        col -= width
    raise ValueError(col)


def _tok_half_scale():
    return jnp.concatenate([jnp.full((w,), 0.5 if kind in _TANH_KINDS else 1.0, F32)
                            for kind, w in _TOK_GROUPS])[None, :]


def _in_proj(x2d, w_in, b_in, lb, norm_g, rope, *, tm, seq):
    n = x2d.shape[0]
    tiles_per_seq = seq // tm
    q_end, k_end, v_end = ATTN_WIDTH, ATTN_WIDTH + KV_WIDTH, ATTN_WIDTH + 2 * KV_WIDTH
    w_t = jnp.concatenate([w_in[:, :q_end], w_in[:, k_end:v_end]], axis=1).T.astype(BF16)
    b_t = jnp.concatenate([b_in[:q_end], b_in[k_end:v_end]])[:, None]
    half = _tok_half_scale()
    w_tok = (jnp.concatenate([w_in[:, v_end:], w_in[:, q_end:k_end]], axis=1) * half).astype(BF16)
    b_tok = jnp.concatenate([b_in[v_end:], b_in[q_end:k_end]])[None, :] * half
    f_a = lb + 0.5 * (1.0 - lb)
    f_b = 0.5 * (1.0 - lb)
    c, s_lo, s_hi, cos_t, sin_t = rope
    row = lambda w: pl.BlockSpec((tm, w), lambda i: (i, 0))
    col = lambda r: pl.BlockSpec((r, tm), lambda i: (0, i))
    tab = pl.BlockSpec((tm, LANES), lambda i: (i % tiles_per_seq, 0))
    tab_t = pl.BlockSpec((ROPE_HALF, tm), lambda i: (0, i % tiles_per_seq))
    return pl.pallas_call(
        functools.partial(_in_proj_body, tiles_per_seq=tiles_per_seq),
        grid=(n // tm,),
        in_specs=[row(D_MODEL), _resident((QV_ROWS, D_MODEL)), _resident((QV_ROWS, 1)),
                  _resident((D_MODEL, TOK_WIDTH)), _resident((1, TOK_WIDTH)),
                  _resident((1, HGRN_WIDTH)), _resident((1, HGRN_WIDTH)), tab, tab, tab, tab_t, tab_t,
                  _resident((1, HGRN_DV))],
        out_specs=[col(ATTN_WIDTH), col(KV_WIDTH), row(KV_WIDTH), row(2 * D_MODEL), row(HGRN_WIDTH)],
        out_shape=[jax.ShapeDtypeStruct((ATTN_WIDTH, n), BF16),
                   jax.ShapeDtypeStruct((KV_WIDTH, n), BF16),
                   jax.ShapeDtypeStruct((n, KV_WIDTH), BF16),
                   jax.ShapeDtypeStruct((n, 2 * D_MODEL), BF16),
                   jax.ShapeDtypeStruct((n, HGRN_WIDTH), BF16)],
        scratch_shapes=[pltpu.VMEM((HGRN_HEADS, tm, HGRN_DK), F32),
                        pltpu.VMEM((HGRN_HEADS, tm, HGRN_DK), BF16),
                        pltpu.VMEM((HGRN_HEADS, tm, HGRN_DV), BF16),
                        pltpu.VMEM((HGRN_HEADS, tm, HGRN_DV), BF16),
                        pltpu.VMEM((HGRN_HEADS, HGRN_DV, HGRN_DK), F32),
                        pltpu.VMEM((HGRN_HEADS, HGRN_CHUNK, HGRN_DK), BF16),
                        pltpu.VMEM((HGRN_HEADS, HGRN_CHUNK, HGRN_DK), BF16),
                        pltpu.VMEM((HGRN_HEADS, HGRN_CHUNK, HGRN_CHUNK), BF16),
                        pltpu.VMEM((1, HGRN_WIDTH), F32)],
        compiler_params=_params("arbitrary"),
        name="in_proj",
    )(x2d, w_t, b_t, w_tok, b_tok, f_a, f_b, c, s_lo, s_hi, cos_t, sin_t, norm_g)


def _rope_tables(seq):
    pos = jnp.arange(seq, dtype=F32)
    inv_freq = ROPE_THETA ** (-jnp.arange(0, ROPE_DIM, 2, dtype=F32) / ROPE_DIM)
    ang = pos[:, None] * inv_freq[None, :]
    cos, sin = jnp.cos(ang), jnp.sin(ang)
    ones = jnp.ones((seq, HEAD_DIM - ROPE_DIM), F32)
    c = jnp.concatenate([cos, cos, ones], axis=1)
    s_lo = jnp.concatenate([-sin, jnp.zeros((seq, HEAD_DIM - ROPE_HALF), F32)], axis=1)
    s_hi = jnp.concatenate([jnp.zeros((seq, ROPE_HALF), F32), sin, 0.0 * ones], axis=1)
    rep = LANES // HEAD_DIM
    return tuple(jnp.tile(t, (1, rep)) for t in (c, s_lo, s_hi)) + (cos.T, sin.T)


ATTN_TILE_BLOCKS = 8
MIX_CHUNK = 256


def _attn_mix_body(sink_ref, x_ref, qt_ref, k_ref, kh_ref, vt_ref, vth_ref, yh_ref, ga_ref, gh_ref,
                   wpa_ref, wph_ref, wo_ref, g_ref, b_ref, o_ref, y_ref, yat_ref, *, tiles, tiles_per_seq):
    step = pl.program_id(0)

    @pl.when(step == 0)
    def _():
        y_ref[...] = jnp.zeros_like(y_ref)

    @pl.when(step == tiles)
    def _():
        o_ref[...] = _layer_norm(y_ref[...], g_ref[...], b_ref[...])

    @pl.when(step < tiles)
    def _():
        _attn_mix_tile(sink_ref, x_ref, qt_ref, k_ref, kh_ref, vt_ref, vth_ref, yh_ref, ga_ref, gh_ref,
                       wpa_ref, wph_ref, wo_ref, g_ref, b_ref, o_ref, y_ref, yat_ref,
                       first_tile=step % tiles_per_seq == 0)


def _attn_mix_tile(sink_ref, x_ref, qt_ref, k_ref, kh_ref, vt_ref, vth_ref, yh_ref, ga_ref, gh_ref,
                   wpa_ref, wph_ref, wo_ref, g_ref, b_ref, o_ref, y_ref, yat_ref, *, first_tile):
    halo_bias = jnp.where(first_tile, NEG_INF, 0.0).astype(F32)
    wide = GQA_GROUP * ATTN_BLOCK
    key_i = lax.broadcasted_iota(jnp.int32, (ATTN_BLOCK, wide), 0)
    qry_i = lax.broadcasted_iota(jnp.int32, (ATTN_BLOCK, wide), 1) % ATTN_BLOCK
    from_prev = key_i > qry_i
    keep_prev = from_prev.astype(BF16)
    keep_cur = 1.0 - keep_prev
    zero_half = jnp.zeros((HEAD_DIM, ATTN_BLOCK), BF16)
    units = [(qb, kh) for qb in range(ATTN_TILE_BLOCKS) for kh in range(N_KV_HEADS)]

    def heads_of(kh):
        return [kh * GQA_GROUP + g for g in range(GQA_GROUP)]

    def toks_of(qb):
        return slice(qb * ATTN_BLOCK, (qb + 1) * ATTN_BLOCK)

    def scores(qb, kh):
        toks = toks_of(qb)
        k_prev = kh_ref[...] if qb == 0 else k_ref[(qb - 1) * ATTN_BLOCK:qb * ATTN_BLOCK, :]
        keys = jnp.concatenate([k_prev, k_ref[toks, :]], axis=0)
        pair = slice((kh // 2) * LANES, (kh // 2 + 1) * LANES)
        cols = []
        for h in heads_of(kh):
            qh = qt_ref[h * HEAD_DIM:(h + 1) * HEAD_DIM, toks]
            cols.append(jnp.concatenate([qh, zero_half] if kh % 2 == 0 else [zero_half, qh], axis=0))
        q_rhs = jnp.concatenate(cols, axis=1)
        return jnp.dot(keys[:, pair], q_rhs, preferred_element_type=F32)

    def finish(qb, kh, s):
        toks = toks_of(qb)
        heads = heads_of(kh)
        s_prev = s[:ATTN_BLOCK]
        if qb == 0:
            s_prev = s_prev + halo_bias
        c = jnp.where(from_prev, s_prev, s[ATTN_BLOCK:])
        sink = jnp.concatenate([jnp.full((1, ATTN_BLOCK), sink_ref[h] * LOG2_E, F32) for h in heads], axis=1)
        m = jnp.maximum(jnp.max(c, axis=0, keepdims=True), sink)
        p = jnp.exp2(c - m)
        denom = jnp.sum(p, axis=0, keepdims=True) + jnp.exp2(sink - m)
        pb = p.astype(BF16)
        p_both = jnp.concatenate([pb * keep_prev, pb * keep_cur], axis=0)
        vt_prev = vth_ref[...] if qb == 0 else vt_ref[:, (qb - 1) * ATTN_BLOCK:qb * ATTN_BLOCK]
        vt_both = jnp.concatenate([vt_prev[kh * HEAD_DIM:(kh + 1) * HEAD_DIM, :],
                                   vt_ref[kh * HEAD_DIM:(kh + 1) * HEAD_DIM, toks]], axis=1)
        ot = jnp.dot(vt_both, p_both, preferred_element_type=F32) * (1.0 / denom)
        for g, h in enumerate(heads):
            yat_ref[h * HEAD_DIM:(h + 1) * HEAD_DIM, toks] = ot[:, g * ATTN_BLOCK:(g + 1) * ATTN_BLOCK].astype(BF16)

    tm = x_ref.shape[0]
    n_chunks = D_MODEL // MIX_CHUNK
    chunk_cols = [slice(c * MIX_CHUNK, (c + 1) * MIX_CHUNK) for c in range(n_chunks)]
    yh = yh_ref[...]

    ph = []
    units_per_chunk = len(units) // n_chunks
    s_next = scores(*units[0])
    for i, unit in enumerate(units):
        s_cur = s_next
        if i + 1 < len(units):
            s_next = scores(*units[i + 1])
        if i % units_per_chunk == 0:
            ph.append(jnp.dot(yh, wph_ref[:, chunk_cols[i // units_per_chunk]], preferred_element_type=F32))
        finish(*unit, s_cur)

    slab_rows = tm // LN_SLABS
    slabs_per_chunk = LN_SLABS // n_chunks
    yat = yat_ref[...]
    merged = []
    for c, cols in enumerate(chunk_cols):
        pa = lax.dot_general(yat, wpa_ref[:, cols], (((0,), (0,)), ((), ())), preferred_element_type=F32)
        m = ga_ref[:, cols].astype(F32) * pa + gh_ref[:, cols].astype(F32) * ph[c]
        zero = jnp.zeros((F32_SUBLANES, LANES), F32)
        for sl in range(c * slabs_per_chunk, (c + 1) * slabs_per_chunk):
            rows = slice(sl * slab_rows, (sl + 1) * slab_rows)
            ln = _layer_norm(y_ref[rows, :], g_ref[...], b_ref[...])
            o_ref[rows, :] = ln
            zero = zero + _zero_after(ln)
        top = m[0:F32_SUBLANES] + jnp.concatenate([zero] * (MIX_CHUNK // LANES), axis=1)
        merged.append(jnp.concatenate([top, m[F32_SUBLANES:]], axis=0).astype(BF16))
    merged = jnp.concatenate(merged, axis=1)
    for cols in chunk_cols:
        mix = jnp.dot(merged, wo_ref[:, cols], preferred_element_type=F32)
        y_ref[:, cols] = DEEPNORM_ALPHA * x_ref[:, cols] + mix


def _attn_mix(sinks, x2d, qt, k, vt, y_hgrn, gates, w_pa, w_ph, w_out, g, b, *, seq):
    n = x2d.shape[0]
    tm = ATTN_TILE_BLOCKS * ATTN_BLOCK
    tiles = n // tm
    cur = lambda i: jnp.minimum(i, tiles - 1)
    halo = lambda i: jnp.maximum(cur(i) * ATTN_TILE_BLOCKS - 1, 0)
    row = lambda w, col=0: pl.BlockSpec((tm, w), lambda i: (cur(i), col))
    sq = _resident((D_MODEL, D_MODEL))
    return pl.pallas_call(
        functools.partial(_attn_mix_body, tiles=tiles, tiles_per_seq=seq // tm),
        grid=(tiles + 1,),
        in_specs=[pl.BlockSpec(memory_space=pltpu.SMEM),
                  row(D_MODEL),
                  pl.BlockSpec((ATTN_WIDTH, tm), lambda i: (0, cur(i))),
                  row(KV_WIDTH),
                  pl.BlockSpec((ATTN_BLOCK, KV_WIDTH), lambda i: (halo(i), 0)),
                  pl.BlockSpec((KV_WIDTH, tm), lambda i: (0, cur(i))),
                  pl.BlockSpec((KV_WIDTH, ATTN_BLOCK), lambda i: (0, halo(i))),
                  row(HGRN_WIDTH), row(D_MODEL, 0), row(D_MODEL, 1), sq, sq, sq,
                  _resident((1, D_MODEL)), _resident((1, D_MODEL))],
        out_specs=pl.BlockSpec((tm, D_MODEL), lambda i: (jnp.maximum(i - 1, 0), 0)),
        out_shape=jax.ShapeDtypeStruct((n, D_MODEL), F32),
        scratch_shapes=[pltpu.VMEM((tm, D_MODEL), F32),
                        pltpu.VMEM((ATTN_WIDTH, tm), BF16)],
        compiler_params=_params("arbitrary"),
        name="attn_mix",
    )(sinks, x2d, qt, k, k, vt, vt, y_hgrn, gates, gates, w_pa, w_ph, w_out, g, b)


def _cumprod_rows(f):
    row = lax.broadcasted_iota(jnp.int32, (F32_SUBLANES, LANES), 0)
    blocks, carry = [], None
    for j in range(f.shape[0] // F32_SUBLANES):
        x = f[j * F32_SUBLANES:(j + 1) * F32_SUBLANES]
        shift = 1
        while shift < F32_SUBLANES:
            x = x * jnp.where(row >= shift, pltpu.roll(x, shift, axis=0), 1.0)
            shift *= 2
        if carry is not None:
            x = x * carry
        carry = x[F32_SUBLANES - 1:F32_SUBLANES, :]
        blocks.append(x)
    return jnp.concatenate(blocks, axis=0)


def _tile(n, want):
    t = min(n, want)
    assert n % t == 0, (n, t)
    return t


def kernel(x, ln1_g, ln1_b, ffn1_w1, ffn1_w3, ffn1_w2, ln2_g, ln2_b, w_in, b_in, attn_sinks, hgrn_lb_logits,
           hgrn_norm_g, w_proj_attn, w_proj_hgrn, w_out, ln3_g, ln3_b, ffn2_w1, ffn2_w3, ffn2_w2):
    batch, seq, d = x.shape
    assert d == D_MODEL and seq % (ATTN_TILE_BLOCKS * ATTN_BLOCK) == 0
    n = batch * seq
    tm = _tile(seq, 512)
    lb_all = jnp.cumsum(jax.nn.softmax(hgrn_lb_logits.astype(F32), axis=0), axis=0)
    rope = _rope_tables(seq)
    bf = lambda w: w.astype(BF16)
    h = x.reshape(n, d)
    for l in range(DEPTH):
        h = _ffn_ln(h, bf(0.5 * ffn1_w1[l]), bf(ffn1_w3[l]), bf(0.5 * ffn1_w2[l]), ln1_g[l:l + 1], ln1_b[l:l + 1], tm=tm)
        qt, vt, k, gates, y_hgrn = _in_proj(h, w_in[l], b_in[l], lb_all[l:l + 1], hgrn_norm_g[l:l + 1], rope,
                                            tm=tm, seq=seq)
        h = _attn_mix(attn_sinks[l], h, qt, k, vt, y_hgrn, gates, bf(w_proj_attn[l]), bf(w_proj_hgrn[l]),
                      bf(w_out[l]), ln2_g[l:l + 1], ln2_b[l:l + 1], seq=seq)
        h = _ffn_ln(h, bf(0.5 * ffn2_w1[l]), bf(ffn2_w3[l]), bf(0.5 * ffn2_w2[l]), ln3_g[l:l + 1], ln3_b[l:l + 1], tm=tm)
    return h.reshape(batch, seq, d)
```

```python
import functools

import jax
import jax.numpy as jnp
from jax import lax
from jax.experimental import pallas as pl
from jax.experimental.pallas import tpu as pltpu

D_MODEL = 1024
DEPTH = 1
N_Q_HEADS = 16
N_KV_HEADS = 4
HEAD_DIM = 64
GQA_GROUP = N_Q_HEADS // N_KV_HEADS
WINDOW = 128
ATTN_BLOCK = 128
ROPE_THETA = 500000.0
ROPE_DIM = HEAD_DIM // 4
ROPE_HALF = ROPE_DIM // 2
HGRN_HEADS = 8
HGRN_DK = 128
HGRN_DV = 128
HGRN_CHUNK = 64
D_FF = 2816
ATTN_WIDTH = N_Q_HEADS * HEAD_DIM
KV_WIDTH = N_KV_HEADS * HEAD_DIM
HGRN_WIDTH = HGRN_HEADS * HGRN_DK
D_IN = ATTN_WIDTH + 2 * KV_WIDTH + 4 * HGRN_WIDTH + 2 * D_MODEL
DEEPNORM_ALPHA = (2 * DEPTH) ** 0.25
LN_EPS = 1e-5
RMS_EPS = 1e-6
NEG_INF = -1e30
LOG2_E = 1.4426950408889634

LANES = 128
F32_SUBLANES = 8
VMEM_LIMIT_BYTES = 56 * 1024 * 1024

F32 = jnp.float32
BF16 = jnp.bfloat16

assert WINDOW == ATTN_BLOCK and ROPE_HALF == F32_SUBLANES


def _layer_norm(y, g, b):
    mu = jnp.mean(y, axis=-1, keepdims=True)
    yc = y - mu
    var = jnp.mean(yc * yc, axis=-1, keepdims=True)
    return yc * lax.rsqrt(var + LN_EPS) * g + b


def _resident(shape):
    return pl.BlockSpec(shape, lambda *_: (0,) * len(shape), pipeline_mode=pl.Buffered(1))


def _params(*semantics):
    return pltpu.CompilerParams(dimension_semantics=semantics, vmem_limit_bytes=VMEM_LIMIT_BYTES)


LN_SLABS = 8


def _zero_after(x):
    bits = pltpu.bitcast(x, jnp.uint32)
    tiles = [bits[r:r + F32_SUBLANES, c:c + LANES]
             for r in range(0, x.shape[0], F32_SUBLANES) for c in range(0, x.shape[1], LANES)]
    folded = functools.reduce(jnp.bitwise_or, tiles)
    return pltpu.bitcast((folded >> 16) >> 16, F32)


def _ffn_ln_body(x_ref, w1_ref, w3_ref, w2_ref, g_ref, b_ref, o_ref, acc_ref, y_ref, *, ff_chunk, tiles):
    step = pl.program_id(0)

    @pl.when(step == 0)
    def _():
        y_ref[...] = jnp.zeros_like(y_ref)

    @pl.when(step == tiles)
    def _():
        o_ref[...] = _layer_norm(y_ref[...], g_ref[...], b_ref[...])

    @pl.when(step < tiles)
    def _():
        tm = x_ref.shape[0]
        n_slabs, slab_rows = LN_SLABS, tm // LN_SLABS
        n_chunks = D_FF // ff_chunk
        assert n_slabs <= n_chunks - 1
        x = x_ref[...]
        xb = x.astype(BF16)
        for c in range(n_chunks):
            cols = slice(c * ff_chunk, (c + 1) * ff_chunk)
            h1 = jnp.dot(xb, w1_ref[:, cols], preferred_element_type=F32)
            h3 = jnp.dot(xb, w3_ref[:, cols], preferred_element_type=F32)
            a = ((h1 * jnp.tanh(h1) + h1) * h3).astype(BF16)
            part = jnp.dot(a, w2_ref[cols, :], preferred_element_type=F32)
            if c == 0:
                acc_ref[...] = DEEPNORM_ALPHA * x + part
            elif c < n_chunks - 1:
                acc_ref[...] += part
            else:
                y_ref[...] = acc_ref[...] + part
            if c < n_slabs:
                rows = slice(c * slab_rows, (c + 1) * slab_rows)
                ln = _layer_norm(y_ref[rows, :], g_ref[...], b_ref[...])
                o_ref[rows, :] = ln
                acc_ref[0:F32_SUBLANES, 0:LANES] += _zero_after(ln)


def _ffn_ln(x2d, w1, w3, w2, g, b, *, tm, ff_chunk=256):
    n = x2d.shape[0]
    tiles = n // tm
    return pl.pallas_call(
        functools.partial(_ffn_ln_body, ff_chunk=ff_chunk, tiles=tiles),
        grid=(tiles + 1,),
        in_specs=[pl.BlockSpec((tm, D_MODEL), lambda i: (jnp.minimum(i, tiles - 1), 0)),
                  _resident((D_MODEL, D_FF)), _resident((D_MODEL, D_FF)), _resident((D_FF, D_MODEL)),
                  _resident((1, D_MODEL)), _resident((1, D_MODEL))],
        out_specs=pl.BlockSpec((tm, D_MODEL), lambda i: (jnp.maximum(i - 1, 0), 0)),
        out_shape=jax.ShapeDtypeStruct((n, D_MODEL), F32),
        scratch_shapes=[pltpu.VMEM((tm, D_MODEL), F32), pltpu.VMEM((tm, D_MODEL), F32)],
        compiler_params=_params("arbitrary"),
        name="ffn_ln",
    )(x2d, w1, w3, w2, g, b)


PROJ_CHUNK = 512
QV_ROWS = ATTN_WIDTH + KV_WIDTH
QV_CHUNK = 256
TOK_WIDTH = 4 * HGRN_WIDTH + 2 * D_MODEL + KV_WIDTH


def _rope_lanes(t, c, s_lo, s_hi):
    return t * c + pltpu.roll(t, ROPE_HALF, axis=1) * s_hi + pltpu.roll(t, LANES - ROPE_HALF, axis=1) * s_lo


def _in_proj_body(x_ref, wt_ref, bt_ref, w_ref, b_ref, fa_ref, fb_ref, c_ref, slo_ref, shi_ref, cos_ref, sin_ref, ng_ref,
                  qt_ref, vt_ref, k_ref, gate_ref, yh_ref,
                  f_s, q_s, v_s, og_s, state_ref, qd_ref, ke_ref, sc_ref, dec_ref, *, tiles_per_seq):
    @pl.when(pl.program_id(0) % tiles_per_seq == 0)
    def _():
        state_ref[...] = jnp.zeros_like(state_ref)

    xb = x_ref[...].astype(BF16)
    tm = xb.shape[0]
    q_scale = HEAD_DIM ** -0.5 * LOG2_E

    def feature_major(j):
        cos_t, sin_t = cos_ref[...], sin_ref[...]
        rows = slice(j * QV_CHUNK, (j + 1) * QV_CHUNK)
        pt = lax.dot_general(wt_ref[rows, :], xb, (((1,), (1,)), ((), ())), preferred_element_type=F32)
        pt = pt + bt_ref[rows, :]
        if j * QV_CHUNK < ATTN_WIDTH:
            for h in range(QV_CHUNK // HEAD_DIM):
                r0 = h * HEAD_DIM
                t1 = pt[r0:r0 + ROPE_HALF]
                t2 = pt[r0 + ROPE_HALF:r0 + ROPE_DIM]
                rot = jnp.concatenate([t1 * cos_t - t2 * sin_t, t2 * cos_t + t1 * sin_t,
                                       pt[r0 + ROPE_DIM:r0 + HEAD_DIM]], axis=0)
                qt_ref[j * QV_CHUNK + r0:j * QV_CHUNK + r0 + HEAD_DIM, :] = (rot * q_scale).astype(BF16)
        else:
            vt_ref[...] = pt.astype(BF16)

    def token_major(lo):
        width = min(PROJ_CHUNK, TOK_WIDTH - lo)
        cols = slice(lo, lo + width)
        p = jnp.dot(xb, w_ref[:, cols], preferred_element_type=F32) + b_ref[:, cols]
        kind, off = _tok_kind(lo)
        dst = slice(off, off + width)
        def per_head(dst_ref, val):
            for u in range(width // HGRN_DK):
                dst_ref[(off + u * HGRN_DK) // HGRN_DK] = val[:, u * HGRN_DK:(u + 1) * HGRN_DK]

        if kind == "forget":
            per_head(f_s, fa_ref[:, dst] + fb_ref[:, dst] * jnp.tanh(p))
        elif kind == "hgrn_q":
            per_head(q_s, (p * jnp.tanh(p) + p).astype(BF16))
        elif kind == "hgrn_v":
            per_head(v_s, p.astype(BF16))
        elif kind == "hgrn_og":
            per_head(og_s, (p * jnp.tanh(p) + p).astype(BF16))
        elif kind in ("gate_a", "gate_h"):
            base = 0 if kind == "gate_a" else D_MODEL
            gate_ref[:, base + off:base + off + width] = (0.5 * jnp.tanh(p) + 0.5).astype(BF16)
        else:
            c, s_lo, s_hi = c_ref[...], slo_ref[...], shi_ref[...]
            for u in range(width // LANES):
                t = p[:, u * LANES:(u + 1) * LANES]
                k_ref[:, u * LANES:(u + 1) * LANES] = _rope_lanes(t, c, s_lo, s_hi).astype(BF16)

    early_cols = 2 * HGRN_WIDTH
    hgrn_cols = 4 * HGRN_WIDTH
    for lo in range(0, early_cols, PROJ_CHUNK):
        token_major(lo)
    first = [functools.partial(token_major, lo) for lo in range(early_cols, hgrn_cols, PROJ_CHUNK)]
    rest = [functools.partial(token_major, lo) for lo in range(hgrn_cols, TOK_WIDTH, PROJ_CHUNK)]
    rest += [functools.partial(feature_major, j) for j in range(QV_ROWS // QV_CHUNK)]

    cz = HGRN_CHUNK
    n_chunks = tm // cz
    ti = lax.broadcasted_iota(jnp.int32, (cz, cz), 0)
    si = lax.broadcasted_iota(jnp.int32, (cz, cz), 1)
    causal = ti >= si
    ng = ng_ref[...]
    nt_dims = (((1,), (1,)), ((), ()))

    def gates(ci):
        rows = slice(ci * cz, (ci + 1) * cz)
        for h in range(HGRN_HEADS):
            lanes = slice(h * HGRN_DK, (h + 1) * HGRN_DK)
            f = f_s[h, rows, :]
            k = 1.0 - f
            decay = _cumprod_rows(f)
            inv = 1.0 / decay
            d_last = decay[cz - 1:cz, :]
            q_dec = (q_s[h, rows, :].astype(F32) * decay).astype(BF16)
            k_inv = (k * inv).astype(BF16)
            sc = lax.dot_general(q_dec, k_inv, nt_dims, preferred_element_type=F32)
            sc_ref[h] = jnp.where(causal, sc, 0.0).astype(BF16)
            qd_ref[h] = q_dec
            ke_ref[h] = (k * (d_last * inv)).astype(BF16)
            dec_ref[:, lanes] = d_last

    def outputs(ci):
        rows = slice(ci * cz, (ci + 1) * cz)
        for h in range(HGRN_HEADS):
            lanes = slice(h * HGRN_DK, (h + 1) * HGRN_DK)
            v = v_s[h, rows, :]
            st = state_ref[h]
            o = jnp.dot(sc_ref[h], v, preferred_element_type=F32)
            o += lax.dot_general(qd_ref[h], st.astype(BF16), nt_dims, preferred_element_type=F32)
            upd = lax.dot_general(v, ke_ref[h], (((0,), (0,)), ((), ())), preferred_element_type=F32)
            state_ref[h] = st * dec_ref[:, lanes] + upd
            o = o * lax.rsqrt(jnp.mean(o * o, axis=-1, keepdims=True) + RMS_EPS) * ng
            yh_ref[rows, lanes] = (o * og_s[h, rows, :].astype(F32)).astype(BF16)

    gates(0)
    for piece in first:
        piece()
    dealt = 0
    for ci in range(n_chunks):
        outputs(ci)
        if ci + 1 < n_chunks:
            gates(ci + 1)
        while dealt < len(rest) and dealt * n_chunks < (ci + 1) * len(rest):
            rest[dealt]()
            dealt += 1
    assert dealt == len(rest)


_TOK_GROUPS = (("forget", HGRN_WIDTH), ("hgrn_q", HGRN_WIDTH), ("hgrn_v", HGRN_WIDTH), ("hgrn_og", HGRN_WIDTH),
               ("gate_a", D_MODEL), ("gate_h", D_MODEL), ("rope_k", KV_WIDTH))
_TANH_KINDS = ("forget", "hgrn_q", "hgrn_og", "gate_a", "gate_h")


def _tok_kind(col):
    for kind, width in _TOK_GROUPS:
        if col < width:
            return kind, col
        col -= width
    raise ValueError(col)


def _tok_half_scale():
    return jnp.concatenate([jnp.full((w,), 0.5 if kind in _TANH_KINDS else 1.0, F32)
                            for kind, w in _TOK_GROUPS])[None, :]


def _in_proj(x2d, w_in, b_in, lb, norm_g, rope, *, tm, seq):
    n = x2d.shape[0]
    tiles_per_seq = seq // tm
    q_end, k_end, v_end = ATTN_WIDTH, ATTN_WIDTH + KV_WIDTH, ATTN_WIDTH + 2 * KV_WIDTH
    w_t = jnp.concatenate([w_in[:, :q_end], w_in[:, k_end:v_end]], axis=1).T.astype(BF16)
    b_t = jnp.concatenate([b_in[:q_end], b_in[k_end:v_end]])[:, None]
    half = _tok_half_scale()
    w_tok = (jnp.concatenate([w_in[:, v_end:], w_in[:, q_end:k_end]], axis=1) * half).astype(BF16)
    b_tok = jnp.concatenate([b_in[v_end:], b_in[q_end:k_end]])[None, :] * half
    f_a = lb + 0.5 * (1.0 - lb)
    f_b = 0.5 * (1.0 - lb)
    c, s_lo, s_hi, cos_t, sin_t = rope
    row = lambda w: pl.BlockSpec((tm, w), lambda i: (i, 0))
    col = lambda r: pl.BlockSpec((r, tm), lambda i: (0, i))
    tab = pl.BlockSpec((tm, LANES), lambda i: (i % tiles_per_seq, 0))
    tab_t = pl.BlockSpec((ROPE_HALF, tm), lambda i: (0, i % tiles_per_seq))
    return pl.pallas_call(
        functools.partial(_in_proj_body, tiles_per_seq=tiles_per_seq),
        grid=(n // tm,),
        in_specs=[row(D_MODEL), _resident((QV_ROWS, D_MODEL)), _resident((QV_ROWS, 1)),
                  _resident((D_MODEL, TOK_WIDTH)), _resident((1, TOK_WIDTH)),
                  _resident((1, HGRN_WIDTH)), _resident((1, HGRN_WIDTH)), tab, tab, tab, tab_t, tab_t,
                  _resident((1, HGRN_DV))],
        out_specs=[col(ATTN_WIDTH), col(KV_WIDTH), row(KV_WIDTH), row(2 * D_MODEL), row(HGRN_WIDTH)],
        out_shape=[jax.ShapeDtypeStruct((ATTN_WIDTH, n), BF16),
                   jax.ShapeDtypeStruct((KV_WIDTH, n), BF16),
                   jax.ShapeDtypeStruct((n, KV_WIDTH), BF16),
                   jax.ShapeDtypeStruct((n, 2 * D_MODEL), BF16),
                   jax.ShapeDtypeStruct((n, HGRN_WIDTH), BF16)],
        scratch_shapes=[pltpu.VMEM((HGRN_HEADS, tm, HGRN_DK), F32),
                        pltpu.VMEM((HGRN_HEADS, tm, HGRN_DK), BF16),
                        pltpu.VMEM((HGRN_HEADS, tm, HGRN_DV), BF16),
                        pltpu.VMEM((HGRN_HEADS, tm, HGRN_DV), BF16),
                        pltpu.VMEM((HGRN_HEADS, HGRN_DV, HGRN_DK), F32),
                        pltpu.VMEM((HGRN_HEADS, HGRN_CHUNK, HGRN_DK), BF16),
                        pltpu.VMEM((HGRN_HEADS, HGRN_CHUNK, HGRN_DK), BF16),
                        pltpu.VMEM((HGRN_HEADS, HGRN_CHUNK, HGRN_CHUNK), BF16),
                        pltpu.VMEM((1, HGRN_WIDTH), F32)],
        compiler_params=_params("arbitrary"),
        name="in_proj",
    )(x2d, w_t, b_t, w_tok, b_tok, f_a, f_b, c, s_lo, s_hi, cos_t, sin_t, norm_g)


def _rope_tables(seq):
    pos = jnp.arange(seq, dtype=F32)
    inv_freq = ROPE_THETA ** (-jnp.arange(0, ROPE_DIM, 2, dtype=F32) / ROPE_DIM)
    ang = pos[:, None] * inv_freq[None, :]
    cos, sin = jnp.cos(ang), jnp.sin(ang)
    ones = jnp.ones((seq, HEAD_DIM - ROPE_DIM), F32)
    c = jnp.concatenate([cos, cos, ones], axis=1)
    s_lo = jnp.concatenate([-sin, jnp.zeros((seq, HEAD_DIM - ROPE_HALF), F32)], axis=1)
    s_hi = jnp.concatenate([jnp.zeros((seq, ROPE_HALF), F32), sin, 0.0 * ones], axis=1)
    rep = LANES // HEAD_DIM
    return tuple(jnp.tile(t, (1, rep)) for t in (c, s_lo, s_hi)) + (cos.T, sin.T)


ATTN_TILE_BLOCKS = 8
MIX_CHUNK = 256


def _attn_mix_body(sink_ref, x_ref, qt_ref, k_ref, kh_ref, vt_ref, vth_ref, yh_ref, ga_ref, gh_ref,
                   wpa_ref, wph_ref, wo_ref, g_ref, b_ref, o_ref, y_ref, yat_ref, *, tiles, tiles_per_seq):
    step = pl.program_id(0)

    @pl.when(step == 0)
    def _():
        y_ref[...] = jnp.zeros_like(y_ref)

    @pl.when(step == tiles)
    def _():
        o_ref[...] = _layer_norm(y_ref[...], g_ref[...], b_ref[...])

    @pl.when(step < tiles)
    def _():
        _attn_mix_tile(sink_ref, x_ref, qt_ref, k_ref, kh_ref, vt_ref, vth_ref, yh_ref, ga_ref, gh_ref,
                       wpa_ref, wph_ref, wo_ref, g_ref, b_ref, o_ref, y_ref, yat_ref,
                       first_tile=step % tiles_per_seq == 0)


def _attn_mix_tile(sink_ref, x_ref, qt_ref, k_ref, kh_ref, vt_ref, vth_ref, yh_ref, ga_ref, gh_ref,
                   wpa_ref, wph_ref, wo_ref, g_ref, b_ref, o_ref, y_ref, yat_ref, *, first_tile):
    halo_bias = jnp.where(first_tile, NEG_INF, 0.0).astype(F32)
    wide = GQA_GROUP * ATTN_BLOCK
    key_i = lax.broadcasted_iota(jnp.int32, (ATTN_BLOCK, wide), 0)
    qry_i = lax.broadcasted_iota(jnp.int32, (ATTN_BLOCK, wide), 1) % ATTN_BLOCK
    from_prev = key_i > qry_i
    keep_prev = from_prev.astype(BF16)
    keep_cur = 1.0 - keep_prev
    zero_half = jnp.zeros((HEAD_DIM, ATTN_BLOCK), BF16)
    units = [(qb, kh) for qb in range(ATTN_TILE_BLOCKS) for kh in range(N_KV_HEADS)]

    def heads_of(kh):
        return [kh * GQA_GROUP + g for g in range(GQA_GROUP)]

    def toks_of(qb):
        return slice(qb * ATTN_BLOCK, (qb + 1) * ATTN_BLOCK)

    def scores(qb, kh):
        toks = toks_of(qb)
        k_prev = kh_ref[...] if qb == 0 else k_ref[(qb - 1) * ATTN_BLOCK:qb * ATTN_BLOCK, :]
        keys = jnp.concatenate([k_prev, k_ref[toks, :]], axis=0)
        pair = slice((kh // 2) * LANES, (kh // 2 + 1) * LANES)
        cols = []
        for h in heads_of(kh):
            qh = qt_ref[h * HEAD_DIM:(h + 1) * HEAD_DIM, toks]
            cols.append(jnp.concatenate([qh, zero_half] if kh % 2 == 0 else [zero_half, qh], axis=0))
        q_rhs = jnp.concatenate(cols, axis=1)
        return jnp.dot(keys[:, pair], q_rhs, preferred_element_type=F32)

    def finish(qb, kh, s):
        toks = toks_of(qb)
        heads = heads_of(kh)
        s_prev = s[:ATTN_BLOCK]
        if qb == 0:
            s_prev = s_prev + halo_bias
        c = jnp.where(from_prev, s_prev, s[ATTN_BLOCK:])
        sink = jnp.concatenate([jnp.full((1, ATTN_BLOCK), sink_ref[h] * LOG2_E, F32) for h in heads], axis=1)
        m = jnp.maximum(jnp.max(c, axis=0, keepdims=True), sink)
        p = jnp.exp2(c - m)
        denom = jnp.sum(p, axis=0, keepdims=True) + jnp.exp2(sink - m)
        pb = p.astype(BF16)
        p_both = jnp.concatenate([pb * keep_prev, pb * keep_cur], axis=0)
        vt_prev = vth_ref[...] if qb == 0 else vt_ref[:, (qb - 1) * ATTN_BLOCK:qb * ATTN_BLOCK]
        vt_both = jnp.concatenate([vt_prev[kh * HEAD_DIM:(kh + 1) * HEAD_DIM, :],
                                   vt_ref[kh * HEAD_DIM:(kh + 1) * HEAD_DIM, toks]], axis=1)
        ot = jnp.dot(vt_both, p_both, preferred_element_type=F32) * (1.0 / denom)
        for g, h in enumerate(heads):
            yat_ref[h * HEAD_DIM:(h + 1) * HEAD_DIM, toks] = ot[:, g * ATTN_BLOCK:(g + 1) * ATTN_BLOCK].astype(BF16)

    tm = x_ref.shape[0]
    n_chunks = D_MODEL // MIX_CHUNK
    chunk_cols = [slice(c * MIX_CHUNK, (c + 1) * MIX_CHUNK) for c in range(n_chunks)]
    yh = yh_ref[...]

    ph = []
    units_per_chunk = len(units) // n_chunks
    s_next = scores(*units[0])
    for i, unit in enumerate(units):
        s_cur = s_next
        if i + 1 < len(units):
            s_next = scores(*units[i + 1])
        if i % units_per_chunk == 0:
            ph.append(jnp.dot(yh, wph_ref[:, chunk_cols[i // units_per_chunk]], preferred_element_type=F32))
        finish(*unit, s_cur)

    slab_rows = tm // LN_SLABS
    slabs_per_chunk = LN_SLABS // n_chunks
    yat = yat_ref[...]
    merged = []
    for c, cols in enumerate(chunk_cols):
        pa = lax.dot_general(yat, wpa_ref[:, cols], (((0,), (0,)), ((), ())), preferred_element_type=F32)
        m = ga_ref[:, cols].astype(F32) * pa + gh_ref[:, cols].astype(F32) * ph[c]
        zero = jnp.zeros((F32_SUBLANES, LANES), F32)
        for sl in range(c * slabs_per_chunk, (c + 1) * slabs_per_chunk):
            rows = slice(sl * slab_rows, (sl + 1) * slab_rows)
            ln = _layer_norm(y_ref[rows, :], g_ref[...], b_ref[...])
            o_ref[rows, :] = ln
            zero = zero + _zero_after(ln)
        top = m[0:F32_SUBLANES] + jnp.concatenate([zero] * (MIX_CHUNK // LANES), axis=1)
        merged.append(jnp.concatenate([top, m[F32_SUBLANES:]], axis=0).astype(BF16))
    merged = jnp.concatenate(merged, axis=1)
    for cols in chunk_cols:
        mix = jnp.dot(merged, wo_ref[:, cols], preferred_element_type=F32)
        y_ref[:, cols] = DEEPNORM_ALPHA * x_ref[:, cols] + mix


def _attn_mix(sinks, x2d, qt, k, vt, y_hgrn, gates, w_pa, w_ph, w_out, g, b, *, seq):
    n = x2d.shape[0]
    tm = ATTN_TILE_BLOCKS * ATTN_BLOCK
    tiles = n // tm
    cur = lambda i: jnp.minimum(i, tiles - 1)
    halo = lambda i: jnp.maximum(cur(i) * ATTN_TILE_BLOCKS - 1, 0)
    row = lambda w, col=0: pl.BlockSpec((tm, w), lambda i: (cur(i), col))
    sq = _resident((D_MODEL, D_MODEL))
    return pl.pallas_call(
        functools.partial(_attn_mix_body, tiles=tiles, tiles_per_seq=seq // tm),
        grid=(tiles + 1,),
        in_specs=[pl.BlockSpec(memory_space=pltpu.SMEM),
                  row(D_MODEL),
                  pl.BlockSpec((ATTN_WIDTH, tm), lambda i: (0, cur(i))),
                  row(KV_WIDTH),
                  pl.BlockSpec((ATTN_BLOCK, KV_WIDTH), lambda i: (halo(i), 0)),
                  pl.BlockSpec((KV_WIDTH, tm), lambda i: (0, cur(i))),
                  pl.BlockSpec((KV_WIDTH, ATTN_BLOCK), lambda i: (0, halo(i))),
                  row(HGRN_WIDTH), row(D_MODEL, 0), row(D_MODEL, 1), sq, sq, sq,
                  _resident((1, D_MODEL)), _resident((1, D_MODEL))],
        out_specs=pl.BlockSpec((tm, D_MODEL), lambda i: (jnp.maximum(i - 1, 0), 0)),
        out_shape=jax.ShapeDtypeStruct((n, D_MODEL), F32),
        scratch_shapes=[pltpu.VMEM((tm, D_MODEL), F32),
                        pltpu.VMEM((ATTN_WIDTH, tm), BF16)],
        compiler_params=_params("arbitrary"),
        name="attn_mix",
    )(sinks, x2d, qt, k, k, vt, vt, y_hgrn, gates, gates, w_pa, w_ph, w_out, g, b)


def _cumprod_rows(f):
    row = lax.broadcasted_iota(jnp.int32, (F32_SUBLANES, LANES), 0)
    blocks, carry = [], None
    for j in range(f.shape[0] // F32_SUBLANES):
        x = f[j * F32_SUBLANES:(j + 1) * F32_SUBLANES]
        shift = 1
        while shift < F32_SUBLANES:
            x = x * jnp.where(row >= shift, pltpu.roll(x, shift, axis=0), 1.0)
            shift *= 2
        if carry is not None:
            x = x * carry
        carry = x[F32_SUBLANES - 1:F32_SUBLANES, :]
        blocks.append(x)
    return jnp.concatenate(blocks, axis=0)


def _tile(n, want):
    t = min(n, want)
    assert n % t == 0, (n, t)
    return t


def kernel(x, ln1_g, ln1_b, ffn1_w1, ffn1_w3, ffn1_w2, ln2_g, ln2_b, w_in, b_in, attn_sinks, hgrn_lb_logits,
           hgrn_norm_g, w_proj_attn, w_proj_hgrn, w_out, ln3_g, ln3_b, ffn2_w1, ffn2_w3, ffn2_w2):
    batch, seq, d = x.shape
    assert d == D_MODEL and seq % (ATTN_TILE_BLOCKS * ATTN_BLOCK) == 0
    n = batch * seq
    tm = _tile(seq, 512)
    lb_all = jnp.cumsum(jax.nn.softmax(hgrn_lb_logits.astype(F32), axis=0), axis=0)
    rope = _rope_tables(seq)
    bf = lambda w: w.astype(BF16)
    h = x.reshape(n, d)
    for l in range(DEPTH):
        h = _ffn_ln(h, bf(0.5 * ffn1_w1[l]), bf(ffn1_w3[l]), bf(0.5 * ffn1_w2[l]), ln1_g[l:l + 1], ln1_b[l:l + 1], tm=tm)
        qt, vt, k, gates, y_hgrn = _in_proj(h, w_in[l], b_in[l], lb_all[l:l + 1], hgrn_norm_g[l:l + 1], rope,
                                            tm=tm, seq=seq)
        h = _attn_mix(attn_sinks[l], h, qt, k, vt, y_hgrn, gates, bf(w_proj_attn[l]), bf(w_proj_hgrn[l]),
                      bf(w_out[l]), ln2_g[l:l + 1], ln2_b[l:l + 1], seq=seq)
        h = _ffn_ln(h, bf(0.5 * ffn2_w1[l]), bf(ffn2_w3[l]), bf(0.5 * ffn2_w2[l]), ln3_g[l:l + 1], ln3_b[l:l + 1], tm=tm)
    return h.reshape(batch, seq, d)
```

```python
import functools

import jax
import jax.numpy as jnp
from jax import lax
from jax.experimental import pallas as pl
from jax.experimental.pallas import tpu as pltpu

D_MODEL = 1024
DEPTH = 1
N_Q_HEADS = 16
N_KV_HEADS = 4
HEAD_DIM = 64
GQA_GROUP = N_Q_HEADS // N_KV_HEADS
WINDOW = 128
ATTN_BLOCK = 128
ROPE_THETA = 500000.0
ROPE_DIM = HEAD_DIM // 4
ROPE_HALF = ROPE_DIM // 2
HGRN_HEADS = 8
HGRN_DK = 128
HGRN_DV = 128
HGRN_CHUNK = 64
D_FF = 2816
ATTN_WIDTH = N_Q_HEADS * HEAD_DIM
KV_WIDTH = N_KV_HEADS * HEAD_DIM
HGRN_WIDTH = HGRN_HEADS * HGRN_DK
D_IN = ATTN_WIDTH + 2 * KV_WIDTH + 4 * HGRN_WIDTH + 2 * D_MODEL
DEEPNORM_ALPHA = (2 * DEPTH) ** 0.25
LN_EPS = 1e-5
RMS_EPS = 1e-6
NEG_INF = -1e30
LOG2_E = 1.4426950408889634

LANES = 128
F32_SUBLANES = 8
VMEM_LIMIT_BYTES = 56 * 1024 * 1024

F32 = jnp.float32
BF16 = jnp.bfloat16

assert WINDOW == ATTN_BLOCK and ROPE_HALF == F32_SUBLANES


def _layer_norm(y, g, b):
    mu = jnp.mean(y, axis=-1, keepdims=True)
    yc = y - mu
    var = jnp.mean(yc * yc, axis=-1, keepdims=True)
    return yc * lax.rsqrt(var + LN_EPS) * g + b


def _resident(shape):
    return pl.BlockSpec(shape, lambda *_: (0,) * len(shape), pipeline_mode=pl.Buffered(1))


def _params(*semantics):
    return pltpu.CompilerParams(dimension_semantics=semantics, vmem_limit_bytes=VMEM_LIMIT_BYTES)


LN_SLABS = 8


def _zero_after(x):
    bits = pltpu.bitcast(x, jnp.uint32)
    tiles = [bits[r:r + F32_SUBLANES, c:c + LANES]
             for r in range(0, x.shape[0], F32_SUBLANES) for c in range(0, x.shape[1], LANES)]
    folded = functools.reduce(jnp.bitwise_or, tiles)
    return pltpu.bitcast((folded >> 16) >> 16, F32)


def _ffn_ln_body(x_ref, w1_ref, w3_ref, w2_ref, g_ref, b_ref, o_ref, acc_ref, y_ref, *, ff_chunk, tiles):
    step = pl.program_id(0)

    @pl.when(step == 0)
    def _():
        y_ref[...] = jnp.zeros_like(y_ref)

    @pl.when(step == tiles)
    def _():
        o_ref[...] = _layer_norm(y_ref[...], g_ref[...], b_ref[...])

    @pl.when(step < tiles)
    def _():
        tm = x_ref.shape[0]
        n_slabs, slab_rows = LN_SLABS, tm // LN_SLABS
        n_chunks = D_FF // ff_chunk
        assert n_slabs <= n_chunks - 1
        x = x_ref[...]
        xb = x.astype(BF16)
        for c in range(n_chunks):
            cols = slice(c * ff_chunk, (c + 1) * ff_chunk)
            h1 = jnp.dot(xb, w1_ref[:, cols], preferred_element_type=F32)
            h3 = jnp.dot(xb, w3_ref[:, cols], preferred_element_type=F32)
            a = ((h1 * jnp.tanh(h1) + h1) * h3).astype(BF16)
            part = jnp.dot(a, w2_ref[cols, :], preferred_element_type=F32)
            if c == 0:
                acc_ref[...] = DEEPNORM_ALPHA * x + part
            elif c < n_chunks - 1:
                acc_ref[...] += part
            else:
                y_ref[...] = acc_ref[...] + part
            if c < n_slabs:
                rows = slice(c * slab_rows, (c + 1) * slab_rows)
                ln = _layer_norm(y_ref[rows, :], g_ref[...], b_ref[...])
                o_ref[rows, :] = ln
                acc_ref[0:F32_SUBLANES, 0:LANES] += _zero_after(ln)


def _ffn_ln(x2d, w1, w3, w2, g, b, *, tm, ff_chunk=256):
    n = x2d.shape[0]
    tiles = n // tm
    return pl.pallas_call(
        functools.partial(_ffn_ln_body, ff_chunk=ff_chunk, tiles=tiles),
        grid=(tiles + 1,),
        in_specs=[pl.BlockSpec((tm, D_MODEL), lambda i: (jnp.minimum(i, tiles - 1), 0)),
                  _resident((D_MODEL, D_FF)), _resident((D_MODEL, D_FF)), _resident((D_FF, D_MODEL)),
                  _resident((1, D_MODEL)), _resident((1, D_MODEL))],
        out_specs=pl.BlockSpec((tm, D_MODEL), lambda i: (jnp.maximum(i - 1, 0), 0)),
        out_shape=jax.ShapeDtypeStruct((n, D_MODEL), F32),
        scratch_shapes=[pltpu.VMEM((tm, D_MODEL), F32), pltpu.VMEM((tm, D_MODEL), F32)],
        compiler_params=_params("arbitrary"),
        name="ffn_ln",
    )(x2d, w1, w3, w2, g, b)


PROJ_CHUNK = 512
QV_ROWS = ATTN_WIDTH + KV_WIDTH
QV_CHUNK = 256
TOK_WIDTH = 4 * HGRN_WIDTH + 2 * D_MODEL + KV_WIDTH


def _rope_lanes(t, c, s_lo, s_hi):
    return t * c + pltpu.roll(t, ROPE_HALF, axis=1) * s_hi + pltpu.roll(t, LANES - ROPE_HALF, axis=1) * s_lo


def _in_proj_body(x_ref, wt_ref, bt_ref, w_ref, b_ref, fa_ref, fb_ref, c_ref, slo_ref, shi_ref, cos_ref, sin_ref, ng_ref,
                  qt_ref, vt_ref, k_ref, gate_ref, yh_ref,
                  f_s, q_s, v_s, og_s, state_ref, qd_ref, upd_ref, sc_ref, dec_ref, *, tiles_per_seq):
    @pl.when(pl.program_id(0) % tiles_per_seq == 0)
    def _():
        state_ref[...] = jnp.zeros_like(state_ref)

    xb = x_ref[...].astype(BF16)
    tm = xb.shape[0]
    q_scale = HEAD_DIM ** -0.5 * LOG2_E

    def feature_major(j):
        cos_t, sin_t = cos_ref[...], sin_ref[...]
        rows = slice(j * QV_CHUNK, (j + 1) * QV_CHUNK)
        pt = lax.dot_general(wt_ref[rows, :], xb, (((1,), (1,)), ((), ())), preferred_element_type=F32)
        pt = pt + bt_ref[rows, :]
        if j * QV_CHUNK < ATTN_WIDTH:
            for h in range(QV_CHUNK // HEAD_DIM):
                r0 = h * HEAD_DIM
                t1 = pt[r0:r0 + ROPE_HALF]
                t2 = pt[r0 + ROPE_HALF:r0 + ROPE_DIM]
                rot = jnp.concatenate([t1 * cos_t - t2 * sin_t, t2 * cos_t + t1 * sin_t,
                                       pt[r0 + ROPE_DIM:r0 + HEAD_DIM]], axis=0)
                qt_ref[j * QV_CHUNK + r0:j * QV_CHUNK + r0 + HEAD_DIM, :] = (rot * q_scale).astype(BF16)
        else:
            vt_ref[...] = pt.astype(BF16)

    def token_major(lo):
        width = min(PROJ_CHUNK, TOK_WIDTH - lo)
        cols = slice(lo, lo + width)
        p = jnp.dot(xb, w_ref[:, cols], preferred_element_type=F32) + b_ref[:, cols]
        kind, off = _tok_kind(lo)
        dst = slice(off, off + width)
        def per_head(dst_ref, val):
            for u in range(width // HGRN_DK):
                dst_ref[(off + u * HGRN_DK) // HGRN_DK] = val[:, u * HGRN_DK:(u + 1) * HGRN_DK]

        if kind == "forget":
            per_head(f_s, fa_ref[:, dst] + fb_ref[:, dst] * jnp.tanh(p))
        elif kind == "hgrn_q":
            per_head(q_s, (p * jnp.tanh(p) + p).astype(BF16))
        elif kind == "hgrn_v":
            per_head(v_s, p.astype(BF16))
        elif kind == "hgrn_og":
            per_head(og_s, (p * jnp.tanh(p) + p).astype(BF16))
        elif kind in ("gate_a", "gate_h"):
            base = 0 if kind == "gate_a" else D_MODEL
            gate_ref[:, base + off:base + off + width] = (0.5 * jnp.tanh(p) + 0.5).astype(BF16)
        else:
            c, s_lo, s_hi = c_ref[...], slo_ref[...], shi_ref[...]
            for u in range(width // LANES):
                t = p[:, u * LANES:(u + 1) * LANES]
                k_ref[:, u * LANES:(u + 1) * LANES] = _rope_lanes(t, c, s_lo, s_hi).astype(BF16)

    hgrn_cols = 4 * HGRN_WIDTH
    for lo in range(0, hgrn_cols, PROJ_CHUNK):
        token_major(lo)
    rest = [functools.partial(token_major, lo) for lo in range(hgrn_cols, TOK_WIDTH, PROJ_CHUNK)]
    rest += [functools.partial(feature_major, j) for j in range(QV_ROWS // QV_CHUNK)]

    cz = HGRN_CHUNK
    n_chunks = tm // cz
    ti = lax.broadcasted_iota(jnp.int32, (cz, cz), 0)
    si = lax.broadcasted_iota(jnp.int32, (cz, cz), 1)
    causal = ti >= si
    ng = ng_ref[...]
    nt_dims = (((1,), (1,)), ((), ()))

    def gates(ci):
        rows = slice(ci * cz, (ci + 1) * cz)
        for h in range(HGRN_HEADS):
            lanes = slice(h * HGRN_DK, (h + 1) * HGRN_DK)
            f = f_s[h, rows, :]
            k = 1.0 - f
            decay = _cumprod_rows(f)
            inv = 1.0 / decay
            d_last = decay[cz - 1:cz, :]
            q_dec = (q_s[h, rows, :].astype(F32) * decay).astype(BF16)
            k_inv = (k * inv).astype(BF16)
            sc = lax.dot_general(q_dec, k_inv, nt_dims, preferred_element_type=F32)
            sc_ref[h] = jnp.where(causal, sc, 0.0).astype(BF16)
            qd_ref[h] = q_dec
            k_end = (k * (d_last * inv)).astype(BF16)
            upd_ref[h] = lax.dot_general(v_s[h, rows, :], k_end, (((0,), (0,)), ((), ())),
                                         preferred_element_type=F32)
            dec_ref[:, lanes] = d_last

    def outputs(ci):
        rows = slice(ci * cz, (ci + 1) * cz)
        for h in range(HGRN_HEADS):
            lanes = slice(h * HGRN_DK, (h + 1) * HGRN_DK)
            v = v_s[h, rows, :]
            st = state_ref[h]
            o = jnp.dot(sc_ref[h], v, preferred_element_type=F32)
            o += lax.dot_general(qd_ref[h], st.astype(BF16), nt_dims, preferred_element_type=F32)
            state_ref[h] = st * dec_ref[:, lanes] + upd_ref[h]
            o = o * lax.rsqrt(jnp.mean(o * o, axis=-1, keepdims=True) + RMS_EPS) * ng
            yh_ref[rows, lanes] = (o * og_s[h, rows, :].astype(F32)).astype(BF16)

    per_chunk = -(-len(rest) // n_chunks)
    gates(0)
    for ci in range(n_chunks):
        for piece in rest[ci * per_chunk:(ci + 1) * per_chunk]:
            piece()
        outputs(ci)
        if ci + 1 < n_chunks:
            gates(ci + 1)
    for piece in rest[n_chunks * per_chunk:]:
        piece()


_TOK_GROUPS = (("forget", HGRN_WIDTH), ("hgrn_q", HGRN_WIDTH), ("hgrn_v", HGRN_WIDTH), ("hgrn_og", HGRN_WIDTH),
               ("gate_a", D_MODEL), ("gate_h", D_MODEL), ("rope_k", KV_WIDTH))
_TANH_KINDS = ("forget", "hgrn_q", "hgrn_og", "gate_a", "gate_h")


def _tok_kind(col):
    for kind, width in _TOK_GROUPS:
        if col < width:
            return kind, col
        col -= width
    raise ValueError(col)


def _tok_half_scale():
    return jnp.concatenate([jnp.full((w,), 0.5 if kind in _TANH_KINDS else 1.0, F32)
                            for kind, w in _TOK_GROUPS])[None, :]


def _in_proj(x2d, w_in, b_in, lb, norm_g, rope, *, tm, seq):
    n = x2d.shape[0]
    tiles_per_seq = seq // tm
    q_end, k_end, v_end = ATTN_WIDTH, ATTN_WIDTH + KV_WIDTH, ATTN_WIDTH + 2 * KV_WIDTH
    w_t = jnp.concatenate([w_in[:, :q_end], w_in[:, k_end:v_end]], axis=1).T.astype(BF16)
    b_t = jnp.concatenate([b_in[:q_end], b_in[k_end:v_end]])[:, None]
    half = _tok_half_scale()
    w_tok = (jnp.concatenate([w_in[:, v_end:], w_in[:, q_end:k_end]], axis=1) * half).astype(BF16)
    b_tok = jnp.concatenate([b_in[v_end:], b_in[q_end:k_end]])[None, :] * half
    f_a = lb + 0.5 * (1.0 - lb)
    f_b = 0.5 * (1.0 - lb)
    c, s_lo, s_hi, cos_t, sin_t = rope
    row = lambda w: pl.BlockSpec((tm, w), lambda i: (i, 0))
    col = lambda r: pl.BlockSpec((r, tm), lambda i: (0, i))
    tab = pl.BlockSpec((tm, LANES), lambda i: (i % tiles_per_seq, 0))
    tab_t = pl.BlockSpec((ROPE_HALF, tm), lambda i: (0, i % tiles_per_seq))
    return pl.pallas_call(
        functools.partial(_in_proj_body, tiles_per_seq=tiles_per_seq),
        grid=(n // tm,),
        in_specs=[row(D_MODEL), _resident((QV_ROWS, D_MODEL)), _resident((QV_ROWS, 1)),
                  _resident((D_MODEL, TOK_WIDTH)), _resident((1, TOK_WIDTH)),
                  _resident((1, HGRN_WIDTH)), _resident((1, HGRN_WIDTH)), tab, tab, tab, tab_t, tab_t,
                  _resident((1, HGRN_DV))],
        out_specs=[col(ATTN_WIDTH), col(KV_WIDTH), row(KV_WIDTH), row(2 * D_MODEL), row(HGRN_WIDTH)],
        out_shape=[jax.ShapeDtypeStruct((ATTN_WIDTH, n), BF16),
                   jax.ShapeDtypeStruct((KV_WIDTH, n), BF16),
                   jax.ShapeDtypeStruct((n, KV_WIDTH), BF16),
                   jax.ShapeDtypeStruct((n, 2 * D_MODEL), BF16),
                   jax.ShapeDtypeStruct((n, HGRN_WIDTH), BF16)],
        scratch_shapes=[pltpu.VMEM((HGRN_HEADS, tm, HGRN_DK), F32),
                        pltpu.VMEM((HGRN_HEADS, tm, HGRN_DK), BF16),
                        pltpu.VMEM((HGRN_HEADS, tm, HGRN_DV), BF16),
                        pltpu.VMEM((HGRN_HEADS, tm, HGRN_DV), BF16),
                        pltpu.VMEM((HGRN_HEADS, HGRN_DV, HGRN_DK), F32),
                        pltpu.VMEM((HGRN_HEADS, HGRN_CHUNK, HGRN_DK), BF16),
                        pltpu.VMEM((HGRN_HEADS, HGRN_DV, HGRN_DK), F32),
                        pltpu.VMEM((HGRN_HEADS, HGRN_CHUNK, HGRN_CHUNK), BF16),
                        pltpu.VMEM((1, HGRN_WIDTH), F32)],
        compiler_params=_params("arbitrary"),
        name="in_proj",
    )(x2d, w_t, b_t, w_tok, b_tok, f_a, f_b, c, s_lo, s_hi, cos_t, sin_t, norm_g)


def _rope_tables(seq):
    pos = jnp.arange(seq, dtype=F32)
    inv_freq = ROPE_THETA ** (-jnp.arange(0, ROPE_DIM, 2, dtype=F32) / ROPE_DIM)
    ang = pos[:, None] * inv_freq[None, :]
    cos, sin = jnp.cos(ang), jnp.sin(ang)
    ones = jnp.ones((seq, HEAD_DIM - ROPE_DIM), F32)
    c = jnp.concatenate([cos, cos, ones], axis=1)
    s_lo = jnp.concatenate([-sin, jnp.zeros((seq, HEAD_DIM - ROPE_HALF), F32)], axis=1)
    s_hi = jnp.concatenate([jnp.zeros((seq, ROPE_HALF), F32), sin, 0.0 * ones], axis=1)
    rep = LANES // HEAD_DIM
    return tuple(jnp.tile(t, (1, rep)) for t in (c, s_lo, s_hi)) + (cos.T, sin.T)


ATTN_TILE_BLOCKS = 8
MIX_CHUNK = 256


def _attn_mix_body(sink_ref, x_ref, qt_ref, k_ref, kh_ref, vt_ref, vth_ref, yh_ref, ga_ref, gh_ref,
                   wpa_ref, wph_ref, wo_ref, g_ref, b_ref, o_ref, y_ref, yat_ref, *, tiles, tiles_per_seq):
    step = pl.program_id(0)

    @pl.when(step == 0)
    def _():
        y_ref[...] = jnp.zeros_like(y_ref)

    @pl.when(step == tiles)
    def _():
        o_ref[...] = _layer_norm(y_ref[...], g_ref[...], b_ref[...])

    @pl.when(step < tiles)
    def _():
        _attn_mix_tile(sink_ref, x_ref, qt_ref, k_ref, kh_ref, vt_ref, vth_ref, yh_ref, ga_ref, gh_ref,
                       wpa_ref, wph_ref, wo_ref, g_ref, b_ref, o_ref, y_ref, yat_ref,
                       first_tile=step % tiles_per_seq == 0)


def _attn_mix_tile(sink_ref, x_ref, qt_ref, k_ref, kh_ref, vt_ref, vth_ref, yh_ref, ga_ref, gh_ref,
                   wpa_ref, wph_ref, wo_ref, g_ref, b_ref, o_ref, y_ref, yat_ref, *, first_tile):
    halo_bias = jnp.where(first_tile, NEG_INF, 0.0).astype(F32)
    wide = GQA_GROUP * ATTN_BLOCK
    key_i = lax.broadcasted_iota(jnp.int32, (ATTN_BLOCK, wide), 0)
    qry_i = lax.broadcasted_iota(jnp.int32, (ATTN_BLOCK, wide), 1) % ATTN_BLOCK
    from_prev = key_i > qry_i
    keep_prev = from_prev.astype(BF16)
    keep_cur = 1.0 - keep_prev
    zero_half = jnp.zeros((HEAD_DIM, ATTN_BLOCK), BF16)
    units = [(qb, kh) for qb in range(ATTN_TILE_BLOCKS) for kh in range(N_KV_HEADS)]

    def heads_of(kh):
        return [kh * GQA_GROUP + g for g in range(GQA_GROUP)]

    def toks_of(qb):
        return slice(qb * ATTN_BLOCK, (qb + 1) * ATTN_BLOCK)

    def scores(qb, kh):
        toks = toks_of(qb)
        k_prev = kh_ref[...] if qb == 0 else k_ref[(qb - 1) * ATTN_BLOCK:qb * ATTN_BLOCK, :]
        keys = jnp.concatenate([k_prev, k_ref[toks, :]], axis=0)
        pair = slice((kh // 2) * LANES, (kh // 2 + 1) * LANES)
        cols = []
        for h in heads_of(kh):
            qh = qt_ref[h * HEAD_DIM:(h + 1) * HEAD_DIM, toks]
            cols.append(jnp.concatenate([qh, zero_half] if kh % 2 == 0 else [zero_half, qh], axis=0))
        q_rhs = jnp.concatenate(cols, axis=1)
        return jnp.dot(keys[:, pair], q_rhs, preferred_element_type=F32)

    def finish(qb, kh, s):
        toks = toks_of(qb)
        heads = heads_of(kh)
        s_prev = s[:ATTN_BLOCK]
        if qb == 0:
            s_prev = s_prev + halo_bias
        c = jnp.where(from_prev, s_prev, s[ATTN_BLOCK:])
        sink = jnp.concatenate([jnp.full((1, ATTN_BLOCK), sink_ref[h] * LOG2_E, F32) for h in heads], axis=1)
        m = jnp.maximum(jnp.max(c, axis=0, keepdims=True), sink)
        p = jnp.exp2(c - m)
        denom = jnp.sum(p, axis=0, keepdims=True) + jnp.exp2(sink - m)
        pb = p.astype(BF16)
        p_both = jnp.concatenate([pb * keep_prev, pb * keep_cur], axis=0)
        vt_prev = vth_ref[...] if qb == 0 else vt_ref[:, (qb - 1) * ATTN_BLOCK:qb * ATTN_BLOCK]
        vt_both = jnp.concatenate([vt_prev[kh * HEAD_DIM:(kh + 1) * HEAD_DIM, :],
                                   vt_ref[kh * HEAD_DIM:(kh + 1) * HEAD_DIM, toks]], axis=1)
        ot = jnp.dot(vt_both, p_both, preferred_element_type=F32) * (1.0 / denom)
        for g, h in enumerate(heads):
            yat_ref[h * HEAD_DIM:(h + 1) * HEAD_DIM, toks] = ot[:, g * ATTN_BLOCK:(g + 1) * ATTN_BLOCK].astype(BF16)

    tm = x_ref.shape[0]
    n_chunks = D_MODEL // MIX_CHUNK
    chunk_cols = [slice(c * MIX_CHUNK, (c + 1) * MIX_CHUNK) for c in range(n_chunks)]
    yh = yh_ref[...]

    ph = []
    units_per_chunk = len(units) // n_chunks
    s_next = scores(*units[0])
    for i, unit in enumerate(units):
        s_cur = s_next
        if i + 1 < len(units):
            s_next = scores(*units[i + 1])
        if i % units_per_chunk == 0:
            ph.append(jnp.dot(yh, wph_ref[:, chunk_cols[i // units_per_chunk]], preferred_element_type=F32))
        finish(*unit, s_cur)

    slab_rows = tm // LN_SLABS
    slabs_per_chunk = LN_SLABS // n_chunks
    yat = yat_ref[...]
    merged = []
    for c, cols in enumerate(chunk_cols):
        pa = lax.dot_general(yat, wpa_ref[:, cols], (((0,), (0,)), ((), ())), preferred_element_type=F32)
        m = ga_ref[:, cols].astype(F32) * pa + gh_ref[:, cols].astype(F32) * ph[c]
        zero = jnp.zeros((F32_SUBLANES, LANES), F32)
        for sl in range(c * slabs_per_chunk, (c + 1) * slabs_per_chunk):
            rows = slice(sl * slab_rows, (sl + 1) * slab_rows)
            ln = _layer_norm(y_ref[rows, :], g_ref[...], b_ref[...])
            o_ref[rows, :] = ln
            zero = zero + _zero_after(ln)
        top = m[0:F32_SUBLANES] + jnp.concatenate([zero] * (MIX_CHUNK // LANES), axis=1)
        merged.append(jnp.concatenate([top, m[F32_SUBLANES:]], axis=0).astype(BF16))
    merged = jnp.concatenate(merged, axis=1)
    for cols in chunk_cols:
        mix = jnp.dot(merged, wo_ref[:, cols], preferred_element_type=F32)
        y_ref[:, cols] = DEEPNORM_ALPHA * x_ref[:, cols] + mix


def _attn_mix(sinks, x2d, qt, k, vt, y_hgrn, gates, w_pa, w_ph, w_out, g, b, *, seq):
    n = x2d.shape[0]
    tm = ATTN_TILE_BLOCKS * ATTN_BLOCK
    tiles = n // tm
    cur = lambda i: jnp.minimum(i, tiles - 1)
    halo = lambda i: jnp.maximum(cur(i) * ATTN_TILE_BLOCKS - 1, 0)
    row = lambda w, col=0: pl.BlockSpec((tm, w), lambda i: (cur(i), col))
    sq = _resident((D_MODEL, D_MODEL))
    return pl.pallas_call(
        functools.partial(_attn_mix_body, tiles=tiles, tiles_per_seq=seq // tm),
        grid=(tiles + 1,),
        in_specs=[pl.BlockSpec(memory_space=pltpu.SMEM),
                  row(D_MODEL),
                  pl.BlockSpec((ATTN_WIDTH, tm), lambda i: (0, cur(i))),
                  row(KV_WIDTH),
                  pl.BlockSpec((ATTN_BLOCK, KV_WIDTH), lambda i: (halo(i), 0)),
                  pl.BlockSpec((KV_WIDTH, tm), lambda i: (0, cur(i))),
                  pl.BlockSpec((KV_WIDTH, ATTN_BLOCK), lambda i: (0, halo(i))),
                  row(HGRN_WIDTH), row(D_MODEL, 0), row(D_MODEL, 1), sq, sq, sq,
                  _resident((1, D_MODEL)), _resident((1, D_MODEL))],
        out_specs=pl.BlockSpec((tm, D_MODEL), lambda i: (jnp.maximum(i - 1, 0), 0)),
        out_shape=jax.ShapeDtypeStruct((n, D_MODEL), F32),
        scratch_shapes=[pltpu.VMEM((tm, D_MODEL), F32),
                        pltpu.VMEM((ATTN_WIDTH, tm), BF16)],
        compiler_params=_params("arbitrary"),
        name="attn_mix",
    )(sinks, x2d, qt, k, k, vt, vt, y_hgrn, gates, gates, w_pa, w_ph, w_out, g, b)


def _cumprod_rows(f):
    row = lax.broadcasted_iota(jnp.int32, (F32_SUBLANES, LANES), 0)
    blocks, carry = [], None
    for j in range(f.shape[0] // F32_SUBLANES):
        x = f[j * F32_SUBLANES:(j + 1) * F32_SUBLANES]
        shift = 1
        while shift < F32_SUBLANES:
            x = x * jnp.where(row >= shift, pltpu.roll(x, shift, axis=0), 1.0)
            shift *= 2
        if carry is not None:
            x = x * carry
        carry = x[F32_SUBLANES - 1:F32_SUBLANES, :]
        blocks.append(x)
    return jnp.concatenate(blocks, axis=0)


def _tile(n, want):
    t = min(n, want)
    assert n % t == 0, (n, t)
    return t


def kernel(x, ln1_g, ln1_b, ffn1_w1, ffn1_w3, ffn1_w2, ln2_g, ln2_b, w_in, b_in, attn_sinks, hgrn_lb_logits,
           hgrn_norm_g, w_proj_attn, w_proj_hgrn, w_out, ln3_g, ln3_b, ffn2_w1, ffn2_w3, ffn2_w2):
    batch, seq, d = x.shape
    assert d == D_MODEL and seq % (ATTN_TILE_BLOCKS * ATTN_BLOCK) == 0
    n = batch * seq
    tm = _tile(seq, 512)
    lb_all = jnp.cumsum(jax.nn.softmax(hgrn_lb_logits.astype(F32), axis=0), axis=0)
    rope = _rope_tables(seq)
    bf = lambda w: w.astype(BF16)
    h = x.reshape(n, d)
    for l in range(DEPTH):
        h = _ffn_ln(h, bf(0.5 * ffn1_w1[l]), bf(ffn1_w3[l]), bf(0.5 * ffn1_w2[l]), ln1_g[l:l + 1], ln1_b[l:l + 1], tm=tm)
        qt, vt, k, gates, y_hgrn = _in_proj(h, w_in[l], b_in[l], lb_all[l:l + 1], hgrn_norm_g[l:l + 1], rope,
                                            tm=tm, seq=seq)
        h = _attn_mix(attn_sinks[l], h, qt, k, vt, y_hgrn, gates, bf(w_proj_attn[l]), bf(w_proj_hgrn[l]),
                      bf(w_out[l]), ln2_g[l:l + 1], ln2_b[l:l + 1], seq=seq)
        h = _ffn_ln(h, bf(0.5 * ffn2_w1[l]), bf(ffn2_w3[l]), bf(0.5 * ffn2_w2[l]), ln3_g[l:l + 1], ln3_b[l:l + 1], tm=tm)
    return h.reshape(batch, seq, d)
```

```python
import functools

import jax
import jax.numpy as jnp
from jax import lax
from jax.experimental import pallas as pl
from jax.experimental.pallas import tpu as pltpu

D_MODEL = 1024
DEPTH = 1
N_Q_HEADS = 16
N_KV_HEADS = 4
HEAD_DIM = 64
GQA_GROUP = N_Q_HEADS // N_KV_HEADS
WINDOW = 128
ATTN_BLOCK = 128
ROPE_THETA = 500000.0
ROPE_DIM = HEAD_DIM // 4
ROPE_HALF = ROPE_DIM // 2
HGRN_HEADS = 8
HGRN_DK = 128
HGRN_DV = 128
HGRN_CHUNK = 64
D_FF = 2816
ATTN_WIDTH = N_Q_HEADS * HEAD_DIM
KV_WIDTH = N_KV_HEADS * HEAD_DIM
HGRN_WIDTH = HGRN_HEADS * HGRN_DK
D_IN = ATTN_WIDTH + 2 * KV_WIDTH + 4 * HGRN_WIDTH + 2 * D_MODEL
DEEPNORM_ALPHA = (2 * DEPTH) ** 0.25
LN_EPS = 1e-5
RMS_EPS = 1e-6
NEG_INF = -1e30
LOG2_E = 1.4426950408889634

LANES = 128
F32_SUBLANES = 8
VMEM_LIMIT_BYTES = 56 * 1024 * 1024

F32 = jnp.float32
BF16 = jnp.bfloat16

assert WINDOW == ATTN_BLOCK and ROPE_HALF == F32_SUBLANES


def _layer_norm(y, g, b):
    mu = jnp.mean(y, axis=-1, keepdims=True)
    yc = y - mu
    var = jnp.mean(yc * yc, axis=-1, keepdims=True)
    return yc * lax.rsqrt(var + LN_EPS) * g + b


def _resident(shape):
    return pl.BlockSpec(shape, lambda *_: (0,) * len(shape), pipeline_mode=pl.Buffered(1))


def _params(*semantics):
    return pltpu.CompilerParams(dimension_semantics=semantics, vmem_limit_bytes=VMEM_LIMIT_BYTES)


LN_SLABS = 8


def _zero_after(x):
    bits = pltpu.bitcast(x, jnp.uint32)
    tiles = [bits[r:r + F32_SUBLANES, c:c + LANES]
             for r in range(0, x.shape[0], F32_SUBLANES) for c in range(0, x.shape[1], LANES)]
    folded = functools.reduce(jnp.bitwise_or, tiles)
    return pltpu.bitcast((folded >> 16) >> 16, F32)


def _ffn_ln_body(x_ref, w1_ref, w3_ref, w2_ref, g_ref, b_ref, o_ref, acc_ref, y_ref, *, ff_chunk, tiles):
    step = pl.program_id(0)

    @pl.when(step == 0)
    def _():
        y_ref[...] = jnp.zeros_like(y_ref)

    @pl.when(step == tiles)
    def _():
        o_ref[...] = _layer_norm(y_ref[...], g_ref[...], b_ref[...])

    @pl.when(step < tiles)
    def _():
        tm = x_ref.shape[0]
        n_slabs, slab_rows = LN_SLABS, tm // LN_SLABS
        n_chunks = D_FF // ff_chunk
        assert n_slabs <= n_chunks - 1
        x = x_ref[...]
        xb = x.astype(BF16)
        for c in range(n_chunks):
            cols = slice(c * ff_chunk, (c + 1) * ff_chunk)
            h1 = jnp.dot(xb, w1_ref[:, cols], preferred_element_type=F32)
            h3 = jnp.dot(xb, w3_ref[:, cols], preferred_element_type=F32)
            a = ((h1 * jnp.tanh(h1) + h1) * h3).astype(BF16)
            part = jnp.dot(a, w2_ref[cols, :], preferred_element_type=F32)
            if c == 0:
                acc_ref[...] = DEEPNORM_ALPHA * x + part
            elif c < n_chunks - 1:
                acc_ref[...] += part
            else:
                y_ref[...] = acc_ref[...] + part
            if c < n_slabs:
                rows = slice(c * slab_rows, (c + 1) * slab_rows)
                ln = _layer_norm(y_ref[rows, :], g_ref[...], b_ref[...])
                o_ref[rows, :] = ln
                acc_ref[0:F32_SUBLANES, 0:LANES] += _zero_after(ln)


def _ffn_ln(x2d, w1, w3, w2, g, b, *, tm, ff_chunk=256):
    n = x2d.shape[0]
    tiles = n // tm
    return pl.pallas_call(
        functools.partial(_ffn_ln_body, ff_chunk=ff_chunk, tiles=tiles),
        grid=(tiles + 1,),
        in_specs=[pl.BlockSpec((tm, D_MODEL), lambda i: (jnp.minimum(i, tiles - 1), 0)),
                  _resident((D_MODEL, D_FF)), _resident((D_MODEL, D_FF)), _resident((D_FF, D_MODEL)),
                  _resident((1, D_MODEL)), _resident((1, D_MODEL))],
        out_specs=pl.BlockSpec((tm, D_MODEL), lambda i: (jnp.maximum(i - 1, 0), 0)),
        out_shape=jax.ShapeDtypeStruct((n, D_MODEL), F32),
        scratch_shapes=[pltpu.VMEM((tm, D_MODEL), F32), pltpu.VMEM((tm, D_MODEL), F32)],
        compiler_params=_params("arbitrary"),
        name="ffn_ln",
    )(x2d, w1, w3, w2, g, b)


PROJ_CHUNK = 512
QV_ROWS = ATTN_WIDTH + KV_WIDTH
QV_CHUNK = 256
TOK_WIDTH = 4 * HGRN_WIDTH + 2 * D_MODEL + KV_WIDTH


def _rope_lanes(t, c, s_lo, s_hi):
    return t * c + pltpu.roll(t, ROPE_HALF, axis=1) * s_hi + pltpu.roll(t, LANES - ROPE_HALF, axis=1) * s_lo


def _in_proj_body(x_ref, wt_ref, bt_ref, w_ref, b_ref, fa_ref, fb_ref, c_ref, slo_ref, shi_ref, cos_ref, sin_ref, ng_ref,
                  qt_ref, vt_ref, k_ref, gate_ref, yh_ref,
                  f_s, q_s, v_s, og_s, state_ref, qd_ref, upd_ref, sc_ref, dec_ref, *, tiles_per_seq):
    @pl.when(pl.program_id(0) % tiles_per_seq == 0)
    def _():
        state_ref[...] = jnp.zeros_like(state_ref)

    xb = x_ref[...].astype(BF16)
    tm = xb.shape[0]
    q_scale = HEAD_DIM ** -0.5 * LOG2_E

    def feature_major(j):
        cos_t, sin_t = cos_ref[...], sin_ref[...]
        rows = slice(j * QV_CHUNK, (j + 1) * QV_CHUNK)
        pt = lax.dot_general(wt_ref[rows, :], xb, (((1,), (1,)), ((), ())), preferred_element_type=F32)
        pt = pt + bt_ref[rows, :]
        if j * QV_CHUNK < ATTN_WIDTH:
            for h in range(QV_CHUNK // HEAD_DIM):
                r0 = h * HEAD_DIM
                t1 = pt[r0:r0 + ROPE_HALF]
                t2 = pt[r0 + ROPE_HALF:r0 + ROPE_DIM]
                rot = jnp.concatenate([t1 * cos_t - t2 * sin_t, t2 * cos_t + t1 * sin_t,
                                       pt[r0 + ROPE_DIM:r0 + HEAD_DIM]], axis=0)
                qt_ref[j * QV_CHUNK + r0:j * QV_CHUNK + r0 + HEAD_DIM, :] = (rot * q_scale).astype(BF16)
        else:
            vt_ref[...] = pt.astype(BF16)

    def token_major(lo):
        width = min(PROJ_CHUNK, TOK_WIDTH - lo)
        cols = slice(lo, lo + width)
        p = jnp.dot(xb, w_ref[:, cols], preferred_element_type=F32) + b_ref[:, cols]
        kind, off = _tok_kind(lo)
        dst = slice(off, off + width)
        def per_head(dst_ref, val):
            for u in range(width // HGRN_DK):
                dst_ref[(off + u * HGRN_DK) // HGRN_DK] = val[:, u * HGRN_DK:(u + 1) * HGRN_DK]

        if kind == "forget":
            per_head(f_s, fa_ref[:, dst] + fb_ref[:, dst] * jnp.tanh(p))
        elif kind == "hgrn_q":
            per_head(q_s, (p * jnp.tanh(p) + p).astype(BF16))
        elif kind == "hgrn_v":
            per_head(v_s, p.astype(BF16))
        elif kind == "hgrn_og":
            per_head(og_s, (p * jnp.tanh(p) + p).astype(BF16))
        elif kind in ("gate_a", "gate_h"):
            base = 0 if kind == "gate_a" else D_MODEL
            gate_ref[:, base + off:base + off + width] = (0.5 * jnp.tanh(p) + 0.5).astype(BF16)
        else:
            c, s_lo, s_hi = c_ref[...], slo_ref[...], shi_ref[...]
            for u in range(width // LANES):
                t = p[:, u * LANES:(u + 1) * LANES]
                k_ref[:, u * LANES:(u + 1) * LANES] = _rope_lanes(t, c, s_lo, s_hi).astype(BF16)

    hgrn_cols = 4 * HGRN_WIDTH
    for lo in range(0, hgrn_cols, PROJ_CHUNK):
        token_major(lo)
    rest = [functools.partial(token_major, lo) for lo in range(hgrn_cols, TOK_WIDTH, PROJ_CHUNK)]
    rest += [functools.partial(feature_major, j) for j in range(QV_ROWS // QV_CHUNK)]

    cz = HGRN_CHUNK
    n_chunks = tm // cz
    ti = lax.broadcasted_iota(jnp.int32, (cz, cz), 0)
    si = lax.broadcasted_iota(jnp.int32, (cz, cz), 1)
    causal = ti >= si
    ng = ng_ref[...]
    nt_dims = (((1,), (1,)), ((), ()))

    def gates(ci, heads=range(HGRN_HEADS)):
        rows = slice(ci * cz, (ci + 1) * cz)
        for h in heads:
            lanes = slice(h * HGRN_DK, (h + 1) * HGRN_DK)
            f = f_s[h, rows, :]
            k = 1.0 - f
            decay = _cumprod_rows(f)
            inv = 1.0 / decay
            d_last = decay[cz - 1:cz, :]
            q_dec = (q_s[h, rows, :].astype(F32) * decay).astype(BF16)
            k_inv = (k * inv).astype(BF16)
            sc = lax.dot_general(q_dec, k_inv, nt_dims, preferred_element_type=F32)
            sc_ref[h] = jnp.where(causal, sc, 0.0).astype(BF16)
            qd_ref[h] = q_dec
            k_end = (k * (d_last * inv)).astype(BF16)
            upd_ref[h] = lax.dot_general(v_s[h, rows, :], k_end, (((0,), (0,)), ((), ())),
                                         preferred_element_type=F32)
            dec_ref[:, lanes] = d_last

    def outputs(ci, heads=range(HGRN_HEADS)):
        rows = slice(ci * cz, (ci + 1) * cz)
        for h in heads:
            lanes = slice(h * HGRN_DK, (h + 1) * HGRN_DK)
            v = v_s[h, rows, :]
            st = state_ref[h]
            o = jnp.dot(sc_ref[h], v, preferred_element_type=F32)
            o += lax.dot_general(qd_ref[h], st.astype(BF16), nt_dims, preferred_element_type=F32)
            state_ref[h] = st * dec_ref[:, lanes] + upd_ref[h]
            o = o * lax.rsqrt(jnp.mean(o * o, axis=-1, keepdims=True) + RMS_EPS) * ng
            yh_ref[rows, lanes] = (o * og_s[h, rows, :].astype(F32)).astype(BF16)

    per_chunk = -(-len(rest) // n_chunks)
    gates(0)
    for ci in range(n_chunks):
        for piece in rest[ci * per_chunk:(ci + 1) * per_chunk]:
            piece()
        for h in range(HGRN_HEADS):
            outputs(ci, [h])
            if ci + 1 < n_chunks:
                gates(ci + 1, [h])
    for piece in rest[n_chunks * per_chunk:]:
        piece()


_TOK_GROUPS = (("forget", HGRN_WIDTH), ("hgrn_q", HGRN_WIDTH), ("hgrn_v", HGRN_WIDTH), ("hgrn_og", HGRN_WIDTH),
               ("gate_a", D_MODEL), ("gate_h", D_MODEL), ("rope_k", KV_WIDTH))
_TANH_KINDS = ("forget", "hgrn_q", "hgrn_og", "gate_a", "gate_h")


def _tok_kind(col):
    for kind, width in _TOK_GROUPS:
        if col < width:
            return kind, col
        col -= width
    raise ValueError(col)


def _tok_half_scale():
    return jnp.concatenate([jnp.full((w,), 0.5 if kind in _TANH_KINDS else 1.0, F32)
                            for kind, w in _TOK_GROUPS])[None, :]


def _in_proj(x2d, w_in, b_in, lb, norm_g, rope, *, tm, seq):
    n = x2d.shape[0]
    tiles_per_seq = seq // tm
    q_end, k_end, v_end = ATTN_WIDTH, ATTN_WIDTH + KV_WIDTH, ATTN_WIDTH + 2 * KV_WIDTH
    w_t = jnp.concatenate([w_in[:, :q_end], w_in[:, k_end:v_end]], axis=1).T.astype(BF16)
    b_t = jnp.concatenate([b_in[:q_end], b_in[k_end:v_end]])[:, None]
    half = _tok_half_scale()
    w_tok = (jnp.concatenate([w_in[:, v_end:], w_in[:, q_end:k_end]], axis=1) * half).astype(BF16)
    b_tok = jnp.concatenate([b_in[v_end:], b_in[q_end:k_end]])[None, :] * half
    f_a = lb + 0.5 * (1.0 - lb)
    f_b = 0.5 * (1.0 - lb)
    c, s_lo, s_hi, cos_t, sin_t = rope
    row = lambda w: pl.BlockSpec((tm, w), lambda i: (i, 0))
    col = lambda r: pl.BlockSpec((r, tm), lambda i: (0, i))
    tab = pl.BlockSpec((tm, LANES), lambda i: (i % tiles_per_seq, 0))
    tab_t = pl.BlockSpec((ROPE_HALF, tm), lambda i: (0, i % tiles_per_seq))
    return pl.pallas_call(
        functools.partial(_in_proj_body, tiles_per_seq=tiles_per_seq),
        grid=(n // tm,),
        in_specs=[row(D_MODEL), _resident((QV_ROWS, D_MODEL)), _resident((QV_ROWS, 1)),
                  _resident((D_MODEL, TOK_WIDTH)), _resident((1, TOK_WIDTH)),
                  _resident((1, HGRN_WIDTH)), _resident((1, HGRN_WIDTH)), tab, tab, tab, tab_t, tab_t,
                  _resident((1, HGRN_DV))],
        out_specs=[col(ATTN_WIDTH), col(KV_WIDTH), row(KV_WIDTH), row(2 * D_MODEL), row(HGRN_WIDTH)],
        out_shape=[jax.ShapeDtypeStruct((ATTN_WIDTH, n), BF16),
                   jax.ShapeDtypeStruct((KV_WIDTH, n), BF16),
                   jax.ShapeDtypeStruct((n, KV_WIDTH), BF16),
                   jax.ShapeDtypeStruct((n, 2 * D_MODEL), BF16),
                   jax.ShapeDtypeStruct((n, HGRN_WIDTH), BF16)],
        scratch_shapes=[pltpu.VMEM((HGRN_HEADS, tm, HGRN_DK), F32),
                        pltpu.VMEM((HGRN_HEADS, tm, HGRN_DK), BF16),
                        pltpu.VMEM((HGRN_HEADS, tm, HGRN_DV), BF16),
                        pltpu.VMEM((HGRN_HEADS, tm, HGRN_DV), BF16),
                        pltpu.VMEM((HGRN_HEADS, HGRN_DV, HGRN_DK), F32),
                        pltpu.VMEM((HGRN_HEADS, HGRN_CHUNK, HGRN_DK), BF16),
                        pltpu.VMEM((HGRN_HEADS, HGRN_DV, HGRN_DK), F32),
                        pltpu.VMEM((HGRN_HEADS, HGRN_CHUNK, HGRN_CHUNK), BF16),
                        pltpu.VMEM((1, HGRN_WIDTH), F32)],
        compiler_params=_params("arbitrary"),
        name="in_proj",
    )(x2d, w_t, b_t, w_tok, b_tok, f_a, f_b, c, s_lo, s_hi, cos_t, sin_t, norm_g)


def _rope_tables(seq):
    pos = jnp.arange(seq, dtype=F32)
    inv_freq = ROPE_THETA ** (-jnp.arange(0, ROPE_DIM, 2, dtype=F32) / ROPE_DIM)
    ang = pos[:, None] * inv_freq[None, :]
    cos, sin = jnp.cos(ang), jnp.sin(ang)
    ones = jnp.ones((seq, HEAD_DIM - ROPE_DIM), F32)
    c = jnp.concatenate([cos, cos, ones], axis=1)
    s_lo = jnp.concatenate([-sin, jnp.zeros((seq, HEAD_DIM - ROPE_HALF), F32)], axis=1)
    s_hi = jnp.concatenate([jnp.zeros((seq, ROPE_HALF), F32), sin, 0.0 * ones], axis=1)
    rep = LANES // HEAD_DIM
    return tuple(jnp.tile(t, (1, rep)) for t in (c, s_lo, s_hi)) + (cos.T, sin.T)


ATTN_TILE_BLOCKS = 8
MIX_CHUNK = 256


def _attn_mix_body(sink_ref, x_ref, qt_ref, k_ref, kh_ref, vt_ref, vth_ref, yh_ref, ga_ref, gh_ref,
                   wpa_ref, wph_ref, wo_ref, g_ref, b_ref, o_ref, y_ref, yat_ref, *, tiles, tiles_per_seq):
    step = pl.program_id(0)

    @pl.when(step == 0)
    def _():
        y_ref[...] = jnp.zeros_like(y_ref)

    @pl.when(step == tiles)
    def _():
        o_ref[...] = _layer_norm(y_ref[...], g_ref[...], b_ref[...])

    @pl.when(step < tiles)
    def _():
        _attn_mix_tile(sink_ref, x_ref, qt_ref, k_ref, kh_ref, vt_ref, vth_ref, yh_ref, ga_ref, gh_ref,
                       wpa_ref, wph_ref, wo_ref, g_ref, b_ref, o_ref, y_ref, yat_ref,
                       first_tile=step % tiles_per_seq == 0)


def _attn_mix_tile(sink_ref, x_ref, qt_ref, k_ref, kh_ref, vt_ref, vth_ref, yh_ref, ga_ref, gh_ref,
                   wpa_ref, wph_ref, wo_ref, g_ref, b_ref, o_ref, y_ref, yat_ref, *, first_tile):
    halo_bias = jnp.where(first_tile, NEG_INF, 0.0).astype(F32)
    wide = GQA_GROUP * ATTN_BLOCK
    key_i = lax.broadcasted_iota(jnp.int32, (ATTN_BLOCK, wide), 0)
    qry_i = lax.broadcasted_iota(jnp.int32, (ATTN_BLOCK, wide), 1) % ATTN_BLOCK
    from_prev = key_i > qry_i
    keep_prev = from_prev.astype(BF16)
    keep_cur = 1.0 - keep_prev
    zero_half = jnp.zeros((HEAD_DIM, ATTN_BLOCK), BF16)
    units = [(qb, kh) for qb in range(ATTN_TILE_BLOCKS) for kh in range(N_KV_HEADS)]

    def heads_of(kh):
        return [kh * GQA_GROUP + g for g in range(GQA_GROUP)]

    def toks_of(qb):
        return slice(qb * ATTN_BLOCK, (qb + 1) * ATTN_BLOCK)

    def scores(qb, kh):
        toks = toks_of(qb)
        k_prev = kh_ref[...] if qb == 0 else k_ref[(qb - 1) * ATTN_BLOCK:qb * ATTN_BLOCK, :]
        keys = jnp.concatenate([k_prev, k_ref[toks, :]], axis=0)
        pair = slice((kh // 2) * LANES, (kh // 2 + 1) * LANES)
        cols = []
        for h in heads_of(kh):
            qh = qt_ref[h * HEAD_DIM:(h + 1) * HEAD_DIM, toks]
            cols.append(jnp.concatenate([qh, zero_half] if kh % 2 == 0 else [zero_half, qh], axis=0))
        q_rhs = jnp.concatenate(cols, axis=1)
        return jnp.dot(keys[:, pair], q_rhs, preferred_element_type=F32)

    def finish(qb, kh, s):
        toks = toks_of(qb)
        heads = heads_of(kh)
        s_prev = s[:ATTN_BLOCK]
        if qb == 0:
            s_prev = s_prev + halo_bias
        c = jnp.where(from_prev, s_prev, s[ATTN_BLOCK:])
        sink = jnp.concatenate([jnp.full((1, ATTN_BLOCK), sink_ref[h] * LOG2_E, F32) for h in heads], axis=1)
        m = jnp.maximum(jnp.max(c, axis=0, keepdims=True), sink)
        p = jnp.exp2(c - m)
        denom = jnp.sum(p, axis=0, keepdims=True) + jnp.exp2(sink - m)
        pb = p.astype(BF16)
        p_both = jnp.concatenate([pb * keep_prev, pb * keep_cur], axis=0)
        vt_prev = vth_ref[...] if qb == 0 else vt_ref[:, (qb - 1) * ATTN_BLOCK:qb * ATTN_BLOCK]
        vt_both = jnp.concatenate([vt_prev[kh * HEAD_DIM:(kh + 1) * HEAD_DIM, :],
                                   vt_ref[kh * HEAD_DIM:(kh + 1) * HEAD_DIM, toks]], axis=1)
        ot = jnp.dot(vt_both, p_both, preferred_element_type=F32) * (1.0 / denom)
        for g, h in enumerate(heads):
            yat_ref[h * HEAD_DIM:(h + 1) * HEAD_DIM, toks] = ot[:, g * ATTN_BLOCK:(g + 1) * ATTN_BLOCK].astype(BF16)

    tm = x_ref.shape[0]
    n_chunks = D_MODEL // MIX_CHUNK
    chunk_cols = [slice(c * MIX_CHUNK, (c + 1) * MIX_CHUNK) for c in range(n_chunks)]
    yh = yh_ref[...]

    ph = []
    units_per_chunk = len(units) // n_chunks
    ahead = 2
    pending = [scores(*u) for u in units[:ahead]]
    for i, unit in enumerate(units):
        s_cur = pending.pop(0)
        if i + ahead < len(units):
            pending.append(scores(*units[i + ahead]))
        if i % units_per_chunk == 0:
            ph.append(jnp.dot(yh, wph_ref[:, chunk_cols[i // units_per_chunk]], preferred_element_type=F32))
        finish(*unit, s_cur)

    slab_rows = tm // LN_SLABS
    slabs_per_chunk = LN_SLABS // n_chunks
    yat = yat_ref[...]
    merged = []
    for c, cols in enumerate(chunk_cols):
        pa = lax.dot_general(yat, wpa_ref[:, cols], (((0,), (0,)), ((), ())), preferred_element_type=F32)
        m = ga_ref[:, cols].astype(F32) * pa + gh_ref[:, cols].astype(F32) * ph[c]
        zero = jnp.zeros((F32_SUBLANES, LANES), F32)
        for sl in range(c * slabs_per_chunk, (c + 1) * slabs_per_chunk):
            rows = slice(sl * slab_rows, (sl + 1) * slab_rows)
            ln = _layer_norm(y_ref[rows, :], g_ref[...], b_ref[...])
            o_ref[rows, :] = ln
            zero = zero + _zero_after(ln)
        top = m[0:F32_SUBLANES] + jnp.concatenate([zero] * (MIX_CHUNK // LANES), axis=1)
        merged.append(jnp.concatenate([top, m[F32_SUBLANES:]], axis=0).astype(BF16))
    merged = jnp.concatenate(merged, axis=1)
    for cols in chunk_cols:
        mix = jnp.dot(merged, wo_ref[:, cols], preferred_element_type=F32)
        y_ref[:, cols] = DEEPNORM_ALPHA * x_ref[:, cols] + mix


def _attn_mix(sinks, x2d, qt, k, vt, y_hgrn, gates, w_pa, w_ph, w_out, g, b, *, seq):
    n = x2d.shape[0]
    tm = ATTN_TILE_BLOCKS * ATTN_BLOCK
    tiles = n // tm
    cur = lambda i: jnp.minimum(i, tiles - 1)
    halo = lambda i: jnp.maximum(cur(i) * ATTN_TILE_BLOCKS - 1, 0)
    row = lambda w, col=0: pl.BlockSpec((tm, w), lambda i: (cur(i), col))
    sq = _resident((D_MODEL, D_MODEL))
    return pl.pallas_call(
        functools.partial(_attn_mix_body, tiles=tiles, tiles_per_seq=seq // tm),
        grid=(tiles + 1,),
        in_specs=[pl.BlockSpec(memory_space=pltpu.SMEM),
                  row(D_MODEL),
                  pl.BlockSpec((ATTN_WIDTH, tm), lambda i: (0, cur(i))),
                  row(KV_WIDTH),
                  pl.BlockSpec((ATTN_BLOCK, KV_WIDTH), lambda i: (halo(i), 0)),
                  pl.BlockSpec((KV_WIDTH, tm), lambda i: (0, cur(i))),
                  pl.BlockSpec((KV_WIDTH, ATTN_BLOCK), lambda i: (0, halo(i))),
                  row(HGRN_WIDTH), row(D_MODEL, 0), row(D_MODEL, 1), sq, sq, sq,
                  _resident((1, D_MODEL)), _resident((1, D_MODEL))],
        out_specs=pl.BlockSpec((tm, D_MODEL), lambda i: (jnp.maximum(i - 1, 0), 0)),
        out_shape=jax.ShapeDtypeStruct((n, D_MODEL), F32),
        scratch_shapes=[pltpu.VMEM((tm, D_MODEL), F32),
                        pltpu.VMEM((ATTN_WIDTH, tm), BF16)],
        compiler_params=_params("arbitrary"),
        name="attn_mix",
    )(sinks, x2d, qt, k, k, vt, vt, y_hgrn, gates, gates, w_pa, w_ph, w_out, g, b)


def _cumprod_rows(f):
    row = lax.broadcasted_iota(jnp.int32, (F32_SUBLANES, LANES), 0)
    blocks, carry = [], None
    for j in range(f.shape[0] // F32_SUBLANES):
        x = f[j * F32_SUBLANES:(j + 1) * F32_SUBLANES]
        shift = 1
        while shift < F32_SUBLANES:
            x = x * jnp.where(row >= shift, pltpu.roll(x, shift, axis=0), 1.0)
            shift *= 2
        if carry is not None:
            x = x * carry
        carry = x[F32_SUBLANES - 1:F32_SUBLANES, :]
        blocks.append(x)
    return jnp.concatenate(blocks, axis=0)


def _tile(n, want):
    t = min(n, want)
    assert n % t == 0, (n, t)
    return t


def kernel(x, ln1_g, ln1_b, ffn1_w1, ffn1_w3, ffn1_w2, ln2_g, ln2_b, w_in, b_in, attn_sinks, hgrn_lb_logits,
           hgrn_norm_g, w_proj_attn, w_proj_hgrn, w_out, ln3_g, ln3_b, ffn2_w1, ffn2_w3, ffn2_w2):
    batch, seq, d = x.shape
    assert d == D_MODEL and seq % (ATTN_TILE_BLOCKS * ATTN_BLOCK) == 0
    n = batch * seq
    tm = _tile(seq, 512)
    lb_all = jnp.cumsum(jax.nn.softmax(hgrn_lb_logits.astype(F32), axis=0), axis=0)
    rope = _rope_tables(seq)
    bf = lambda w: w.astype(BF16)
    h = x.reshape(n, d)
    for l in range(DEPTH):
        h = _ffn_ln(h, bf(0.5 * ffn1_w1[l]), bf(ffn1_w3[l]), bf(0.5 * ffn1_w2[l]), ln1_g[l:l + 1], ln1_b[l:l + 1], tm=tm)
        qt, vt, k, gates, y_hgrn = _in_proj(h, w_in[l], b_in[l], lb_all[l:l + 1], hgrn_norm_g[l:l + 1], rope,
                                            tm=tm, seq=seq)
        h = _attn_mix(attn_sinks[l], h, qt, k, vt, y_hgrn, gates, bf(w_proj_attn[l]), bf(w_proj_hgrn[l]),
                      bf(w_out[l]), ln2_g[l:l + 1], ln2_b[l:l + 1], seq=seq)
        h = _ffn_ln(h, bf(0.5 * ffn2_w1[l]), bf(ffn2_w3[l]), bf(0.5 * ffn2_w2[l]), ln3_g[l:l + 1], ln3_b[l:l + 1], tm=tm)
    return h.reshape(batch, seq, d)
```

```python
import functools

import jax
import jax.numpy as jnp
from jax import lax
from jax.experimental import pallas as pl
from jax.experimental.pallas import tpu as pltpu

D_MODEL = 1024
DEPTH = 1
N_Q_HEADS = 16
N_KV_HEADS = 4
HEAD_DIM = 64
GQA_GROUP = N_Q_HEADS // N_KV_HEADS
WINDOW = 128
ATTN_BLOCK = 128
ROPE_THETA = 500000.0
ROPE_DIM = HEAD_DIM // 4
ROPE_HALF = ROPE_DIM // 2
HGRN_HEADS = 8
HGRN_DK = 128
HGRN_DV = 128
HGRN_CHUNK = 64
D_FF = 2816
ATTN_WIDTH = N_Q_HEADS * HEAD_DIM
KV_WIDTH = N_KV_HEADS * HEAD_DIM
HGRN_WIDTH = HGRN_HEADS * HGRN_DK
D_IN = ATTN_WIDTH + 2 * KV_WIDTH + 4 * HGRN_WIDTH + 2 * D_MODEL
DEEPNORM_ALPHA = (2 * DEPTH) ** 0.25
LN_EPS = 1e-5
RMS_EPS = 1e-6
NEG_INF = -1e30
LOG2_E = 1.4426950408889634

LANES = 128
F32_SUBLANES = 8
VMEM_LIMIT_BYTES = 56 * 1024 * 1024

F32 = jnp.float32
BF16 = jnp.bfloat16

assert WINDOW == ATTN_BLOCK and ROPE_HALF == F32_SUBLANES


def _layer_norm(y, g, b):
    mu = jnp.mean(y, axis=-1, keepdims=True)
    yc = y - mu
    var = jnp.mean(yc * yc, axis=-1, keepdims=True)
    return yc * lax.rsqrt(var + LN_EPS) * g + b


def _resident(shape):
    return pl.BlockSpec(shape, lambda *_: (0,) * len(shape), pipeline_mode=pl.Buffered(1))


def _params(*semantics):
    return pltpu.CompilerParams(dimension_semantics=semantics, vmem_limit_bytes=VMEM_LIMIT_BYTES)


LN_SLABS = 8


def _zero_after(x):
    bits = pltpu.bitcast(x, jnp.uint32)
    tiles = [bits[r:r + F32_SUBLANES, c:c + LANES]
             for r in range(0, x.shape[0], F32_SUBLANES) for c in range(0, x.shape[1], LANES)]
    folded = functools.reduce(jnp.bitwise_or, tiles)
    return pltpu.bitcast((folded >> 16) >> 16, F32)


def _ffn_ln_body(x_ref, w1_ref, w3_ref, w2_ref, g_ref, b_ref, o_ref, acc_ref, y_ref, *, ff_chunk, tiles):
    step = pl.program_id(0)

    @pl.when(step == 0)
    def _():
        y_ref[...] = jnp.zeros_like(y_ref)

    @pl.when(step == tiles)
    def _():
        o_ref[...] = _layer_norm(y_ref[...], g_ref[...], b_ref[...])

    @pl.when(step < tiles)
    def _():
        tm = x_ref.shape[0]
        n_slabs, slab_rows = LN_SLABS, tm // LN_SLABS
        n_chunks = D_FF // ff_chunk
        assert n_slabs <= n_chunks - 1
        x = x_ref[...]
        xb = x.astype(BF16)
        for c in range(n_chunks):
            cols = slice(c * ff_chunk, (c + 1) * ff_chunk)
            h1 = jnp.dot(xb, w1_ref[:, cols], preferred_element_type=F32)
            h3 = jnp.dot(xb, w3_ref[:, cols], preferred_element_type=F32)
            a = ((h1 * jnp.tanh(h1) + h1) * h3).astype(BF16)
            part = jnp.dot(a, w2_ref[cols, :], preferred_element_type=F32)
            if c == 0:
                acc_ref[...] = DEEPNORM_ALPHA * x + part
            elif c < n_chunks - 1:
                acc_ref[...] += part
            else:
                y_ref[...] = acc_ref[...] + part
            if c < n_slabs:
                rows = slice(c * slab_rows, (c + 1) * slab_rows)
                ln = _layer_norm(y_ref[rows, :], g_ref[...], b_ref[...])
                o_ref[rows, :] = ln
                acc_ref[0:F32_SUBLANES, 0:LANES] += _zero_after(ln)


def _ffn_ln(x2d, w1, w3, w2, g, b, *, tm, ff_chunk=256):
    n = x2d.shape[0]
    tiles = n // tm
    return pl.pallas_call(
        functools.partial(_ffn_ln_body, ff_chunk=ff_chunk, tiles=tiles),
        grid=(tiles + 1,),
        in_specs=[pl.BlockSpec((tm, D_MODEL), lambda i: (jnp.minimum(i, tiles - 1), 0)),
                  _resident((D_MODEL, D_FF)), _resident((D_MODEL, D_FF)), _resident((D_FF, D_MODEL)),
                  _resident((1, D_MODEL)), _resident((1, D_MODEL))],
        out_specs=pl.BlockSpec((tm, D_MODEL), lambda i: (jnp.maximum(i - 1, 0), 0)),
        out_shape=jax.ShapeDtypeStruct((n, D_MODEL), F32),
        scratch_shapes=[pltpu.VMEM((tm, D_MODEL), F32), pltpu.VMEM((tm, D_MODEL), F32)],
        compiler_params=_params("arbitrary"),
        name="ffn_ln",
    )(x2d, w1, w3, w2, g, b)


PROJ_CHUNK = 512
QV_ROWS = ATTN_WIDTH + KV_WIDTH
QV_CHUNK = 256
TOK_WIDTH = 4 * HGRN_WIDTH + 2 * D_MODEL + KV_WIDTH


def _rope_lanes(t, c, s_lo, s_hi):
    return t * c + pltpu.roll(t, ROPE_HALF, axis=1) * s_hi + pltpu.roll(t, LANES - ROPE_HALF, axis=1) * s_lo


def _in_proj_body(x_ref, wt_ref, bt_ref, w_ref, b_ref, fa_ref, fb_ref, c_ref, slo_ref, shi_ref, cos_ref, sin_ref, ng_ref,
                  qt_ref, vt_ref, k_ref, gate_ref, yh_ref,
                  f_s, q_s, v_s, og_s, state_ref, qd_ref, upd_ref, sc_ref, dec_ref, *, tiles_per_seq):
    @pl.when(pl.program_id(0) % tiles_per_seq == 0)
    def _():
        state_ref[...] = jnp.zeros_like(state_ref)

    xb = x_ref[...].astype(BF16)
    tm = xb.shape[0]
    q_scale = HEAD_DIM ** -0.5 * LOG2_E

    def feature_major(j):
        cos_t, sin_t = cos_ref[...], sin_ref[...]
        rows = slice(j * QV_CHUNK, (j + 1) * QV_CHUNK)
        pt = lax.dot_general(wt_ref[rows, :], xb, (((1,), (1,)), ((), ())), preferred_element_type=F32)
        pt = pt + bt_ref[rows, :]
        if j * QV_CHUNK < ATTN_WIDTH:
            for h in range(QV_CHUNK // HEAD_DIM):
                r0 = h * HEAD_DIM
                t1 = pt[r0:r0 + ROPE_HALF]
                t2 = pt[r0 + ROPE_HALF:r0 + ROPE_DIM]
                rot = jnp.concatenate([t1 * cos_t - t2 * sin_t, t2 * cos_t + t1 * sin_t,
                                       pt[r0 + ROPE_DIM:r0 + HEAD_DIM]], axis=0)
                qt_ref[j * QV_CHUNK + r0:j * QV_CHUNK + r0 + HEAD_DIM, :] = (rot * q_scale).astype(BF16)
        else:
            vt_ref[...] = pt.astype(BF16)

    def token_major(lo):
        width = min(PROJ_CHUNK, TOK_WIDTH - lo)
        cols = slice(lo, lo + width)
        p = jnp.dot(xb, w_ref[:, cols], preferred_element_type=F32) + b_ref[:, cols]
        kind, off = _tok_kind(lo)
        dst = slice(off, off + width)
        def per_head(dst_ref, val):
            for u in range(width // HGRN_DK):
                dst_ref[(off + u * HGRN_DK) // HGRN_DK] = val[:, u * HGRN_DK:(u + 1) * HGRN_DK]

        if kind == "forget":
            per_head(f_s, fa_ref[:, dst] + fb_ref[:, dst] * jnp.tanh(p))
        elif kind == "hgrn_q":
            per_head(q_s, (p * jnp.tanh(p) + p).astype(BF16))
        elif kind == "hgrn_v":
            per_head(v_s, p.astype(BF16))
        elif kind == "hgrn_og":
            per_head(og_s, (p * jnp.tanh(p) + p).astype(BF16))
        elif kind in ("gate_a", "gate_h"):
            base = 0 if kind == "gate_a" else D_MODEL
            gate_ref[:, base + off:base + off + width] = (0.5 * jnp.tanh(p) + 0.5).astype(BF16)
        else:
            c, s_lo, s_hi = c_ref[...], slo_ref[...], shi_ref[...]
            for u in range(width // LANES):
                t = p[:, u * LANES:(u + 1) * LANES]
                k_ref[:, u * LANES:(u + 1) * LANES] = _rope_lanes(t, c, s_lo, s_hi).astype(BF16)

    hgrn_cols = 4 * HGRN_WIDTH
    for lo in range(0, hgrn_cols, PROJ_CHUNK):
        token_major(lo)
    rest = [functools.partial(token_major, lo) for lo in range(hgrn_cols, TOK_WIDTH, PROJ_CHUNK)]
    rest += [functools.partial(feature_major, j) for j in range(QV_ROWS // QV_CHUNK)]

    cz = HGRN_CHUNK
    n_chunks = tm // cz
    ti = lax.broadcasted_iota(jnp.int32, (cz, cz), 0)
    si = lax.broadcasted_iota(jnp.int32, (cz, cz), 1)
    causal = ti >= si
    ng = ng_ref[...]
    nt_dims = (((1,), (1,)), ((), ()))

    def gates(ci, heads=range(HGRN_HEADS)):
        rows = slice(ci * cz, (ci + 1) * cz)
        for h in heads:
            lanes = slice(h * HGRN_DK, (h + 1) * HGRN_DK)
            f = f_s[h, rows, :]
            k = 1.0 - f
            decay = _cumprod_rows(f)
            inv = 1.0 / decay
            d_last = decay[cz - 1:cz, :]
            q_dec = (q_s[h, rows, :].astype(F32) * decay).astype(BF16)
            k_inv = (k * inv).astype(BF16)
            sc = lax.dot_general(q_dec, k_inv, nt_dims, preferred_element_type=F32)
            sc_ref[h] = jnp.where(causal, sc, 0.0).astype(BF16)
            qd_ref[h] = q_dec
            k_end = (k * (d_last * inv)).astype(BF16)
            upd_ref[h] = lax.dot_general(v_s[h, rows, :], k_end, (((0,), (0,)), ((), ())),
                                         preferred_element_type=F32)
            dec_ref[:, lanes] = d_last

    def outputs(ci, heads=range(HGRN_HEADS)):
        rows = slice(ci * cz, (ci + 1) * cz)
        for h in heads:
            lanes = slice(h * HGRN_DK, (h + 1) * HGRN_DK)
            v = v_s[h, rows, :]
            st = state_ref[h]
            o = jnp.dot(sc_ref[h], v, preferred_element_type=F32)
            o += lax.dot_general(qd_ref[h], st.astype(BF16), nt_dims, preferred_element_type=F32)
            state_ref[h] = st * dec_ref[:, lanes] + upd_ref[h]
            o = o * lax.rsqrt(jnp.mean(o * o, axis=-1, keepdims=True) + RMS_EPS) * ng
            yh_ref[rows, lanes] = (o * og_s[h, rows, :].astype(F32)).astype(BF16)

    per_chunk = -(-len(rest) // n_chunks)
    gates(0)
    for ci in range(n_chunks):
        for piece in rest[ci * per_chunk:(ci + 1) * per_chunk]:
            piece()
        for h in range(HGRN_HEADS):
            outputs(ci, [h])
            if ci + 1 < n_chunks:
                gates(ci + 1, [h])
    for piece in rest[n_chunks * per_chunk:]:
        piece()


_TOK_GROUPS = (("forget", HGRN_WIDTH), ("hgrn_q", HGRN_WIDTH), ("hgrn_v", HGRN_WIDTH), ("hgrn_og", HGRN_WIDTH),
               ("gate_a", D_MODEL), ("gate_h", D_MODEL), ("rope_k", KV_WIDTH))
_TANH_KINDS = ("forget", "hgrn_q", "hgrn_og", "gate_a", "gate_h")


def _tok_kind(col):
    for kind, width in _TOK_GROUPS:
        if col < width:
            return kind, col
        col -= width
    raise ValueError(col)


def _tok_half_scale():
    return jnp.concatenate([jnp.full((w,), 0.5 if kind in _TANH_KINDS else 1.0, F32)
                            for kind, w in _TOK_GROUPS])[None, :]


def _in_proj(x2d, w_in, b_in, lb, norm_g, rope, *, tm, seq):
    n = x2d.shape[0]
    tiles_per_seq = seq // tm
    q_end, k_end, v_end = ATTN_WIDTH, ATTN_WIDTH + KV_WIDTH, ATTN_WIDTH + 2 * KV_WIDTH
    w_t = jnp.concatenate([w_in[:, :q_end], w_in[:, k_end:v_end]], axis=1).T.astype(BF16)
    b_t = jnp.concatenate([b_in[:q_end], b_in[k_end:v_end]])[:, None]
    half = _tok_half_scale()
    w_tok = (jnp.concatenate([w_in[:, v_end:], w_in[:, q_end:k_end]], axis=1) * half).astype(BF16)
    b_tok = jnp.concatenate([b_in[v_end:], b_in[q_end:k_end]])[None, :] * half
    f_a = lb + 0.5 * (1.0 - lb)
    f_b = 0.5 * (1.0 - lb)
    c, s_lo, s_hi, cos_t, sin_t = rope
    row = lambda w: pl.BlockSpec((tm, w), lambda i: (i, 0))
    col = lambda r: pl.BlockSpec((r, tm), lambda i: (0, i))
    tab = pl.BlockSpec((tm, LANES), lambda i: (i % tiles_per_seq, 0))
    tab_t = pl.BlockSpec((ROPE_HALF, tm), lambda i: (0, i % tiles_per_seq))
    return pl.pallas_call(
        functools.partial(_in_proj_body, tiles_per_seq=tiles_per_seq),
        grid=(n // tm,),
        in_specs=[row(D_MODEL), _resident((QV_ROWS, D_MODEL)), _resident((QV_ROWS, 1)),
                  _resident((D_MODEL, TOK_WIDTH)), _resident((1, TOK_WIDTH)),
                  _resident((1, HGRN_WIDTH)), _resident((1, HGRN_WIDTH)), tab, tab, tab, tab_t, tab_t,
                  _resident((1, HGRN_DV))],
        out_specs=[col(ATTN_WIDTH), col(KV_WIDTH), row(KV_WIDTH), row(2 * D_MODEL), row(HGRN_WIDTH)],
        out_shape=[jax.ShapeDtypeStruct((ATTN_WIDTH, n), BF16),
                   jax.ShapeDtypeStruct((KV_WIDTH, n), BF16),
                   jax.ShapeDtypeStruct((n, KV_WIDTH), BF16),
                   jax.ShapeDtypeStruct((n, 2 * D_MODEL), BF16),
                   jax.ShapeDtypeStruct((n, HGRN_WIDTH), BF16)],
        scratch_shapes=[pltpu.VMEM((HGRN_HEADS, tm, HGRN_DK), F32),
                        pltpu.VMEM((HGRN_HEADS, tm, HGRN_DK), BF16),
                        pltpu.VMEM((HGRN_HEADS, tm, HGRN_DV), BF16),
                        pltpu.VMEM((HGRN_HEADS, tm, HGRN_DV), BF16),
                        pltpu.VMEM((HGRN_HEADS, HGRN_DV, HGRN_DK), F32),
                        pltpu.VMEM((HGRN_HEADS, HGRN_CHUNK, HGRN_DK), BF16),
                        pltpu.VMEM((HGRN_HEADS, HGRN_DV, HGRN_DK), F32),
                        pltpu.VMEM((HGRN_HEADS, HGRN_CHUNK, HGRN_CHUNK), BF16),
                        pltpu.VMEM((1, HGRN_WIDTH), F32)],
        compiler_params=_params("arbitrary"),
        name="in_proj",
    )(x2d, w_t, b_t, w_tok, b_tok, f_a, f_b, c, s_lo, s_hi, cos_t, sin_t, norm_g)


def _rope_tables(seq):
    pos = jnp.arange(seq, dtype=F32)
    inv_freq = ROPE_THETA ** (-jnp.arange(0, ROPE_DIM, 2, dtype=F32) / ROPE_DIM)
    ang = pos[:, None] * inv_freq[None, :]
    cos, sin = jnp.cos(ang), jnp.sin(ang)
    ones = jnp.ones((seq, HEAD_DIM - ROPE_DIM), F32)
    c = jnp.concatenate([cos, cos, ones], axis=1)
    s_lo = jnp.concatenate([-sin, jnp.zeros((seq, HEAD_DIM - ROPE_HALF), F32)], axis=1)
    s_hi = jnp.concatenate([jnp.zeros((seq, ROPE_HALF), F32), sin, 0.0 * ones], axis=1)
    rep = LANES // HEAD_DIM
    return tuple(jnp.tile(t, (1, rep)) for t in (c, s_lo, s_hi)) + (cos.T, sin.T)


ATTN_TILE_BLOCKS = 8
MIX_CHUNK = 256


def _attn_mix_body(sink_ref, x_ref, qt_ref, k_ref, kh_ref, vt_ref, vth_ref, yh_ref, ga_ref, gh_ref,
                   wpa_ref, wph_ref, wo_ref, g_ref, b_ref, o_ref, y_ref, yat_ref, *, tiles, tiles_per_seq):
    step = pl.program_id(0)

    @pl.when(step == 0)
    def _():
        y_ref[...] = jnp.zeros_like(y_ref)

    @pl.when(step == tiles)
    def _():
        o_ref[...] = _layer_norm(y_ref[...], g_ref[...], b_ref[...])

    @pl.when(step < tiles)
    def _():
        _attn_mix_tile(sink_ref, x_ref, qt_ref, k_ref, kh_ref, vt_ref, vth_ref, yh_ref, ga_ref, gh_ref,
                       wpa_ref, wph_ref, wo_ref, g_ref, b_ref, o_ref, y_ref, yat_ref,
                       first_tile=step % tiles_per_seq == 0)


def _attn_mix_tile(sink_ref, x_ref, qt_ref, k_ref, kh_ref, vt_ref, vth_ref, yh_ref, ga_ref, gh_ref,
                   wpa_ref, wph_ref, wo_ref, g_ref, b_ref, o_ref, y_ref, yat_ref, *, first_tile):
    halo_bias = jnp.where(first_tile, NEG_INF, 0.0).astype(F32)
    wide = GQA_GROUP * ATTN_BLOCK
    key_i = lax.broadcasted_iota(jnp.int32, (ATTN_BLOCK, wide), 0)
    qry_i = lax.broadcasted_iota(jnp.int32, (ATTN_BLOCK, wide), 1) % ATTN_BLOCK
    from_prev = key_i > qry_i
    keep_prev = from_prev.astype(BF16)
    keep_cur = 1.0 - keep_prev
    zero_half = jnp.zeros((HEAD_DIM, ATTN_BLOCK), BF16)
    units = [(qb, kh) for qb in range(ATTN_TILE_BLOCKS) for kh in range(N_KV_HEADS)]

    def heads_of(kh):
        return [kh * GQA_GROUP + g for g in range(GQA_GROUP)]

    def toks_of(qb):
        return slice(qb * ATTN_BLOCK, (qb + 1) * ATTN_BLOCK)

    def scores(qb, kh):
        toks = toks_of(qb)
        k_prev = kh_ref[...] if qb == 0 else k_ref[(qb - 1) * ATTN_BLOCK:qb * ATTN_BLOCK, :]
        keys = jnp.concatenate([k_prev, k_ref[toks, :]], axis=0)
        pair = slice((kh // 2) * LANES, (kh // 2 + 1) * LANES)
        cols = []
        for h in heads_of(kh):
            qh = qt_ref[h * HEAD_DIM:(h + 1) * HEAD_DIM, toks]
            cols.append(jnp.concatenate([qh, zero_half] if kh % 2 == 0 else [zero_half, qh], axis=0))
        q_rhs = jnp.concatenate(cols, axis=1)
        return jnp.dot(keys[:, pair], q_rhs, preferred_element_type=F32)

    def finish(qb, kh, s):
        toks = toks_of(qb)
        heads = heads_of(kh)
        s_prev = s[:ATTN_BLOCK]
        if qb == 0:
            s_prev = s_prev + halo_bias
        c = jnp.where(from_prev, s_prev, s[ATTN_BLOCK:])
        sink = jnp.concatenate([jnp.full((1, ATTN_BLOCK), sink_ref[h] * LOG2_E, F32) for h in heads], axis=1)
        m = jnp.maximum(jnp.max(c, axis=0, keepdims=True), sink)
        p = jnp.exp2(c - m)
        denom = jnp.sum(p, axis=0, keepdims=True) + jnp.exp2(sink - m)
        pb = p.astype(BF16)
        p_both = jnp.concatenate([pb * keep_prev, pb * keep_cur], axis=0)
        vt_prev = vth_ref[...] if qb == 0 else vt_ref[:, (qb - 1) * ATTN_BLOCK:qb * ATTN_BLOCK]
        vt_both = jnp.concatenate([vt_prev[kh * HEAD_DIM:(kh + 1) * HEAD_DIM, :],
                                   vt_ref[kh * HEAD_DIM:(kh + 1) * HEAD_DIM, toks]], axis=1)
        ot = jnp.dot(vt_both, p_both, preferred_element_type=F32) * (1.0 / denom)
        for g, h in enumerate(heads):
            yat_ref[h * HEAD_DIM:(h + 1) * HEAD_DIM, toks] = ot[:, g * ATTN_BLOCK:(g + 1) * ATTN_BLOCK].astype(BF16)

    tm = x_ref.shape[0]
    n_chunks = D_MODEL // MIX_CHUNK
    chunk_cols = [slice(c * MIX_CHUNK, (c + 1) * MIX_CHUNK) for c in range(n_chunks)]
    yh = yh_ref[...]

    ph = []
    units_per_chunk = len(units) // n_chunks
    ahead = 2
    pending = [scores(*u) for u in units[:ahead]]
    for i, unit in enumerate(units):
        s_cur = pending.pop(0)
        if i + ahead < len(units):
            pending.append(scores(*units[i + ahead]))
        finish(*unit, s_cur)
        if i % units_per_chunk == 0:
            ph.append(jnp.dot(yh, wph_ref[:, chunk_cols[i // units_per_chunk]], preferred_element_type=F32))

    slab_rows = tm // LN_SLABS
    slabs_per_chunk = LN_SLABS // n_chunks
    yat = yat_ref[...]
    merged = []
    for c, cols in enumerate(chunk_cols):
        pa = lax.dot_general(yat, wpa_ref[:, cols], (((0,), (0,)), ((), ())), preferred_element_type=F32)
        m = ga_ref[:, cols].astype(F32) * pa + gh_ref[:, cols].astype(F32) * ph[c]
        zero = jnp.zeros((F32_SUBLANES, LANES), F32)
        for sl in range(c * slabs_per_chunk, (c + 1) * slabs_per_chunk):
            rows = slice(sl * slab_rows, (sl + 1) * slab_rows)
            ln = _layer_norm(y_ref[rows, :], g_ref[...], b_ref[...])
            o_ref[rows, :] = ln
            zero = zero + _zero_after(ln)
        top = m[0:F32_SUBLANES] + jnp.concatenate([zero] * (MIX_CHUNK // LANES), axis=1)
        merged.append(jnp.concatenate([top, m[F32_SUBLANES:]], axis=0).astype(BF16))
    merged = jnp.concatenate(merged, axis=1)
    for cols in chunk_cols:
        mix = jnp.dot(merged, wo_ref[:, cols], preferred_element_type=F32)
        y_ref[:, cols] = DEEPNORM_ALPHA * x_ref[:, cols] + mix


def _attn_mix(sinks, x2d, qt, k, vt, y_hgrn, gates, w_pa, w_ph, w_out, g, b, *, seq):
    n = x2d.shape[0]
    tm = ATTN_TILE_BLOCKS * ATTN_BLOCK
    tiles = n // tm
    cur = lambda i: jnp.minimum(i, tiles - 1)
    halo = lambda i: jnp.maximum(cur(i) * ATTN_TILE_BLOCKS - 1, 0)
    row = lambda w, col=0: pl.BlockSpec((tm, w), lambda i: (cur(i), col))
    sq = _resident((D_MODEL, D_MODEL))
    return pl.pallas_call(
        functools.partial(_attn_mix_body, tiles=tiles, tiles_per_seq=seq // tm),
        grid=(tiles + 1,),
        in_specs=[pl.BlockSpec(memory_space=pltpu.SMEM),
                  row(D_MODEL),
                  pl.BlockSpec((ATTN_WIDTH, tm), lambda i: (0, cur(i))),
                  row(KV_WIDTH),
                  pl.BlockSpec((ATTN_BLOCK, KV_WIDTH), lambda i: (halo(i), 0)),
                  pl.BlockSpec((KV_WIDTH, tm), lambda i: (0, cur(i))),
                  pl.BlockSpec((KV_WIDTH, ATTN_BLOCK), lambda i: (0, halo(i))),
                  row(HGRN_WIDTH), row(D_MODEL, 0), row(D_MODEL, 1), sq, sq, sq,
                  _resident((1, D_MODEL)), _resident((1, D_MODEL))],
        out_specs=pl.BlockSpec((tm, D_MODEL), lambda i: (jnp.maximum(i - 1, 0), 0)),
        out_shape=jax.ShapeDtypeStruct((n, D_MODEL), F32),
        scratch_shapes=[pltpu.VMEM((tm, D_MODEL), F32),
                        pltpu.VMEM((ATTN_WIDTH, tm), BF16)],
        compiler_params=_params("arbitrary"),
        name="attn_mix",
    )(sinks, x2d, qt, k, k, vt, vt, y_hgrn, gates, gates, w_pa, w_ph, w_out, g, b)


def _cumprod_rows(f):
    row = lax.broadcasted_iota(jnp.int32, (F32_SUBLANES, LANES), 0)
    blocks, carry = [], None
    for j in range(f.shape[0] // F32_SUBLANES):
        x = f[j * F32_SUBLANES:(j + 1) * F32_SUBLANES]
        shift = 1
        while shift < F32_SUBLANES:
            x = x * jnp.where(row >= shift, pltpu.roll(x, shift, axis=0), 1.0)
            shift *= 2
        if carry is not None:
            x = x * carry
        carry = x[F32_SUBLANES - 1:F32_SUBLANES, :]
        blocks.append(x)
    return jnp.concatenate(blocks, axis=0)


def _tile(n, want):
    t = min(n, want)
    assert n % t == 0, (n, t)
    return t


def kernel(x, ln1_g, ln1_b, ffn1_w1, ffn1_w3, ffn1_w2, ln2_g, ln2_b, w_in, b_in, attn_sinks, hgrn_lb_logits,
           hgrn_norm_g, w_proj_attn, w_proj_hgrn, w_out, ln3_g, ln3_b, ffn2_w1, ffn2_w3, ffn2_w2):
    batch, seq, d = x.shape
    assert d == D_MODEL and seq % (ATTN_TILE_BLOCKS * ATTN_BLOCK) == 0
    n = batch * seq
    tm = _tile(seq, 512)
    lb_all = jnp.cumsum(jax.nn.softmax(hgrn_lb_logits.astype(F32), axis=0), axis=0)
    rope = _rope_tables(seq)
    bf = lambda w: w.astype(BF16)
    h = x.reshape(n, d)
    for l in range(DEPTH):
        h = _ffn_ln(h, bf(0.5 * ffn1_w1[l]), bf(ffn1_w3[l]), bf(0.5 * ffn1_w2[l]), ln1_g[l:l + 1], ln1_b[l:l + 1], tm=tm)
        qt, vt, k, gates, y_hgrn = _in_proj(h, w_in[l], b_in[l], lb_all[l:l + 1], hgrn_norm_g[l:l + 1], rope,
                                            tm=tm, seq=seq)
        h = _attn_mix(attn_sinks[l], h, qt, k, vt, y_hgrn, gates, bf(w_proj_attn[l]), bf(w_proj_hgrn[l]),
                      bf(w_out[l]), ln2_g[l:l + 1], ln2_b[l:l + 1], seq=seq)
        h = _ffn_ln(h, bf(0.5 * ffn2_w1[l]), bf(ffn2_w3[l]), bf(0.5 * ffn2_w2[l]), ln3_g[l:l + 1], ln3_b[l:l + 1], tm=tm)
    return h.reshape(batch, seq, d)
```

```python
import functools

import jax
import jax.numpy as jnp
from jax import lax
from jax.experimental import pallas as pl
from jax.experimental.pallas import tpu as pltpu

D_MODEL = 1024
DEPTH = 1
N_Q_HEADS = 16
N_KV_HEADS = 4
HEAD_DIM = 64
GQA_GROUP = N_Q_HEADS // N_KV_HEADS
WINDOW = 128
ATTN_BLOCK = 128
ROPE_THETA = 500000.0
ROPE_DIM = HEAD_DIM // 4
ROPE_HALF = ROPE_DIM // 2
HGRN_HEADS = 8
HGRN_DK = 128
HGRN_DV = 128
HGRN_CHUNK = 64
D_FF = 2816
ATTN_WIDTH = N_Q_HEADS * HEAD_DIM
KV_WIDTH = N_KV_HEADS * HEAD_DIM
HGRN_WIDTH = HGRN_HEADS * HGRN_DK
D_IN = ATTN_WIDTH + 2 * KV_WIDTH + 4 * HGRN_WIDTH + 2 * D_MODEL
DEEPNORM_ALPHA = (2 * DEPTH) ** 0.25
LN_EPS = 1e-5
RMS_EPS = 1e-6
NEG_INF = -1e30
LOG2_E = 1.4426950408889634

LANES = 128
F32_SUBLANES = 8
VMEM_LIMIT_BYTES = 56 * 1024 * 1024

F32 = jnp.float32
BF16 = jnp.bfloat16

assert WINDOW == ATTN_BLOCK and ROPE_HALF == F32_SUBLANES


def _layer_norm(y, g, b):
    mu = jnp.mean(y, axis=-1, keepdims=True)
    yc = y - mu
    var = jnp.mean(yc * yc, axis=-1, keepdims=True)
    return yc * lax.rsqrt(var + LN_EPS) * g + b


def _resident(shape):
    return pl.BlockSpec(shape, lambda *_: (0,) * len(shape), pipeline_mode=pl.Buffered(1))


def _params(*semantics):
    return pltpu.CompilerParams(dimension_semantics=semantics, vmem_limit_bytes=VMEM_LIMIT_BYTES)


LN_SLABS = 8


def _zero_after(x):
    bits = pltpu.bitcast(x, jnp.uint32)
    tiles = [bits[r:r + F32_SUBLANES, c:c + LANES]
             for r in range(0, x.shape[0], F32_SUBLANES) for c in range(0, x.shape[1], LANES)]
    folded = functools.reduce(jnp.bitwise_or, tiles)
    return pltpu.bitcast((folded >> 16) >> 16, F32)


def _ffn_ln_body(x_ref, w1_ref, w3_ref, w2_ref, g_ref, b_ref, o_ref, acc_ref, y_ref, *, ff_chunk, tiles):
    step = pl.program_id(0)

    @pl.when(step == 0)
    def _():
        y_ref[...] = jnp.zeros_like(y_ref)

    @pl.when(step == tiles)
    def _():
        o_ref[...] = _layer_norm(y_ref[...], g_ref[...], b_ref[...])

    @pl.when(step < tiles)
    def _():
        tm = x_ref.shape[0]
        n_slabs, slab_rows = LN_SLABS, tm // LN_SLABS
        n_chunks = D_FF // ff_chunk
        assert n_slabs <= n_chunks - 1
        x = x_ref[...]
        xb = x.astype(BF16)
        for c in range(n_chunks):
            cols = slice(c * ff_chunk, (c + 1) * ff_chunk)
            h1 = jnp.dot(xb, w1_ref[:, cols], preferred_element_type=F32)
            h3 = jnp.dot(xb, w3_ref[:, cols], preferred_element_type=F32)
            a = ((h1 * jnp.tanh(h1) + h1) * h3).astype(BF16)
            part = jnp.dot(a, w2_ref[cols, :], preferred_element_type=F32)
            if c == 0:
                acc_ref[...] = DEEPNORM_ALPHA * x + part
            elif c < n_chunks - 1:
                acc_ref[...] += part
            else:
                y_ref[...] = acc_ref[...] + part
            if c < n_slabs:
                rows = slice(c * slab_rows, (c + 1) * slab_rows)
                ln = _layer_norm(y_ref[rows, :], g_ref[...], b_ref[...])
                o_ref[rows, :] = ln
                acc_ref[0:F32_SUBLANES, 0:LANES] += _zero_after(ln)


def _ffn_ln(x2d, w1, w3, w2, g, b, *, tm, ff_chunk=256):
    n = x2d.shape[0]
    tiles = n // tm
    return pl.pallas_call(
        functools.partial(_ffn_ln_body, ff_chunk=ff_chunk, tiles=tiles),
        grid=(tiles + 1,),
        in_specs=[pl.BlockSpec((tm, D_MODEL), lambda i: (jnp.minimum(i, tiles - 1), 0)),
                  _resident((D_MODEL, D_FF)), _resident((D_MODEL, D_FF)), _resident((D_FF, D_MODEL)),
                  _resident((1, D_MODEL)), _resident((1, D_MODEL))],
        out_specs=pl.BlockSpec((tm, D_MODEL), lambda i: (jnp.maximum(i - 1, 0), 0)),
        out_shape=jax.ShapeDtypeStruct((n, D_MODEL), F32),
        scratch_shapes=[pltpu.VMEM((tm, D_MODEL), F32), pltpu.VMEM((tm, D_MODEL), F32)],
        compiler_params=_params("arbitrary"),
        name="ffn_ln",
    )(x2d, w1, w3, w2, g, b)


PROJ_CHUNK = 512
QV_ROWS = ATTN_WIDTH + KV_WIDTH
QV_CHUNK = 256
TOK_WIDTH = 4 * HGRN_WIDTH + 2 * D_MODEL + KV_WIDTH


def _rope_lanes(t, c, s_lo, s_hi):
    return t * c + pltpu.roll(t, ROPE_HALF, axis=1) * s_hi + pltpu.roll(t, LANES - ROPE_HALF, axis=1) * s_lo


def _in_proj_body(x_ref, wt_ref, bt_ref, w_ref, b_ref, fa_ref, fb_ref, c_ref, slo_ref, shi_ref, cos_ref, sin_ref, ng_ref,
                  qt_ref, vt_ref, k_ref, gate_ref, yh_ref,
                  f_s, q_s, v_s, og_s, state_ref, qd_ref, upd_ref, sc_ref, dec_ref, *, tiles_per_seq):
    @pl.when(pl.program_id(0) % tiles_per_seq == 0)
    def _():
        state_ref[...] = jnp.zeros_like(state_ref)

    xb = x_ref[...].astype(BF16)
    tm = xb.shape[0]
    q_scale = HEAD_DIM ** -0.5 * LOG2_E

    def feature_major(j):
        cos_t, sin_t = cos_ref[...], sin_ref[...]
        rows = slice(j * QV_CHUNK, (j + 1) * QV_CHUNK)
        pt = lax.dot_general(wt_ref[rows, :], xb, (((1,), (1,)), ((), ())), preferred_element_type=F32)
        pt = pt + bt_ref[rows, :]
        if j * QV_CHUNK < ATTN_WIDTH:
            for h in range(QV_CHUNK // HEAD_DIM):
                r0 = h * HEAD_DIM
                t1 = pt[r0:r0 + ROPE_HALF]
                t2 = pt[r0 + ROPE_HALF:r0 + ROPE_DIM]
                rot = jnp.concatenate([t1 * cos_t - t2 * sin_t, t2 * cos_t + t1 * sin_t,
                                       pt[r0 + ROPE_DIM:r0 + HEAD_DIM]], axis=0)
                qt_ref[j * QV_CHUNK + r0:j * QV_CHUNK + r0 + HEAD_DIM, :] = (rot * q_scale).astype(BF16)
        else:
            vt_ref[...] = pt.astype(BF16)

    def token_major(lo):
        width = min(PROJ_CHUNK, TOK_WIDTH - lo)
        cols = slice(lo, lo + width)
        p = jnp.dot(xb, w_ref[:, cols], preferred_element_type=F32) + b_ref[:, cols]
        kind, off = _tok_kind(lo)
        dst = slice(off, off + width)
        def per_head(dst_ref, val):
            for u in range(width // HGRN_DK):
                dst_ref[(off + u * HGRN_DK) // HGRN_DK] = val[:, u * HGRN_DK:(u + 1) * HGRN_DK]

        if kind == "forget":
            per_head(f_s, fa_ref[:, dst] + fb_ref[:, dst] * jnp.tanh(p))
        elif kind == "hgrn_q":
            per_head(q_s, (p * jnp.tanh(p) + p).astype(BF16))
        elif kind == "hgrn_v":
            per_head(v_s, p.astype(BF16))
        elif kind == "hgrn_og":
            per_head(og_s, (p * jnp.tanh(p) + p).astype(BF16))
        elif kind in ("gate_a", "gate_h"):
            base = 0 if kind == "gate_a" else D_MODEL
            gate_ref[:, base + off:base + off + width] = (0.5 * jnp.tanh(p) + 0.5).astype(BF16)
        else:
            c, s_lo, s_hi = c_ref[...], slo_ref[...], shi_ref[...]
            for u in range(width // LANES):
                t = p[:, u * LANES:(u + 1) * LANES]
                k_ref[:, u * LANES:(u + 1) * LANES] = _rope_lanes(t, c, s_lo, s_hi).astype(BF16)

    hgrn_cols = 4 * HGRN_WIDTH
    for lo in range(0, hgrn_cols, PROJ_CHUNK):
        token_major(lo)
    rest = [functools.partial(token_major, lo) for lo in range(hgrn_cols, TOK_WIDTH, PROJ_CHUNK)]
    rest += [functools.partial(feature_major, j) for j in range(QV_ROWS // QV_CHUNK)]

    cz = HGRN_CHUNK
    n_chunks = tm // cz
    ti = lax.broadcasted_iota(jnp.int32, (cz, cz), 0)
    si = lax.broadcasted_iota(jnp.int32, (cz, cz), 1)
    causal = ti >= si
    ng = ng_ref[...]
    nt_dims = (((1,), (1,)), ((), ()))

    def gates(ci, heads=range(HGRN_HEADS)):
        rows = slice(ci * cz, (ci + 1) * cz)
        for h in heads:
            lanes = slice(h * HGRN_DK, (h + 1) * HGRN_DK)
            f = f_s[h, rows, :]
            k = 1.0 - f
            decay = _cumprod_rows(f)
            inv = 1.0 / decay
            d_last = decay[cz - 1:cz, :]
            q_dec = (q_s[h, rows, :].astype(F32) * decay).astype(BF16)
            k_inv = (k * inv).astype(BF16)
            sc = lax.dot_general(q_dec, k_inv, nt_dims, preferred_element_type=F32)
            sc_ref[h] = jnp.where(causal, sc, 0.0).astype(BF16)
            qd_ref[h] = q_dec
            k_end = (k * (d_last * inv)).astype(BF16)
            upd_ref[h] = lax.dot_general(v_s[h, rows, :], k_end, (((0,), (0,)), ((), ())),
                                         preferred_element_type=F32)
            dec_ref[:, lanes] = d_last

    def outputs(ci, heads=range(HGRN_HEADS)):
        rows = slice(ci * cz, (ci + 1) * cz)
        for h in heads:
            lanes = slice(h * HGRN_DK, (h + 1) * HGRN_DK)
            v = v_s[h, rows, :]
            st = state_ref[h]
            o = jnp.dot(sc_ref[h], v, preferred_element_type=F32)
            o += lax.dot_general(qd_ref[h], st.astype(BF16), nt_dims, preferred_element_type=F32)
            state_ref[h] = st * dec_ref[:, lanes] + upd_ref[h]
            o = o * lax.rsqrt(jnp.mean(o * o, axis=-1, keepdims=True) + RMS_EPS) * ng
            yh_ref[rows, lanes] = (o * og_s[h, rows, :].astype(F32)).astype(BF16)

    per_chunk = -(-len(rest) // n_chunks) + 1
    gates(0)
    for ci in range(n_chunks):
        for piece in rest[ci * per_chunk:(ci + 1) * per_chunk]:
            piece()
        for h in range(HGRN_HEADS):
            outputs(ci, [h])
            if ci + 1 < n_chunks:
                gates(ci + 1, [h])
    for piece in rest[n_chunks * per_chunk:]:
        piece()


_TOK_GROUPS = (("forget", HGRN_WIDTH), ("hgrn_q", HGRN_WIDTH), ("hgrn_v", HGRN_WIDTH), ("hgrn_og", HGRN_WIDTH),
               ("gate_a", D_MODEL), ("gate_h", D_MODEL), ("rope_k", KV_WIDTH))
_TANH_KINDS = ("forget", "hgrn_q", "hgrn_og", "gate_a", "gate_h")


def _tok_kind(col):
    for kind, width in _TOK_GROUPS:
        if col < width:
            return kind, col
        col -= width
    raise ValueError(col)


def _tok_half_scale():
    return jnp.concatenate([jnp.full((w,), 0.5 if kind in _TANH_KINDS else 1.0, F32)
                            for kind, w in _TOK_GROUPS])[None, :]


def _in_proj(x2d, w_in, b_in, lb, norm_g, rope, *, tm, seq):
    n = x2d.shape[0]
    tiles_per_seq = seq // tm
    q_end, k_end, v_end = ATTN_WIDTH, ATTN_WIDTH + KV_WIDTH, ATTN_WIDTH + 2 * KV_WIDTH
    w_t = jnp.concatenate([w_in[:, :q_end], w_in[:, k_end:v_end]], axis=1).T.astype(BF16)
    b_t = jnp.concatenate([b_in[:q_end], b_in[k_end:v_end]])[:, None]
    half = _tok_half_scale()
    w_tok = (jnp.concatenate([w_in[:, v_end:], w_in[:, q_end:k_end]], axis=1) * half).astype(BF16)
    b_tok = jnp.concatenate([b_in[v_end:], b_in[q_end:k_end]])[None, :] * half
    f_a = lb + 0.5 * (1.0 - lb)
    f_b = 0.5 * (1.0 - lb)
    c, s_lo, s_hi, cos_t, sin_t = rope
    row = lambda w: pl.BlockSpec((tm, w), lambda i: (i, 0))
    col = lambda r: pl.BlockSpec((r, tm), lambda i: (0, i))
    tab = pl.BlockSpec((tm, LANES), lambda i: (i % tiles_per_seq, 0))
    tab_t = pl.BlockSpec((ROPE_HALF, tm), lambda i: (0, i % tiles_per_seq))
    return pl.pallas_call(
        functools.partial(_in_proj_body, tiles_per_seq=tiles_per_seq),
        grid=(n // tm,),
        in_specs=[row(D_MODEL), _resident((QV_ROWS, D_MODEL)), _resident((QV_ROWS, 1)),
                  _resident((D_MODEL, TOK_WIDTH)), _resident((1, TOK_WIDTH)),
                  _resident((1, HGRN_WIDTH)), _resident((1, HGRN_WIDTH)), tab, tab, tab, tab_t, tab_t,
                  _resident((1, HGRN_DV))],
        out_specs=[col(ATTN_WIDTH), col(KV_WIDTH), row(KV_WIDTH), row(2 * D_MODEL), row(HGRN_WIDTH)],
        out_shape=[jax.ShapeDtypeStruct((ATTN_WIDTH, n), BF16),
                   jax.ShapeDtypeStruct((KV_WIDTH, n), BF16),
                   jax.ShapeDtypeStruct((n, KV_WIDTH), BF16),
                   jax.ShapeDtypeStruct((n, 2 * D_MODEL), BF16),
                   jax.ShapeDtypeStruct((n, HGRN_WIDTH), BF16)],
        scratch_shapes=[pltpu.VMEM((HGRN_HEADS, tm, HGRN_DK), F32),
                        pltpu.VMEM((HGRN_HEADS, tm, HGRN_DK), BF16),
                        pltpu.VMEM((HGRN_HEADS, tm, HGRN_DV), BF16),
                        pltpu.VMEM((HGRN_HEADS, tm, HGRN_DV), BF16),
                        pltpu.VMEM((HGRN_HEADS, HGRN_DV, HGRN_DK), F32),
                        pltpu.VMEM((HGRN_HEADS, HGRN_CHUNK, HGRN_DK), BF16),
                        pltpu.VMEM((HGRN_HEADS, HGRN_DV, HGRN_DK), F32),
                        pltpu.VMEM((HGRN_HEADS, HGRN_CHUNK, HGRN_CHUNK), BF16),
                        pltpu.VMEM((1, HGRN_WIDTH), F32)],
        compiler_params=_params("arbitrary"),
        name="in_proj",
    )(x2d, w_t, b_t, w_tok, b_tok, f_a, f_b, c, s_lo, s_hi, cos_t, sin_t, norm_g)


def _rope_tables(seq):
    pos = jnp.arange(seq, dtype=F32)
    inv_freq = ROPE_THETA ** (-jnp.arange(0, ROPE_DIM, 2, dtype=F32) / ROPE_DIM)
    ang = pos[:, None] * inv_freq[None, :]
    cos, sin = jnp.cos(ang), jnp.sin(ang)
    ones = jnp.ones((seq, HEAD_DIM - ROPE_DIM), F32)
    c = jnp.concatenate([cos, cos, ones], axis=1)
    s_lo = jnp.concatenate([-sin, jnp.zeros((seq, HEAD_DIM - ROPE_HALF), F32)], axis=1)
    s_hi = jnp.concatenate([jnp.zeros((seq, ROPE_HALF), F32), sin, 0.0 * ones], axis=1)
    rep = LANES // HEAD_DIM
    return tuple(jnp.tile(t, (1, rep)) for t in (c, s_lo, s_hi)) + (cos.T, sin.T)


ATTN_TILE_BLOCKS = 8
MIX_CHUNK = 256


def _attn_mix_body(sink_ref, x_ref, qt_ref, k_ref, kh_ref, vt_ref, vth_ref, yh_ref, ga_ref, gh_ref,
                   wpa_ref, wph_ref, wo_ref, g_ref, b_ref, o_ref, y_ref, yat_ref, *, tiles, tiles_per_seq):
    step = pl.program_id(0)

    @pl.when(step == 0)
    def _():
        y_ref[...] = jnp.zeros_like(y_ref)

    @pl.when(step == tiles)
    def _():
        o_ref[...] = _layer_norm(y_ref[...], g_ref[...], b_ref[...])

    @pl.when(step < tiles)
    def _():
        _attn_mix_tile(sink_ref, x_ref, qt_ref, k_ref, kh_ref, vt_ref, vth_ref, yh_ref, ga_ref, gh_ref,
                       wpa_ref, wph_ref, wo_ref, g_ref, b_ref, o_ref, y_ref, yat_ref,
                       first_tile=step % tiles_per_seq == 0)


def _attn_mix_tile(sink_ref, x_ref, qt_ref, k_ref, kh_ref, vt_ref, vth_ref, yh_ref, ga_ref, gh_ref,
                   wpa_ref, wph_ref, wo_ref, g_ref, b_ref, o_ref, y_ref, yat_ref, *, first_tile):
    halo_bias = jnp.where(first_tile, NEG_INF, 0.0).astype(F32)
    wide = GQA_GROUP * ATTN_BLOCK
    key_i = lax.broadcasted_iota(jnp.int32, (ATTN_BLOCK, wide), 0)
    qry_i = lax.broadcasted_iota(jnp.int32, (ATTN_BLOCK, wide), 1) % ATTN_BLOCK
    from_prev = key_i > qry_i
    keep_prev = from_prev.astype(BF16)
    keep_cur = 1.0 - keep_prev
    zero_half = jnp.zeros((HEAD_DIM, ATTN_BLOCK), BF16)
    units = [(qb, kh) for qb in range(ATTN_TILE_BLOCKS) for kh in range(N_KV_HEADS)]

    def heads_of(kh):
        return [kh * GQA_GROUP + g for g in range(GQA_GROUP)]

    def toks_of(qb):
        return slice(qb * ATTN_BLOCK, (qb + 1) * ATTN_BLOCK)

    def scores(qb, kh):
        toks = toks_of(qb)
        k_prev = kh_ref[...] if qb == 0 else k_ref[(qb - 1) * ATTN_BLOCK:qb * ATTN_BLOCK, :]
        keys = jnp.concatenate([k_prev, k_ref[toks, :]], axis=0)
        pair = slice((kh // 2) * LANES, (kh // 2 + 1) * LANES)
        cols = []
        for h in heads_of(kh):
            qh = qt_ref[h * HEAD_DIM:(h + 1) * HEAD_DIM, toks]
            cols.append(jnp.concatenate([qh, zero_half] if kh % 2 == 0 else [zero_half, qh], axis=0))
        q_rhs = jnp.concatenate(cols, axis=1)
        return jnp.dot(keys[:, pair], q_rhs, preferred_element_type=F32)

    def finish(qb, kh, s):
        toks = toks_of(qb)
        heads = heads_of(kh)
        s_prev = s[:ATTN_BLOCK]
        if qb == 0:
            s_prev = s_prev + halo_bias
        c = jnp.where(from_prev, s_prev, s[ATTN_BLOCK:])
        sink = jnp.concatenate([jnp.full((1, ATTN_BLOCK), sink_ref[h] * LOG2_E, F32) for h in heads], axis=1)
        m = jnp.maximum(jnp.max(c, axis=0, keepdims=True), sink)
        p = jnp.exp2(c - m)
        denom = jnp.sum(p, axis=0, keepdims=True) + jnp.exp2(sink - m)
        pb = p.astype(BF16)
        p_both = jnp.concatenate([pb * keep_prev, pb * keep_cur], axis=0)
        vt_prev = vth_ref[...] if qb == 0 else vt_ref[:, (qb - 1) * ATTN_BLOCK:qb * ATTN_BLOCK]
        vt_both = jnp.concatenate([vt_prev[kh * HEAD_DIM:(kh + 1) * HEAD_DIM, :],
                                   vt_ref[kh * HEAD_DIM:(kh + 1) * HEAD_DIM, toks]], axis=1)
        ot = jnp.dot(vt_both, p_both, preferred_element_type=F32) * (1.0 / denom)
        for g, h in enumerate(heads):
            yat_ref[h * HEAD_DIM:(h + 1) * HEAD_DIM, toks] = ot[:, g * ATTN_BLOCK:(g + 1) * ATTN_BLOCK].astype(BF16)

    tm = x_ref.shape[0]
    n_chunks = D_MODEL // MIX_CHUNK
    chunk_cols = [slice(c * MIX_CHUNK, (c + 1) * MIX_CHUNK) for c in range(n_chunks)]
    yh = yh_ref[...]

    ph = []
    units_per_chunk = len(units) // n_chunks
    ahead = 2
    pending = [scores(*u) for u in units[:ahead]]
    for i, unit in enumerate(units):
        s_cur = pending.pop(0)
        if i + ahead < len(units):
            pending.append(scores(*units[i + ahead]))
        finish(*unit, s_cur)
        if i % units_per_chunk == 0:
            ph.append(jnp.dot(yh, wph_ref[:, chunk_cols[i // units_per_chunk]], preferred_element_type=F32))

    slab_rows = tm // LN_SLABS
    slabs_per_chunk = LN_SLABS // n_chunks
    yat = yat_ref[...]
    merged = []
    for c, cols in enumerate(chunk_cols):
        pa = lax.dot_general(yat, wpa_ref[:, cols], (((0,), (0,)), ((), ())), preferred_element_type=F32)
        m = ga_ref[:, cols].astype(F32) * pa + gh_ref[:, cols].astype(F32) * ph[c]
        zero = jnp.zeros((F32_SUBLANES, LANES), F32)
        for sl in range(c * slabs_per_chunk, (c + 1) * slabs_per_chunk):
            rows = slice(sl * slab_rows, (sl + 1) * slab_rows)
            ln = _layer_norm(y_ref[rows, :], g_ref[...], b_ref[...])
            o_ref[rows, :] = ln
            zero = zero + _zero_after(ln)
        top = m[0:F32_SUBLANES] + jnp.concatenate([zero] * (MIX_CHUNK // LANES), axis=1)
        merged.append(jnp.concatenate([top, m[F32_SUBLANES:]], axis=0).astype(BF16))
    merged = jnp.concatenate(merged, axis=1)
    for cols in chunk_cols:
        mix = jnp.dot(merged, wo_ref[:, cols], preferred_element_type=F32)
        y_ref[:, cols] = DEEPNORM_ALPHA * x_ref[:, cols] + mix


def _attn_mix(sinks, x2d, qt, k, vt, y_hgrn, gates, w_pa, w_ph, w_out, g, b, *, seq):
    n = x2d.shape[0]
    tm = ATTN_TILE_BLOCKS * ATTN_BLOCK
    tiles = n // tm
    cur = lambda i: jnp.minimum(i, tiles - 1)
    halo = lambda i: jnp.maximum(cur(i) * ATTN_TILE_BLOCKS - 1, 0)
    row = lambda w, col=0: pl.BlockSpec((tm, w), lambda i: (cur(i), col))
    sq = _resident((D_MODEL, D_MODEL))
    return pl.pallas_call(
        functools.partial(_attn_mix_body, tiles=tiles, tiles_per_seq=seq // tm),
        grid=(tiles + 1,),
        in_specs=[pl.BlockSpec(memory_space=pltpu.SMEM),
                  row(D_MODEL),
                  pl.BlockSpec((ATTN_WIDTH, tm), lambda i: (0, cur(i))),
                  row(KV_WIDTH),
                  pl.BlockSpec((ATTN_BLOCK, KV_WIDTH), lambda i: (halo(i), 0)),
                  pl.BlockSpec((KV_WIDTH, tm), lambda i: (0, cur(i))),
                  pl.BlockSpec((KV_WIDTH, ATTN_BLOCK), lambda i: (0, halo(i))),
                  row(HGRN_WIDTH), row(D_MODEL, 0), row(D_MODEL, 1), sq, sq, sq,
                  _resident((1, D_MODEL)), _resident((1, D_MODEL))],
        out_specs=pl.BlockSpec((tm, D_MODEL), lambda i: (jnp.maximum(i - 1, 0), 0)),
        out_shape=jax.ShapeDtypeStruct((n, D_MODEL), F32),
        scratch_shapes=[pltpu.VMEM((tm, D_MODEL), F32),
                        pltpu.VMEM((ATTN_WIDTH, tm), BF16)],
        compiler_params=_params("arbitrary"),
        name="attn_mix",
    )(sinks, x2d, qt, k, k, vt, vt, y_hgrn, gates, gates, w_pa, w_ph, w_out, g, b)


def _cumprod_rows(f):
    row = lax.broadcasted_iota(jnp.int32, (F32_SUBLANES, LANES), 0)
    blocks, carry = [], None
    for j in range(f.shape[0] // F32_SUBLANES):
        x = f[j * F32_SUBLANES:(j + 1) * F32_SUBLANES]
        shift = 1
        while shift < F32_SUBLANES:
            x = x * jnp.where(row >= shift, pltpu.roll(x, shift, axis=0), 1.0)
            shift *= 2
        if carry is not None:
            x = x * carry
        carry = x[F32_SUBLANES - 1:F32_SUBLANES, :]
        blocks.append(x)
    return jnp.concatenate(blocks, axis=0)


def _tile(n, want):
    t = min(n, want)
    assert n % t == 0, (n, t)
    return t


def kernel(x, ln1_g, ln1_b, ffn1_w1, ffn1_w3, ffn1_w2, ln2_g, ln2_b, w_in, b_in, attn_sinks, hgrn_lb_logits,
           hgrn_norm_g, w_proj_attn, w_proj_hgrn, w_out, ln3_g, ln3_b, ffn2_w1, ffn2_w3, ffn2_w2):
    batch, seq, d = x.shape
    assert d == D_MODEL and seq % (ATTN_TILE_BLOCKS * ATTN_BLOCK) == 0
    n = batch * seq
    tm = _tile(seq, 512)
    lb_all = jnp.cumsum(jax.nn.softmax(hgrn_lb_logits.astype(F32), axis=0), axis=0)
    rope = _rope_tables(seq)
    bf = lambda w: w.astype(BF16)
    h = x.reshape(n, d)
    for l in range(DEPTH):
        h = _ffn_ln(h, bf(0.5 * ffn1_w1[l]), bf(ffn1_w3[l]), bf(0.5 * ffn1_w2[l]), ln1_g[l:l + 1], ln1_b[l:l + 1], tm=tm)
        qt, vt, k, gates, y_hgrn = _in_proj(h, w_in[l], b_in[l], lb_all[l:l + 1], hgrn_norm_g[l:l + 1], rope,
                                            tm=tm, seq=seq)
        h = _attn_mix(attn_sinks[l], h, qt, k, vt, y_hgrn, gates, bf(w_proj_attn[l]), bf(w_proj_hgrn[l]),
                      bf(w_out[l]), ln2_g[l:l + 1], ln2_b[l:l + 1], seq=seq)
        h = _ffn_ln(h, bf(0.5 * ffn2_w1[l]), bf(ffn2_w3[l]), bf(0.5 * ffn2_w2[l]), ln3_g[l:l + 1], ln3_b[l:l + 1], tm=tm)
    return h.reshape(batch, seq, d)
```

```python
import functools

import jax
import jax.numpy as jnp
from jax import lax
from jax.experimental import pallas as pl
from jax.experimental.pallas import tpu as pltpu

D_MODEL = 1024
DEPTH = 1
N_Q_HEADS = 16
N_KV_HEADS = 4
HEAD_DIM = 64
GQA_GROUP = N_Q_HEADS // N_KV_HEADS
WINDOW = 128
ATTN_BLOCK = 128
ROPE_THETA = 500000.0
ROPE_DIM = HEAD_DIM // 4
ROPE_HALF = ROPE_DIM // 2
HGRN_HEADS = 8
HGRN_DK = 128
HGRN_DV = 128
HGRN_CHUNK = 64
D_FF = 2816
ATTN_WIDTH = N_Q_HEADS * HEAD_DIM
KV_WIDTH = N_KV_HEADS * HEAD_DIM
HGRN_WIDTH = HGRN_HEADS * HGRN_DK
D_IN = ATTN_WIDTH + 2 * KV_WIDTH + 4 * HGRN_WIDTH + 2 * D_MODEL
DEEPNORM_ALPHA = (2 * DEPTH) ** 0.25
LN_EPS = 1e-5
RMS_EPS = 1e-6
NEG_INF = -1e30
LOG2_E = 1.4426950408889634

LANES = 128
F32_SUBLANES = 8
VMEM_LIMIT_BYTES = 56 * 1024 * 1024

F32 = jnp.float32
BF16 = jnp.bfloat16

assert WINDOW == ATTN_BLOCK and ROPE_HALF == F32_SUBLANES


def _layer_norm(y, g, b):
    mu = jnp.mean(y, axis=-1, keepdims=True)
    yc = y - mu
    var = jnp.mean(yc * yc, axis=-1, keepdims=True)
    return yc * lax.rsqrt(var + LN_EPS) * g + b


def _resident(shape):
    return pl.BlockSpec(shape, lambda *_: (0,) * len(shape), pipeline_mode=pl.Buffered(1))


def _params(*semantics):
    return pltpu.CompilerParams(dimension_semantics=semantics, vmem_limit_bytes=VMEM_LIMIT_BYTES)


LN_SLABS = 8


def _zero_after(x):
    bits = pltpu.bitcast(x, jnp.uint32)
    tiles = [bits[r:r + F32_SUBLANES, c:c + LANES]
             for r in range(0, x.shape[0], F32_SUBLANES) for c in range(0, x.shape[1], LANES)]
    folded = functools.reduce(jnp.bitwise_or, tiles)
    return pltpu.bitcast((folded >> 16) >> 16, F32)


def _ffn_ln_body(x_ref, w1_ref, w3_ref, w2_ref, g_ref, b_ref, o_ref, acc_ref, y_ref, *, ff_chunk, tiles):
    step = pl.program_id(0)

    @pl.when(step == 0)
    def _():
        y_ref[...] = jnp.zeros_like(y_ref)

    @pl.when(step == tiles)
    def _():
        o_ref[...] = _layer_norm(y_ref[...], g_ref[...], b_ref[...])

    @pl.when(step < tiles)
    def _():
        tm = x_ref.shape[0]
        n_slabs, slab_rows = LN_SLABS, tm // LN_SLABS
        n_chunks = D_FF // ff_chunk
        assert n_slabs <= n_chunks - 1
        x = x_ref[...]
        xb = x.astype(BF16)
        for c in range(n_chunks):
            cols = slice(c * ff_chunk, (c + 1) * ff_chunk)
            h1 = jnp.dot(xb, w1_ref[:, cols], preferred_element_type=F32)
            h3 = jnp.dot(xb, w3_ref[:, cols], preferred_element_type=F32)
            a = ((h1 * jnp.tanh(h1) + h1) * h3).astype(BF16)
            part = jnp.dot(a, w2_ref[cols, :], preferred_element_type=F32)
            if c == 0:
                acc_ref[...] = DEEPNORM_ALPHA * x + part
            elif c < n_chunks - 1:
                acc_ref[...] += part
            else:
                y_ref[...] = acc_ref[...] + part
            if c < n_slabs:
                rows = slice(c * slab_rows, (c + 1) * slab_rows)
                ln = _layer_norm(y_ref[rows, :], g_ref[...], b_ref[...])
                o_ref[rows, :] = ln
                acc_ref[0:F32_SUBLANES, 0:LANES] += _zero_after(ln)


def _ffn_ln(x2d, w1, w3, w2, g, b, *, tm, ff_chunk=256):
    n = x2d.shape[0]
    tiles = n // tm
    return pl.pallas_call(
        functools.partial(_ffn_ln_body, ff_chunk=ff_chunk, tiles=tiles),
        grid=(tiles + 1,),
        in_specs=[pl.BlockSpec((tm, D_MODEL), lambda i: (jnp.minimum(i, tiles - 1), 0)),
                  _resident((D_MODEL, D_FF)), _resident((D_MODEL, D_FF)), _resident((D_FF, D_MODEL)),
                  _resident((1, D_MODEL)), _resident((1, D_MODEL))],
        out_specs=pl.BlockSpec((tm, D_MODEL), lambda i: (jnp.maximum(i - 1, 0), 0)),
        out_shape=jax.ShapeDtypeStruct((n, D_MODEL), F32),
        scratch_shapes=[pltpu.VMEM((tm, D_MODEL), F32), pltpu.VMEM((tm, D_MODEL), F32)],
        compiler_params=_params("arbitrary"),
        name="ffn_ln",
    )(x2d, w1, w3, w2, g, b)


PROJ_CHUNK = 512
QV_ROWS = ATTN_WIDTH + KV_WIDTH
QV_CHUNK = 256
TOK_WIDTH = 4 * HGRN_WIDTH + 2 * D_MODEL + KV_WIDTH


def _rope_lanes(t, c, s_lo, s_hi):
    return t * c + pltpu.roll(t, ROPE_HALF, axis=1) * s_hi + pltpu.roll(t, LANES - ROPE_HALF, axis=1) * s_lo


def _in_proj_body(x_ref, wt_ref, bt_ref, w_ref, b_ref, fa_ref, fb_ref, c_ref, slo_ref, shi_ref, cos_ref, sin_ref, ng_ref,
                  qt_ref, vt_ref, k_ref, gate_ref, yh_ref,
                  f_s, q_s, v_s, og_s, state_ref, qd_ref, upd_ref, sc_ref, dec_ref, *, tiles_per_seq):
    @pl.when(pl.program_id(0) % tiles_per_seq == 0)
    def _():
        state_ref[...] = jnp.zeros_like(state_ref)

    xb = x_ref[...].astype(BF16)
    tm = xb.shape[0]
    q_scale = HEAD_DIM ** -0.5 * LOG2_E

    def feature_major(j):
        cos_t, sin_t = cos_ref[...], sin_ref[...]
        rows = slice(j * QV_CHUNK, (j + 1) * QV_CHUNK)
        pt = lax.dot_general(wt_ref[rows, :], xb, (((1,), (1,)), ((), ())), preferred_element_type=F32)
        pt = pt + bt_ref[rows, :]
        if j * QV_CHUNK < ATTN_WIDTH:
            for h in range(QV_CHUNK // HEAD_DIM):
                r0 = h * HEAD_DIM
                t1 = pt[r0:r0 + ROPE_HALF]
                t2 = pt[r0 + ROPE_HALF:r0 + ROPE_DIM]
                rot = jnp.concatenate([t1 * cos_t - t2 * sin_t, t2 * cos_t + t1 * sin_t,
                                       pt[r0 + ROPE_DIM:r0 + HEAD_DIM]], axis=0)
                qt_ref[j * QV_CHUNK + r0:j * QV_CHUNK + r0 + HEAD_DIM, :] = (rot * q_scale).astype(BF16)
        else:
            vt_ref[...] = pt.astype(BF16)

    def token_major(lo):
        width = min(PROJ_CHUNK, TOK_WIDTH - lo)
        cols = slice(lo, lo + width)
        p = jnp.dot(xb, w_ref[:, cols], preferred_element_type=F32) + b_ref[:, cols]
        kind, off = _tok_kind(lo)
        dst = slice(off, off + width)
        def per_head(dst_ref, val):
            for u in range(width // HGRN_DK):
                dst_ref[(off + u * HGRN_DK) // HGRN_DK] = val[:, u * HGRN_DK:(u + 1) * HGRN_DK]

        if kind == "forget":
            per_head(f_s, fa_ref[:, dst] + fb_ref[:, dst] * jnp.tanh(p))
        elif kind == "hgrn_q":
            per_head(q_s, (p * jnp.tanh(p) + p).astype(BF16))
        elif kind == "hgrn_v":
            per_head(v_s, p.astype(BF16))
        elif kind == "hgrn_og":
            per_head(og_s, (p * jnp.tanh(p) + p).astype(BF16))
        elif kind in ("gate_a", "gate_h"):
            base = 0 if kind == "gate_a" else D_MODEL
            gate_ref[:, base + off:base + off + width] = (0.5 * jnp.tanh(p) + 0.5).astype(BF16)
        else:
            c, s_lo, s_hi = c_ref[...], slo_ref[...], shi_ref[...]
            for u in range(width // LANES):
                t = p[:, u * LANES:(u + 1) * LANES]
                k_ref[:, u * LANES:(u + 1) * LANES] = _rope_lanes(t, c, s_lo, s_hi).astype(BF16)

    hgrn_cols = 4 * HGRN_WIDTH
    for lo in range(0, hgrn_cols, PROJ_CHUNK):
        token_major(lo)
    rest = [functools.partial(token_major, lo) for lo in range(hgrn_cols, TOK_WIDTH, PROJ_CHUNK)]
    rest += [functools.partial(feature_major, j) for j in range(QV_ROWS // QV_CHUNK)]

    cz = HGRN_CHUNK
    n_chunks = tm // cz
    ti = lax.broadcasted_iota(jnp.int32, (cz, cz), 0)
    si = lax.broadcasted_iota(jnp.int32, (cz, cz), 1)
    causal = ti >= si
    ng = ng_ref[...]
    nt_dims = (((1,), (1,)), ((), ()))

    def gates(ci, heads=range(HGRN_HEADS)):
        rows = slice(ci * cz, (ci + 1) * cz)
        for h in heads:
            lanes = slice(h * HGRN_DK, (h + 1) * HGRN_DK)
            f = f_s[h, rows, :]
            k = 1.0 - f
            decay = _cumprod_rows(f)
            inv = 1.0 / decay
            d_last = decay[cz - 1:cz, :]
            q_dec = (q_s[h, rows, :].astype(F32) * decay).astype(BF16)
            k_inv = (k * inv).astype(BF16)
            sc = lax.dot_general(q_dec, k_inv, nt_dims, preferred_element_type=F32)
            sc_ref[h] = jnp.where(causal, sc, 0.0).astype(BF16)
            qd_ref[h] = q_dec
            k_end = (k * (d_last * inv)).astype(BF16)
            upd_ref[h] = lax.dot_general(v_s[h, rows, :], k_end, (((0,), (0,)), ((), ())),
                                         preferred_element_type=F32)
            dec_ref[:, lanes] = d_last

    def outputs(ci, heads=range(HGRN_HEADS)):
        rows = slice(ci * cz, (ci + 1) * cz)
        for h in heads:
            lanes = slice(h * HGRN_DK, (h + 1) * HGRN_DK)
            v = v_s[h, rows, :]
            st = state_ref[h]
            o = jnp.dot(sc_ref[h], v, preferred_element_type=F32)
            o += lax.dot_general(qd_ref[h], st.astype(BF16), nt_dims, preferred_element_type=F32)
            state_ref[h] = st * dec_ref[:, lanes] + upd_ref[h]
            o = o * lax.rsqrt(jnp.mean(o * o, axis=-1, keepdims=True) + RMS_EPS) * ng
            yh_ref[rows, lanes] = (o * og_s[h, rows, :].astype(F32)).astype(BF16)

    per_chunk = -(-len(rest) // n_chunks) + 2
    gates(0)
    for ci in range(n_chunks):
        for piece in rest[ci * per_chunk:(ci + 1) * per_chunk]:
            piece()
        for h in range(HGRN_HEADS):
            outputs(ci, [h])
            if ci + 1 < n_chunks:
                gates(ci + 1, [h])
    for piece in rest[n_chunks * per_chunk:]:
        piece()


_TOK_GROUPS = (("forget", HGRN_WIDTH), ("hgrn_q", HGRN_WIDTH), ("hgrn_v", HGRN_WIDTH), ("hgrn_og", HGRN_WIDTH),
               ("gate_a", D_MODEL), ("gate_h", D_MODEL), ("rope_k", KV_WIDTH))
_TANH_KINDS = ("forget", "hgrn_q", "hgrn_og", "gate_a", "gate_h")


def _tok_kind(col):
    for kind, width in _TOK_GROUPS:
        if col < width:
            return kind, col
        col -= width
    raise ValueError(col)


def _tok_half_scale():
    return jnp.concatenate([jnp.full((w,), 0.5 if kind in _TANH_KINDS else 1.0, F32)
                            for kind, w in _TOK_GROUPS])[None, :]


def _in_proj(x2d, w_in, b_in, lb, norm_g, rope, *, tm, seq):
    n = x2d.shape[0]
    tiles_per_seq = seq // tm
    q_end, k_end, v_end = ATTN_WIDTH, ATTN_WIDTH + KV_WIDTH, ATTN_WIDTH + 2 * KV_WIDTH
    w_t = jnp.concatenate([w_in[:, :q_end], w_in[:, k_end:v_end]], axis=1).T.astype(BF16)
    b_t = jnp.concatenate([b_in[:q_end], b_in[k_end:v_end]])[:, None]
    half = _tok_half_scale()
    w_tok = (jnp.concatenate([w_in[:, v_end:], w_in[:, q_end:k_end]], axis=1) * half).astype(BF16)
    b_tok = jnp.concatenate([b_in[v_end:], b_in[q_end:k_end]])[None, :] * half
    f_a = lb + 0.5 * (1.0 - lb)
    f_b = 0.5 * (1.0 - lb)
    c, s_lo, s_hi, cos_t, sin_t = rope
    row = lambda w: pl.BlockSpec((tm, w), lambda i: (i, 0))
    col = lambda r: pl.BlockSpec((r, tm), lambda i: (0, i))
    tab = pl.BlockSpec((tm, LANES), lambda i: (i % tiles_per_seq, 0))
    tab_t = pl.BlockSpec((ROPE_HALF, tm), lambda i: (0, i % tiles_per_seq))
    return pl.pallas_call(
        functools.partial(_in_proj_body, tiles_per_seq=tiles_per_seq),
        grid=(n // tm,),
        in_specs=[row(D_MODEL), _resident((QV_ROWS, D_MODEL)), _resident((QV_ROWS, 1)),
                  _resident((D_MODEL, TOK_WIDTH)), _resident((1, TOK_WIDTH)),
                  _resident((1, HGRN_WIDTH)), _resident((1, HGRN_WIDTH)), tab, tab, tab, tab_t, tab_t,
                  _resident((1, HGRN_DV))],
        out_specs=[col(ATTN_WIDTH), col(KV_WIDTH), row(KV_WIDTH), row(2 * D_MODEL), row(HGRN_WIDTH)],
        out_shape=[jax.ShapeDtypeStruct((ATTN_WIDTH, n), BF16),
                   jax.ShapeDtypeStruct((KV_WIDTH, n), BF16),
                   jax.ShapeDtypeStruct((n, KV_WIDTH), BF16),
                   jax.ShapeDtypeStruct((n, 2 * D_MODEL), BF16),
                   jax.ShapeDtypeStruct((n, HGRN_WIDTH), BF16)],
        scratch_shapes=[pltpu.VMEM((HGRN_HEADS, tm, HGRN_DK), F32),
                        pltpu.VMEM((HGRN_HEADS, tm, HGRN_DK), BF16),
                        pltpu.VMEM((HGRN_HEADS, tm, HGRN_DV), BF16),
                        pltpu.VMEM((HGRN_HEADS, tm, HGRN_DV), BF16),
                        pltpu.VMEM((HGRN_HEADS, HGRN_DV, HGRN_DK), F32),
                        pltpu.VMEM((HGRN_HEADS, HGRN_CHUNK, HGRN_DK), BF16),
                        pltpu.VMEM((HGRN_HEADS, HGRN_DV, HGRN_DK), F32),
                        pltpu.VMEM((HGRN_HEADS, HGRN_CHUNK, HGRN_CHUNK), BF16),
                        pltpu.VMEM((1, HGRN_WIDTH), F32)],
        compiler_params=_params("arbitrary"),
        name="in_proj",
    )(x2d, w_t, b_t, w_tok, b_tok, f_a, f_b, c, s_lo, s_hi, cos_t, sin_t, norm_g)


def _rope_tables(seq):
    pos = jnp.arange(seq, dtype=F32)
    inv_freq = ROPE_THETA ** (-jnp.arange(0, ROPE_DIM, 2, dtype=F32) / ROPE_DIM)
    ang = pos[:, None] * inv_freq[None, :]
    cos, sin = jnp.cos(ang), jnp.sin(ang)
    ones = jnp.ones((seq, HEAD_DIM - ROPE_DIM), F32)
    c = jnp.concatenate([cos, cos, ones], axis=1)
    s_lo = jnp.concatenate([-sin, jnp.zeros((seq, HEAD_DIM - ROPE_HALF), F32)], axis=1)
    s_hi = jnp.concatenate([jnp.zeros((seq, ROPE_HALF), F32), sin, 0.0 * ones], axis=1)
    rep = LANES // HEAD_DIM
    return tuple(jnp.tile(t, (1, rep)) for t in (c, s_lo, s_hi)) + (cos.T, sin.T)


ATTN_TILE_BLOCKS = 8
MIX_CHUNK = 256


def _attn_mix_body(sink_ref, x_ref, qt_ref, k_ref, kh_ref, vt_ref, vth_ref, yh_ref, ga_ref, gh_ref,
                   wpa_ref, wph_ref, wo_ref, g_ref, b_ref, o_ref, y_ref, yat_ref, *, tiles, tiles_per_seq):
    step = pl.program_id(0)

    @pl.when(step == 0)
    def _():
        y_ref[...] = jnp.zeros_like(y_ref)

    @pl.when(step == tiles)
    def _():
        o_ref[...] = _layer_norm(y_ref[...], g_ref[...], b_ref[...])

    @pl.when(step < tiles)
    def _():
        _attn_mix_tile(sink_ref, x_ref, qt_ref, k_ref, kh_ref, vt_ref, vth_ref, yh_ref, ga_ref, gh_ref,
                       wpa_ref, wph_ref, wo_ref, g_ref, b_ref, o_ref, y_ref, yat_ref,
                       first_tile=step % tiles_per_seq == 0)


def _attn_mix_tile(sink_ref, x_ref, qt_ref, k_ref, kh_ref, vt_ref, vth_ref, yh_ref, ga_ref, gh_ref,
                   wpa_ref, wph_ref, wo_ref, g_ref, b_ref, o_ref, y_ref, yat_ref, *, first_tile):
    halo_bias = jnp.where(first_tile, NEG_INF, 0.0).astype(F32)
    wide = GQA_GROUP * ATTN_BLOCK
    key_i = lax.broadcasted_iota(jnp.int32, (ATTN_BLOCK, wide), 0)
    qry_i = lax.broadcasted_iota(jnp.int32, (ATTN_BLOCK, wide), 1) % ATTN_BLOCK
    from_prev = key_i > qry_i
    keep_prev = from_prev.astype(BF16)
    keep_cur = 1.0 - keep_prev
    zero_half = jnp.zeros((HEAD_DIM, ATTN_BLOCK), BF16)
    units = [(qb, kh) for qb in range(ATTN_TILE_BLOCKS) for kh in range(N_KV_HEADS)]

    def heads_of(kh):
        return [kh * GQA_GROUP + g for g in range(GQA_GROUP)]

    def toks_of(qb):
        return slice(qb * ATTN_BLOCK, (qb + 1) * ATTN_BLOCK)

    def scores(qb, kh):
        toks = toks_of(qb)
        k_prev = kh_ref[...] if qb == 0 else k_ref[(qb - 1) * ATTN_BLOCK:qb * ATTN_BLOCK, :]
        keys = jnp.concatenate([k_prev, k_ref[toks, :]], axis=0)
        pair = slice((kh // 2) * LANES, (kh // 2 + 1) * LANES)
        cols = []
        for h in heads_of(kh):
            qh = qt_ref[h * HEAD_DIM:(h + 1) * HEAD_DIM, toks]
            cols.append(jnp.concatenate([qh, zero_half] if kh % 2 == 0 else [zero_half, qh], axis=0))
        q_rhs = jnp.concatenate(cols, axis=1)
        return jnp.dot(keys[:, pair], q_rhs, preferred_element_type=F32)

    def finish(qb, kh, s):
        toks = toks_of(qb)
        heads = heads_of(kh)
        s_prev = s[:ATTN_BLOCK]
        if qb == 0:
            s_prev = s_prev + halo_bias
        c = jnp.where(from_prev, s_prev, s[ATTN_BLOCK:])
        sink = jnp.concatenate([jnp.full((1, ATTN_BLOCK), sink_ref[h] * LOG2_E, F32) for h in heads], axis=1)
        m = jnp.maximum(jnp.max(c, axis=0, keepdims=True), sink)
        p = jnp.exp2(c - m)
        denom = jnp.sum(p, axis=0, keepdims=True) + jnp.exp2(sink - m)
        pb = p.astype(BF16)
        p_both = jnp.concatenate([pb * keep_prev, pb * keep_cur], axis=0)
        vt_prev = vth_ref[...] if qb == 0 else vt_ref[:, (qb - 1) * ATTN_BLOCK:qb * ATTN_BLOCK]
        vt_both = jnp.concatenate([vt_prev[kh * HEAD_DIM:(kh + 1) * HEAD_DIM, :],
                                   vt_ref[kh * HEAD_DIM:(kh + 1) * HEAD_DIM, toks]], axis=1)
        ot = jnp.dot(vt_both, p_both, preferred_element_type=F32) * (1.0 / denom)
        for g, h in enumerate(heads):
            yat_ref[h * HEAD_DIM:(h + 1) * HEAD_DIM, toks] = ot[:, g * ATTN_BLOCK:(g + 1) * ATTN_BLOCK].astype(BF16)

    tm = x_ref.shape[0]
    n_chunks = D_MODEL // MIX_CHUNK
    chunk_cols = [slice(c * MIX_CHUNK, (c + 1) * MIX_CHUNK) for c in range(n_chunks)]
    yh = yh_ref[...]

    ph = []
    units_per_chunk = len(units) // n_chunks
    ahead = 2
    pending = [scores(*u) for u in units[:ahead]]
    for i, unit in enumerate(units):
        s_cur = pending.pop(0)
        if i + ahead < len(units):
            pending.append(scores(*units[i + ahead]))
        finish(*unit, s_cur)
        if i % units_per_chunk == 0:
            ph.append(jnp.dot(yh, wph_ref[:, chunk_cols[i // units_per_chunk]], preferred_element_type=F32))

    slab_rows = tm // LN_SLABS
    slabs_per_chunk = LN_SLABS // n_chunks
    yat = yat_ref[...]
    merged = []
    for c, cols in enumerate(chunk_cols):
        pa = lax.dot_general(yat, wpa_ref[:, cols], (((0,), (0,)), ((), ())), preferred_element_type=F32)
        m = ga_ref[:, cols].astype(F32) * pa + gh_ref[:, cols].astype(F32) * ph[c]
        zero = jnp.zeros((F32_SUBLANES, LANES), F32)
        for sl in range(c * slabs_per_chunk, (c + 1) * slabs_per_chunk):
            rows = slice(sl * slab_rows, (sl + 1) * slab_rows)
            ln = _layer_norm(y_ref[rows, :], g_ref[...], b_ref[...])
            o_ref[rows, :] = ln
            zero = zero + _zero_after(ln)
        top = m[0:F32_SUBLANES] + jnp.concatenate([zero] * (MIX_CHUNK // LANES), axis=1)
        merged.append(jnp.concatenate([top, m[F32_SUBLANES:]], axis=0).astype(BF16))
    merged = jnp.concatenate(merged, axis=1)
    for cols in chunk_cols:
        mix = jnp.dot(merged, wo_ref[:, cols], preferred_element_type=F32)
        y_ref[:, cols] = DEEPNORM_ALPHA * x_ref[:, cols] + mix


def _attn_mix(sinks, x2d, qt, k, vt, y_hgrn, gates, w_pa, w_ph, w_out, g, b, *, seq):
    n = x2d.shape[0]
    tm = ATTN_TILE_BLOCKS * ATTN_BLOCK
    tiles = n // tm
    cur = lambda i: jnp.minimum(i, tiles - 1)
    halo = lambda i: jnp.maximum(cur(i) * ATTN_TILE_BLOCKS - 1, 0)
    row = lambda w, col=0: pl.BlockSpec((tm, w), lambda i: (cur(i), col))
    sq = _resident((D_MODEL, D_MODEL))
    return pl.pallas_call(
        functools.partial(_attn_mix_body, tiles=tiles, tiles_per_seq=seq // tm),
        grid=(tiles + 1,),
        in_specs=[pl.BlockSpec(memory_space=pltpu.SMEM),
                  row(D_MODEL),
                  pl.BlockSpec((ATTN_WIDTH, tm), lambda i: (0, cur(i))),
                  row(KV_WIDTH),
                  pl.BlockSpec((ATTN_BLOCK, KV_WIDTH), lambda i: (halo(i), 0)),
                  pl.BlockSpec((KV_WIDTH, tm), lambda i: (0, cur(i))),
                  pl.BlockSpec((KV_WIDTH, ATTN_BLOCK), lambda i: (0, halo(i))),
                  row(HGRN_WIDTH), row(D_MODEL, 0), row(D_MODEL, 1), sq, sq, sq,
                  _resident((1, D_MODEL)), _resident((1, D_MODEL))],
        out_specs=pl.BlockSpec((tm, D_MODEL), lambda i: (jnp.maximum(i - 1, 0), 0)),
        out_shape=jax.ShapeDtypeStruct((n, D_MODEL), F32),
        scratch_shapes=[pltpu.VMEM((tm, D_MODEL), F32),
                        pltpu.VMEM((ATTN_WIDTH, tm), BF16)],
        compiler_params=_params("arbitrary"),
        name="attn_mix",
    )(sinks, x2d, qt, k, k, vt, vt, y_hgrn, gates, gates, w_pa, w_ph, w_out, g, b)


def _cumprod_rows(f):
    row = lax.broadcasted_iota(jnp.int32, (F32_SUBLANES, LANES), 0)
    blocks, carry = [], None
    for j in range(f.shape[0] // F32_SUBLANES):
        x = f[j * F32_SUBLANES:(j + 1) * F32_SUBLANES]
        shift = 1
        while shift < F32_SUBLANES:
            x = x * jnp.where(row >= shift, pltpu.roll(x, shift, axis=0), 1.0)
            shift *= 2
        if carry is not None:
            x = x * carry
        carry = x[F32_SUBLANES - 1:F32_SUBLANES, :]
        blocks.append(x)
    return jnp.concatenate(blocks, axis=0)


def _tile(n, want):
    t = min(n, want)
    assert n % t == 0, (n, t)
    return t


def kernel(x, ln1_g, ln1_b, ffn1_w1, ffn1_w3, ffn1_w2, ln2_g, ln2_b, w_in, b_in, attn_sinks, hgrn_lb_logits,
           hgrn_norm_g, w_proj_attn, w_proj_hgrn, w_out, ln3_g, ln3_b, ffn2_w1, ffn2_w3, ffn2_w2):
    batch, seq, d = x.shape
    assert d == D_MODEL and seq % (ATTN_TILE_BLOCKS * ATTN_BLOCK) == 0
    n = batch * seq
    tm = _tile(seq, 512)
    lb_all = jnp.cumsum(jax.nn.softmax(hgrn_lb_logits.astype(F32), axis=0), axis=0)
    rope = _rope_tables(seq)
    bf = lambda w: w.astype(BF16)
    h = x.reshape(n, d)
    for l in range(DEPTH):
        h = _ffn_ln(h, bf(0.5 * ffn1_w1[l]), bf(ffn1_w3[l]), bf(0.5 * ffn1_w2[l]), ln1_g[l:l + 1], ln1_b[l:l + 1], tm=tm)
        qt, vt, k, gates, y_hgrn = _in_proj(h, w_in[l], b_in[l], lb_all[l:l + 1], hgrn_norm_g[l:l + 1], rope,
                                            tm=tm, seq=seq)
        h = _attn_mix(attn_sinks[l], h, qt, k, vt, y_hgrn, gates, bf(w_proj_attn[l]), bf(w_proj_hgrn[l]),
                      bf(w_out[l]), ln2_g[l:l + 1], ln2_b[l:l + 1], seq=seq)
        h = _ffn_ln(h, bf(0.5 * ffn2_w1[l]), bf(ffn2_w3[l]), bf(0.5 * ffn2_w2[l]), ln3_g[l:l + 1], ln3_b[l:l + 1], tm=tm)
    return h.reshape(batch, seq, d)
```
